```python
import jax, jax.numpy as jnp
from jax import lax
import numpy as np

D_MODEL = 1024
BATCH = 2
SEQ = 8192
DEPTH = 2
DEC_BATCH = 32
DEC_SEQ = 8
PAST_LEN = 8192
PAGE_SIZE = 128

HEAD_DIM = 64
N_EVEN = (DEPTH + 1) // 2
N_ODD = DEPTH // 2
NSA_HEADS = 8
NSA_KV_HEADS = 2
NSA_GROUP = NSA_HEADS // NSA_KV_HEADS
NSA_CMP_BLOCK = 32
NSA_CMP_STRIDE = 16
NSA_CMP_HIDDEN = 2 * HEAD_DIM
NSA_SEL_BLOCK = 64
NSA_TOPN = 16
NSA_WINDOW = 512
SB_HEADS = 8
MOBA_HEADS = 16
MOBA_BLOCK = 256
MOBA_TOPK = 3
Q_BLOCK = 128
MOBA_Q_BLOCK = 32
D_FF = 2816
N_EXPERTS = 8
TOP_K = 2
D_FF_EXPERT = 3584
MOE_BLOCK = 128
PLE_DIM = 256
ROPE_THETA = 10000.0
RMS_EPS = 1e-6
NEG = -1e30

NSA_Q_W = NSA_HEADS * HEAD_DIM
NSA_KV_W = 2 * NSA_KV_HEADS * HEAD_DIM
NSA_IN = NSA_Q_W + 3 * NSA_KV_W + 3 * NSA_HEADS
SB_IN = 3 * SB_HEADS * HEAD_DIM
EVEN_IN = NSA_IN + SB_IN
EVEN_MIX = (NSA_HEADS + SB_HEADS) * HEAD_DIM
ODD_IN = 3 * MOBA_HEADS * HEAD_DIM
ODD_MIX = MOBA_HEADS * HEAD_DIM

kernel_name = 'nsa_stickbreak_moba_hybrid_step'


def rmsnorm(x, g):
    xf = x.astype(jnp.float32)
    y = xf * lax.rsqrt(jnp.mean(xf * xf, axis=-1, keepdims=True) + RMS_EPS)
    return (y * g.astype(jnp.float32)).astype(x.dtype)


def rope(x, pos):
    half = HEAD_DIM // 2
    inv = 1.0 / (ROPE_THETA ** (jnp.arange(half, dtype=jnp.float32) / half))
    ang = pos.astype(jnp.float32)[:, None] * inv[None, :]
    ang = ang.reshape((ang.shape[0],) + (1,) * (x.ndim - 3) + (half,))
    cos, sin = jnp.cos(ang), jnp.sin(ang)
    xf = x.astype(jnp.float32)
    x1, x2 = xf[..., :half], xf[..., half:]
    return jnp.concatenate([x1 * cos - x2 * sin, x2 * cos + x1 * sin], axis=-1).astype(x.dtype)


def masked_softmax(s, mask):
    p = jax.nn.softmax(jnp.where(mask, s, NEG), axis=-1)
    return jnp.where(mask, p, 0.0)


def swiglu(h, wg, wu, wd):
    return (jax.nn.silu(h @ wg) * (h @ wu)) @ wd


def gather_past(cache, li, page_table):
    rows = cache[li, page_table]
    n_seq, n_pages, page = rows.shape[:3]
    return rows.reshape((n_seq, n_pages * page) + rows.shape[3:])


def sweep(body, n_rows, block):
    out = lax.map(body, jnp.arange(n_rows // block, dtype=jnp.int32) * block)
    out = jnp.swapaxes(out, 0, 1)
    return out.reshape((out.shape[0], n_rows) + out.shape[3:])


def nsa_compress(rows, pe, w1, w2):
    b, l = rows.shape[:2]
    n_chunks = l // NSA_CMP_STRIDE
    ch = rows[:, :n_chunks * NSA_CMP_STRIDE].reshape(b, n_chunks, NSA_CMP_STRIDE, NSA_KV_HEADS, HEAD_DIM)
    ch = ch.transpose(0, 1, 3, 2, 4).reshape(b, n_chunks, NSA_KV_HEADS, NSA_CMP_STRIDE * HEAD_DIM)
    half = NSA_CMP_STRIDE * HEAD_DIM
    ha = ch @ w1[:half]
    hb = ch @ w1[half:]
    hid = jax.nn.silu(ha[:, :-1] + hb[:, 1:] + pe.reshape(-1) @ w1)
    return hid @ w2


def nsa_mixer(u, pos, past, pe, w1, w2):
    f32 = jnp.float32
    b, t = u.shape[:2]
    G, R = NSA_KV_HEADS, NSA_GROUP
    q = u[..., :NSA_Q_W].reshape(b, t, G, R, HEAD_DIM)
    kv = [u[..., NSA_Q_W + i * NSA_KV_W: NSA_Q_W + (i + 1) * NSA_KV_W].reshape(b, t, 2, G, HEAD_DIM) for i in range(3)]
    gates = jax.nn.sigmoid(u[..., NSA_Q_W + 3 * NSA_KV_W:].astype(f32)).reshape(b, t, G, R, 3)
    qr = rope(q, pos)
    cmp_rows = kv[0]
    sel_rows = jnp.stack([rope(kv[1][:, :, 0], pos), kv[1][:, :, 1]], axis=2)
    win_rows = jnp.stack([rope(kv[2][:, :, 0], pos), kv[2][:, :, 1]], axis=2)
    if past is None:
        cmp_ctx, sel_ctx = cmp_rows, sel_rows
    else:
        past_cmp, past_sel, past_win = past
        cmp_ctx = jnp.concatenate([past_cmp, cmp_rows], axis=1)
        sel_ctx = jnp.concatenate([past_sel, sel_rows], axis=1)
    L = sel_ctx.shape[1]
    kc = nsa_compress(cmp_ctx[:, :, 0], pe[0], w1[0], w2[0]).astype(f32)
    vc = nsa_compress(cmp_ctx[:, :, 1], pe[1], w1[1], w2[1]).astype(f32)
    n_c = kc.shape[1]
    c_start = jnp.arange(n_c) * NSA_CMP_STRIDE
    c_end = c_start + NSA_CMP_BLOCK - 1
    n_s = -(-L // NSA_SEL_BLOCK)
    s_start = jnp.arange(n_s) * NSA_SEL_BLOCK
    overlap = ((c_start[:, None] <= s_start[None, :] + NSA_SEL_BLOCK - 1)
               & (c_end[:, None] >= s_start[None, :])).astype(f32)
    sel_blk = jnp.pad(sel_ctx, ((0, 0), (0, n_s * NSA_SEL_BLOCK - L), (0, 0), (0, 0), (0, 0)))
    sel_blk = sel_blk.reshape(b, n_s, NSA_SEL_BLOCK, 2, G, HEAD_DIM)
    sel_k, sel_v = sel_blk[:, :, :, 0], sel_blk[:, :, :, 1]
    bi = jnp.arange(b)[:, None, None, None]
    gi = jnp.arange(G)[None, :, None, None]
    jb = jnp.arange(n_s)
    scale = HEAD_DIM ** -0.5

    def attend(q_c, qr_c, g_c, pos_c, wk, wv, wpos):
        tq = q_c.shape[1]
        s = jnp.einsum('bqgrd,bcgd->bgrqc', q_c.astype(f32), kc) * scale
        p = masked_softmax(s, c_end[None, :] <= pos_c[:, None])
        o_cmp = jnp.einsum('bgrqc,bcgd->bqgrd', p, vc)
        imp = jnp.einsum('bgrqc,cn->bgqn', p, overlap)
        blk = pos_c // NSA_SEL_BLOCK
        forced = (jb[None] == 0) | (jb[None] == blk[:, None]) | (jb[None] == blk[:, None] - 1)
        score = jnp.where(forced, jnp.inf, jnp.where(jb[None] <= blk[:, None], imp, -jnp.inf))
        if n_s < NSA_TOPN:
            score = jnp.pad(score, ((0, 0), (0, 0), (0, 0), (0, NSA_TOPN - n_s)), constant_values=-jnp.inf)
        top_v, top_i = lax.top_k(score, NSA_TOPN)
        ok = top_v > -jnp.inf
        top_i = jnp.minimum(top_i, n_s - 1)
        kg = sel_k[bi, top_i, :, gi].reshape(b, G, tq, -1, HEAD_DIM)
        vg = sel_v[bi, top_i, :, gi].reshape(b, G, tq, -1, HEAD_DIM)
        kpos = top_i[..., None] * NSA_SEL_BLOCK + jnp.arange(NSA_SEL_BLOCK)
        smask = (ok[..., None] & (kpos <= pos_c[None, None, :, None, None])).reshape(b, G, tq, -1)
        s = jnp.einsum('bqgrd,bgqkd->bgrqk', qr_c.astype(f32), kg.astype(f32)) * scale
        p = masked_softmax(s, smask[:, :, None])
        o_sel = jnp.einsum('bgrqk,bgqkd->bqgrd', p, vg.astype(f32))
        s = jnp.einsum('bqgrd,bkgd->bgrqk', qr_c.astype(f32), wk.astype(f32)) * scale
        wmask = (wpos[None] >= 0) & (wpos[None] <= pos_c[:, None]) & (pos_c[:, None] - wpos[None] < NSA_WINDOW)
        p = masked_softmax(s, wmask)
        o_win = jnp.einsum('bgrqk,bkgd->bqgrd', p, wv.astype(f32))
        out = g_c[..., 0:1] * o_cmp + g_c[..., 1:2] * o_sel + g_c[..., 2:3] * o_win
        return out.astype(u.dtype)

    if past is None:
        wk_pad = jnp.pad(win_rows, ((0, 0), (NSA_WINDOW, 0), (0, 0), (0, 0), (0, 0)))

        def body(s0):
            sl = lambda a: lax.dynamic_slice_in_dim(a, s0, Q_BLOCK, axis=1)
            band = lax.dynamic_slice_in_dim(wk_pad, s0, NSA_WINDOW + Q_BLOCK, axis=1)
            wpos = s0 - NSA_WINDOW + jnp.arange(NSA_WINDOW + Q_BLOCK)
            pos_c = lax.dynamic_slice_in_dim(pos, s0, Q_BLOCK)
            return attend(sl(q), sl(qr), sl(gates), pos_c, band[:, :, 0], band[:, :, 1], wpos)

        out = sweep(body, t, Q_BLOCK)
        win_state = win_rows[:, -min(NSA_WINDOW, t):]
    else:
        wctx = jnp.concatenate([past_win, win_rows], axis=1)
        wb = past_win.shape[1]
        wpos = pos[0] - wb + jnp.arange(wb + t)
        out = attend(q, qr, gates, pos, wctx[:, :, 0], wctx[:, :, 1], wpos)
        win_state = wctx[:, -wb:]
    return out.reshape(b, t, NSA_Q_W), cmp_rows, sel_rows, win_state


def sb_mixer(u, pos, past):
    f32 = jnp.float32
    b, t = u.shape[:2]
    qkv = u.reshape(b, t, 3, SB_HEADS, HEAD_DIM)
    q = qkv[:, :, 0]
    rows = qkv[:, :, 1:]
    ctx = rows if past is None else jnp.concatenate([past, rows], axis=1)
    k_ctx, v_ctx = ctx[:, :, 0].astype(f32), ctx[:, :, 1].astype(f32)
    kpos = jnp.arange(ctx.shape[1])
    scale = HEAD_DIM ** -0.5

    def attend(q_c, pos_c):
        z = jnp.einsum('bqhd,bkhd->bhqk', q_c.astype(f32), k_ctx) * scale
        mask = kpos[None, :] < pos_c[:, None]
        log_stay = jnp.where(mask, jax.nn.log_sigmoid(-z), 0.0)
        after = lax.cumsum(log_stay, axis=3, reverse=True) - log_stay
        a = jnp.where(mask, jnp.exp(jax.nn.log_sigmoid(z) + after), 0.0)
        return jnp.einsum('bhqk,bkhd->bqhd', a, v_ctx).astype(u.dtype)

    if past is None:
        out = sweep(lambda s0: attend(lax.dynamic_slice_in_dim(q, s0, Q_BLOCK, axis=1),
                                      lax.dynamic_slice_in_dim(pos, s0, Q_BLOCK)), t, Q_BLOCK)
    else:
        out = attend(q, pos)
    return out.reshape(b, t, SB_HEADS * HEAD_DIM), rows


def moba_mixer(u, pos, past):
    f32 = jnp.float32
    b, t = u.shape[:2]
    qkv = u.reshape(b, t, 3, MOBA_HEADS, HEAD_DIM)
    q = rope(qkv[:, :, 0], pos)
    rows = jnp.stack([rope(qkv[:, :, 1], pos), qkv[:, :, 2]], axis=2)
    parts = [rows] if past is None else [past, rows]
    L = sum(a.shape[1] for a in parts)
    nb = -(-L // MOBA_BLOCK)
    parts.append(jnp.zeros((b, nb * MOBA_BLOCK - L) + rows.shape[2:], rows.dtype))
    ctx = jnp.concatenate(parts, axis=1).reshape(b, nb, MOBA_BLOCK, 2, MOBA_HEADS, HEAD_DIM)
    kb, vb = ctx[:, :, :, 0], ctx[:, :, :, 1]
    k_mean = jnp.mean(kb.astype(f32), axis=2)
    bi = jnp.arange(b)[:, None, None, None]
    hi = jnp.arange(MOBA_HEADS)[None, :, None, None]
    scale = HEAD_DIM ** -0.5

    def attend(q_c, pos_c):
        tq = q_c.shape[1]
        qf = q_c.astype(f32)
        gate = jnp.einsum('bqhd,bnhd->bhqn', qf, k_mean)
        own = pos_c // MOBA_BLOCK
        gate = jnp.where(jnp.arange(nb)[None] < own[:, None], gate, -jnp.inf)
        if nb < MOBA_TOPK:
            gate = jnp.pad(gate, ((0, 0), (0, 0), (0, 0), (0, MOBA_TOPK - nb)), constant_values=-jnp.inf)
        top_v, top_i = lax.top_k(gate, MOBA_TOPK)
        top_i = jnp.minimum(top_i, nb - 1)
        idx = jnp.concatenate([top_i, jnp.broadcast_to(own[None, None, :, None], (b, MOBA_HEADS, tq, 1))], axis=-1)
        ok = jnp.concatenate([top_v > -jnp.inf, jnp.ones((b, MOBA_HEADS, tq, 1), bool)], axis=-1)
        kg = kb[bi, idx, :, hi].reshape(b, MOBA_HEADS, tq, -1, HEAD_DIM)
        vg = vb[bi, idx, :, hi].reshape(b, MOBA_HEADS, tq, -1, HEAD_DIM)
        kpos = idx[..., None] * MOBA_BLOCK + jnp.arange(MOBA_BLOCK)
        mask = (ok[..., None] & (kpos <= pos_c[None, None, :, None, None])).reshape(b, MOBA_HEADS, tq, -1)
        s = jnp.einsum('bqhd,bhqkd->bhqk', qf, kg.astype(f32)) * scale
        p = masked_softmax(s, mask)
        return jnp.einsum('bhqk,bhqkd->bqhd', p, vg.astype(f32)).astype(u.dtype)

    if past is None:
        out = sweep(lambda s0: attend(lax.dynamic_slice_in_dim(q, s0, MOBA_Q_BLOCK, axis=1),
                                      lax.dynamic_slice_in_dim(pos, s0, MOBA_Q_BLOCK)), t, MOBA_Q_BLOCK)
    else:
        out = attend(q, pos)
    return out.reshape(b, t, ODD_MIX), rows


def moe_ffn(h, w_router, wg, wu, wd):
    shp = h.shape
    x = h.reshape(-1, shp[-1])
    n = x.shape[0]
    logits = (x @ w_router).astype(jnp.float32)
    top_v, top_e = lax.top_k(logits, TOP_K)
    gates = jax.nn.softmax(top_v, axis=-1)
    flat_e = top_e.reshape(-1)
    flat_tok = jnp.repeat(jnp.arange(n, dtype=jnp.int32), TOP_K)
    flat_g = gates.reshape(-1)
    order = jnp.argsort(flat_e)
    e_s, tok_s, g_s = flat_e[order], flat_tok[order], flat_g[order]
    counts = jnp.zeros((N_EXPERTS,), jnp.int32).at[flat_e].add(1)
    padded = (counts + MOE_BLOCK - 1) // MOE_BLOCK * MOE_BLOCK
    start = jnp.cumsum(counts) - counts
    pend = jnp.cumsum(padded)
    pstart = pend - padded
    dest = pstart[e_s] + jnp.arange(n * TOP_K, dtype=jnp.int32) - start[e_s]
    n_blocks = -(-(n * TOP_K) // MOE_BLOCK) + N_EXPERTS
    rows = jnp.full((n_blocks * MOE_BLOCK,), n, jnp.int32).at[dest].set(tok_s)
    blk_e = jnp.minimum(jnp.searchsorted(pend, jnp.arange(n_blocks) * MOE_BLOCK, side='right'), N_EXPERTS - 1)
    xp = jnp.concatenate([x, jnp.zeros((1, x.shape[1]), x.dtype)], axis=0)
    xb = xp[rows].reshape(n_blocks, MOE_BLOCK, x.shape[1])
    yb = lax.map(lambda a: swiglu(a[0], wg[a[1]], wu[a[1]], wd[a[1]]), (xb, blk_e)).reshape(-1, x.shape[1])
    y = jnp.zeros_like(x).at[tok_s].add(yb[dest] * g_s[:, None].astype(x.dtype))
    return y.reshape(shp)


def trunk(x, p, pos, past, W):
    h = x
    cmp_l, sel_l, win_l, sb_l, moba_l = [], [], [], [], []
    for i in range(DEPTH):
        if i % 2 == 0:
            li = i // 2
            u = rmsnorm(h, W['norm_mix'][i]) @ W['w_in_even'][li]
            nsa_past, sb_past = None, None
            if past is not None:
                pt = past['page_table']
                nsa_past = (gather_past(past['cache_nsa_cmp'], li, pt), gather_past(past['cache_nsa_sel'], li, pt),
                            past['state_nsa_win'][li])
                sb_past = gather_past(past['cache_sb'], li, pt)
            o_a, cmp_r, sel_r, win_s = nsa_mixer(u[..., :NSA_IN], pos, nsa_past,
                                                 W['nsa_cmp_pe'][li], W['nsa_cmp_w1'][li], W['nsa_cmp_w2'][li])
            o_b, sb_r = sb_mixer(u[..., NSA_IN:], pos, sb_past)
            h = h + jnp.concatenate([o_a, o_b], axis=-1) @ W['w_out_even'][li]
            h = h + swiglu(rmsnorm(h, W['norm_ffn'][i]), W['w_ffn_gate'][li], W['w_ffn_up'][li], W['w_ffn_down'][li])
            cmp_l.append(cmp_r); sel_l.append(sel_r); win_l.append(win_s); sb_l.append(sb_r)
        else:
            li = i // 2
            u = rmsnorm(h, W['norm_mix'][i]) @ W['w_in_odd'][li]
            moba_past = None if past is None else gather_past(past['cache_moba'], li, past['page_table'])
            o_c, moba_r = moba_mixer(u, pos, moba_past)
            h = h + o_c @ W['w_out_odd'][li]
            h = h + moe_ffn(rmsnorm(h, W['norm_ffn'][i]), W['w_router'][li], W['w_exp_gate'][li],
                            W['w_exp_up'][li], W['w_exp_down'][li])
            moba_l.append(moba_r)
        gate = jax.nn.sigmoid(rmsnorm(h, W['norm_ple'][i]) @ W['w_ple_gate'][i])
        h = h + gate * (p[i] @ W['w_ple_proj'][i])
    y = rmsnorm(h, W['norm_final'])
    return (y, jnp.stack(cmp_l), jnp.stack(sel_l), jnp.stack(win_l), jnp.stack(sb_l), jnp.stack(moba_l))


def setup_inputs(seed: int = 0) -> dict:
    key = jax.random.key(seed)
    ks = iter(jax.random.split(key, 32))
    nrm = lambda shape, scale: jax.random.normal(next(ks), shape, jnp.float32) * scale
    gain = lambda shape: 1.0 + nrm(shape, 0.02)
    n_pages = PAST_LEN // PAGE_SIZE
    n_used = DEC_BATCH * n_pages
    n_pool = n_used + max(1, n_used // 4)
    win_buf = min(NSA_WINDOW, PAST_LEN)
    return {
        'x_prompt': nrm((BATCH, SEQ, D_MODEL), 1.0),
        'x_sample': nrm((DEC_BATCH, DEC_SEQ, D_MODEL), 1.0),
        'cache_nsa_cmp': nrm((N_EVEN, n_pool, PAGE_SIZE, 2, NSA_KV_HEADS, HEAD_DIM), 1.0),
        'cache_nsa_sel': nrm((N_EVEN, n_pool, PAGE_SIZE, 2, NSA_KV_HEADS, HEAD_DIM), 1.0),
        'state_nsa_win': nrm((N_EVEN, DEC_BATCH, win_buf, 2, NSA_KV_HEADS, HEAD_DIM), 1.0),
        'cache_sb': nrm((N_EVEN, n_pool, PAGE_SIZE, 2, SB_HEADS, HEAD_DIM), 1.0),
        'cache_moba': nrm((N_ODD, n_pool, PAGE_SIZE, 2, MOBA_HEADS, HEAD_DIM), 1.0),
        'page_table': jax.random.permutation(next(ks), n_pool)[:n_used].reshape(DEC_BATCH, n_pages).astype(jnp.int32),
        'p_prompt': nrm((DEPTH, BATCH, SEQ, PLE_DIM), 1.0),
        'p_sample': nrm((DEPTH, DEC_BATCH, DEC_SEQ, PLE_DIM), 1.0),
        'norm_mix': gain((DEPTH, D_MODEL)),
        'norm_ffn': gain((DEPTH, D_MODEL)),
        'norm_ple': gain((DEPTH, D_MODEL)),
        'norm_final': gain((D_MODEL,)),
        'w_in_even': nrm((N_EVEN, D_MODEL, EVEN_IN), D_MODEL ** -0.5),
        'w_out_even': nrm((N_EVEN, EVEN_MIX, D_MODEL), EVEN_MIX ** -0.5),
        'nsa_cmp_pe': nrm((N_EVEN, 2, NSA_CMP_BLOCK, HEAD_DIM), 0.5),
        'nsa_cmp_w1': nrm((N_EVEN, 2, NSA_CMP_BLOCK * HEAD_DIM, NSA_CMP_HIDDEN), (NSA_CMP_BLOCK * HEAD_DIM) ** -0.5),
        'nsa_cmp_w2': nrm((N_EVEN, 2, NSA_CMP_HIDDEN, HEAD_DIM), NSA_CMP_HIDDEN ** -0.5),
        'w_ffn_gate': nrm((N_EVEN, D_MODEL, D_FF), D_MODEL ** -0.5),
        'w_ffn_up': nrm((N_EVEN, D_MODEL, D_FF), D_MODEL ** -0.5),
        'w_ffn_down': nrm((N_EVEN, D_FF, D_MODEL), D_FF ** -0.5),
        'w_in_odd': nrm((N_ODD, D_MODEL, ODD_IN), D_MODEL ** -0.5),
        'w_out_odd': nrm((N_ODD, ODD_MIX, D_MODEL), ODD_MIX ** -0.5),
        'w_router': nrm((N_ODD, D_MODEL, N_EXPERTS), D_MODEL ** -0.5),
        'w_exp_gate': nrm((N_ODD, N_EXPERTS, D_MODEL, D_FF_EXPERT), D_MODEL ** -0.5),
        'w_exp_up': nrm((N_ODD, N_EXPERTS, D_MODEL, D_FF_EXPERT), D_MODEL ** -0.5),
        'w_exp_down': nrm((N_ODD, N_EXPERTS, D_FF_EXPERT, D_MODEL), D_FF_EXPERT ** -0.5),
        'w_ple_proj': nrm((DEPTH, PLE_DIM, D_MODEL), PLE_DIM ** -0.5),
        'w_ple_gate': nrm((DEPTH, D_MODEL, D_MODEL), D_MODEL ** -0.5),
    }


def reference(x_prompt, x_sample, cache_nsa_cmp, cache_nsa_sel, state_nsa_win, cache_sb, cache_moba, page_table,
              p_prompt, p_sample, norm_mix, norm_ffn, norm_ple, norm_final, w_in_even, w_out_even,
              nsa_cmp_pe, nsa_cmp_w1, nsa_cmp_w2, w_ffn_gate, w_ffn_up, w_ffn_down, w_in_odd, w_out_odd,
              w_router, w_exp_gate, w_exp_up, w_exp_down, w_ple_proj, w_ple_gate):
    W = {'norm_mix': norm_mix, 'norm_ffn': norm_ffn, 'norm_ple': norm_ple, 'norm_final': norm_final,
         'w_in_even': w_in_even, 'w_out_even': w_out_even, 'nsa_cmp_pe': nsa_cmp_pe, 'nsa_cmp_w1': nsa_cmp_w1,
         'nsa_cmp_w2': nsa_cmp_w2, 'w_ffn_gate': w_ffn_gate, 'w_ffn_up': w_ffn_up, 'w_ffn_down': w_ffn_down,
         'w_in_odd': w_in_odd, 'w_out_odd': w_out_odd, 'w_router': w_router, 'w_exp_gate': w_exp_gate,
         'w_exp_up': w_exp_up, 'w_exp_down': w_exp_down, 'w_ple_proj': w_ple_proj, 'w_ple_gate': w_ple_gate}
    past = {'cache_nsa_cmp': cache_nsa_cmp, 'cache_nsa_sel': cache_nsa_sel, 'state_nsa_win': state_nsa_win,
            'cache_sb': cache_sb, 'cache_moba': cache_moba, 'page_table': page_table}
    pos_p = jnp.arange(x_prompt.shape[1], dtype=jnp.int32)
    past_len = page_table.shape[1] * cache_sb.shape[2]
    pos_s = past_len + jnp.arange(x_sample.shape[1], dtype=jnp.int32)
    y_prompt, cmp_p, sel_p, win_p, sb_p, moba_p = trunk(x_prompt, p_prompt, pos_p, None, W)
    y_sample, cmp_s, sel_s, win_s, sb_s, moba_s = trunk(x_sample, p_sample, pos_s, past, W)
    return (y_prompt, y_sample, cmp_p, cmp_s, sel_p, sel_s, win_p, win_s, sb_p, sb_s, moba_p, moba_s)
```

```python
import functools

import jax
import jax.numpy as jnp
from jax import lax
from jax.experimental import pallas as pl
from jax.experimental.pallas import tpu as pltpu

F32 = jnp.float32
BF16 = jnp.bfloat16

LANES = 128
HEAD_DIM = 64
PAGE = 128
RMS_EPS = 1e-6
ROPE_THETA = 10000.0
NEG = -1e30

NSA_HEADS = 8
NSA_GROUP = 4
NSA_STRIDE = 16
NSA_HID = 128
NSA_SEL_BLOCK = 64
NSA_PICKS = 13
NSA_WINDOW = 512
SB_HEADS = 8
MOBA_HEADS = 16
MOBA_BLOCK = 256
MOBA_TOPK = 3
N_EXPERTS = 8

VMEM_LIMIT = 56 * 1024 * 1024


def _cparams(sem):
    return pltpu.CompilerParams(dimension_semantics=sem, vmem_limit_bytes=VMEM_LIMIT)


def _dot(a, b):
    return jnp.dot(a, b, preferred_element_type=F32)


def _dot_nt(a, b):
    return lax.dot_general(a, b, (((1,), (1,)), ((), ())), preferred_element_type=F32)


def _split(x):
    hi = x.astype(BF16)
    lo = (x - hi.astype(F32)).astype(BF16)
    return hi, lo


def _rmsnorm(x, g):
    return x * lax.rsqrt(jnp.mean(x * x, axis=-1, keepdims=True) + RMS_EPS) * g


def _sigmoid(x):
    return 1.0 / (1.0 + jnp.exp(-x))


def _rope_tile(y, cos, sa, sb):
    return y * cos + pltpu.roll(y, 32, 1) * sa + pltpu.roll(y, 96, 1) * sb


def _proj_kernel(x_ref, g_ref, cos_ref, sa_ref, sb_ref, *refs, segs):
    n_seg = len(segs)
    w_refs = refs[:n_seg]
    o_refs = list(refs[n_seg:])
    nb = _rmsnorm(x_ref[...], g_ref[...]).astype(BF16)
    cos, sa, sb = cos_ref[...], sa_ref[...], sb_ref[...]
    for w_ref, kinds in zip(w_refs, segs):
        y = _dot(nb, w_ref[...])
        if kinds == "plain":
            o_refs.pop(0)[...] = y
            continue
        if kinds == "rope_dual":
            o_refs.pop(0)[...] = y
            kinds = ("rope",) * (y.shape[1] // LANES)
        o_ref = o_refs.pop(0)
        for t, kind in enumerate(kinds):
            yt = y[:, t * LANES:(t + 1) * LANES]
            if kind == "rope":
                yt = _rope_tile(yt, cos, sa, sb)
            elif kind == "sigmoid":
                yt = _sigmoid(yt)
            o_ref[:, t * LANES:(t + 1) * LANES] = yt


def _proj(x, g, tabs, weights, segs, tm):
    n, d = x.shape
    nblk = tabs[0].shape[0] // tm
    out_shape, out_specs = [], []
    for w, kinds in zip(weights, segs):
        for _ in range(2 if kinds == "rope_dual" else 1):
            out_shape.append(jax.ShapeDtypeStruct((n, w.shape[1]), F32))
            out_specs.append(pl.BlockSpec((tm, w.shape[1]), lambda i: (i, 0)))
    tab_spec = pl.BlockSpec((tm, LANES), lambda i: (i % nblk, 0))
    return pl.pallas_call(
        functools.partial(_proj_kernel, segs=tuple(segs)),
        grid=(n // tm,),
        in_specs=[pl.BlockSpec((tm, d), lambda i: (i, 0)),
                  pl.BlockSpec((1, d), lambda i: (0, 0)),
                  tab_spec, tab_spec, tab_spec]
                 + [pl.BlockSpec(w.shape, lambda i: (0, 0)) for w in weights],
        out_specs=out_specs,
        out_shape=out_shape,
        compiler_params=_cparams(("parallel",)),
        name="norm_proj",
    )(x, g, *tabs, *weights)


def _outproj_kernel(h_ref, a_ref, b_ref, wa_ref, wb_ref, o_ref):
    o_ref[...] = (h_ref[...] + _dot(a_ref[...].astype(BF16), wa_ref[...])
                  + _dot(b_ref[...].astype(BF16), wb_ref[...]))


def _outproj(h, a, b, wa, wb, tm):
    n, d = h.shape
    row = lambda w: pl.BlockSpec((tm, w), lambda i: (i, 0))
    full = lambda w: pl.BlockSpec(w.shape, lambda i: (0, 0))
    return pl.pallas_call(
        _outproj_kernel,
        grid=(n // tm,),
        in_specs=[row(d), row(a.shape[1]), row(b.shape[1]), full(wa), full(wb)],
        out_specs=row(d),
        out_shape=jax.ShapeDtypeStruct((n, d), F32),
        compiler_params=_cparams(("parallel",)),
        name="out_proj",
    )(h, a, b, wa, wb)


def _ffn_kernel(h_ref, g_ref, wg_ref, wu_ref, wd_ref, o_ref, n_scr, acc_scr):
    f = pl.program_id(1)

    @pl.when(f == 0)
    def _():
        n_scr[...] = _rmsnorm(h_ref[...], g_ref[...]).astype(BF16)
        acc_scr[...] = jnp.zeros_like(acc_scr)

    nb = n_scr[...]
    gate = _dot(nb, wg_ref[...])
    up = _dot(nb, wu_ref[...])
    hid = (gate * _sigmoid(gate) * up).astype(BF16)
    acc_scr[...] += _dot(hid, wd_ref[...])

    @pl.when(f == pl.num_programs(1) - 1)
    def _():
        o_ref[...] = h_ref[...] + acc_scr[...]


def _ffn(h, g, wg, wu, wd, tm, tf):
    n, d = h.shape
    dff = wg.shape[1]
    return pl.pallas_call(
        _ffn_kernel,
        grid=(n // tm, dff // tf),
        in_specs=[pl.BlockSpec((tm, d), lambda i, f: (i, 0)),
                  pl.BlockSpec((1, d), lambda i, f: (0, 0)),
                  pl.BlockSpec((d, tf), lambda i, f: (0, f)),
                  pl.BlockSpec((d, tf), lambda i, f: (0, f)),
                  pl.BlockSpec((tf, d), lambda i, f: (f, 0))],
        out_specs=pl.BlockSpec((tm, d), lambda i, f: (i, 0)),
        out_shape=jax.ShapeDtypeStruct((n, d), F32),
        scratch_shapes=[pltpu.VMEM((tm, d), BF16), pltpu.VMEM((tm, d), F32)],
        compiler_params=_cparams(("parallel", "arbitrary")),
        name="swiglu_ffn",
    )(h, g, wg, wu, wd)


def _ple_kernel(h_ref, p_ref, g_ref, gf_ref, wg_ref, wp_ref, o_ref, *, final):
    h = h_ref[...]
    gate = _sigmoid(_dot(_rmsnorm(h, g_ref[...]).astype(BF16), wg_ref[...]))
    out = h + gate * _dot(p_ref[...].astype(BF16), wp_ref[...])
    if final:
        out = _rmsnorm(out, gf_ref[...])
    o_ref[...] = out


def _ple(h, p, g, gf, wg, wp, tm, final):
    n, d = h.shape
    row = lambda w: pl.BlockSpec((tm, w), lambda i: (i, 0))
    full = lambda a: pl.BlockSpec(a.shape, lambda i: (0, 0))
    return pl.pallas_call(
        functools.partial(_ple_kernel, final=final),
        grid=(n // tm,),
        in_specs=[row(d), row(p.shape[1]), full(g), full(gf), full(wg), full(wp)],
        out_specs=row(d),
        out_shape=jax.ShapeDtypeStruct((n, d), F32),
        compiler_params=_cparams(("parallel",)),
        name="ple",
    )(h, p, g, gf, wg, wp)


def _moe_kernel(h_ref, g_ref, wr_hi_ref, wr_lo_ref, wg_ref, wu_ref, wd_ref, o_ref,
                n_scr, gw_scr, acc_scr):
    e = pl.program_id(1)
    f = pl.program_id(2)
    lane = lax.broadcasted_iota(jnp.int32, (1, LANES), 1)

    @pl.when((e == 0) & (f == 0))
    def _():
        n = _rmsnorm(h_ref[...], g_ref[...])
        n_hi, n_lo = _split(n)
        n_scr[...] = n_hi
        logits = (_dot(n_hi, wr_hi_ref[...]) + _dot(n_lo, wr_hi_ref[...])
                  + _dot(n_hi, wr_lo_ref[...]))
        logits = jnp.where(lane < N_EXPERTS, logits, NEG)
        lane_f = lane.astype(F32)
        v1 = jnp.max(logits, axis=1, keepdims=True)
        i1 = jnp.min(jnp.where(logits == v1, lane_f, float(LANES)), axis=1, keepdims=True)
        rest = jnp.where(lane_f == i1, NEG, logits)
        v2 = jnp.max(rest, axis=1, keepdims=True)
        i2 = jnp.min(jnp.where(rest == v2, lane_f, float(LANES)), axis=1, keepdims=True)
        e2 = jnp.exp(v2 - v1)
        g1 = 1.0 / (1.0 + e2)
        gw_scr[...] = jnp.where(lane_f == i1, g1, jnp.where(lane_f == i2, e2 * g1, 0.0))
        acc_scr[...] = jnp.zeros_like(acc_scr)

    nb = n_scr[...]
    gate = _dot(nb, wg_ref[0])
    up = _dot(nb, wu_ref[0])
    hid = (gate * _sigmoid(gate) * up).astype(BF16)
    col = jnp.sum(jnp.where(lane == e, gw_scr[...], 0.0), axis=1, keepdims=True)
    acc_scr[...] += col * _dot(hid, wd_ref[0])

    @pl.when((e == pl.num_programs(1) - 1) & (f == pl.num_programs(2) - 1))
    def _():
        o_ref[...] = h_ref[...] + acc_scr[...]


def _moe(h, g, wr_hi, wr_lo, wg, wu, wd, tm, tf):
    n, d = h.shape
    n_e, _, dff = wg.shape
    return pl.pallas_call(
        _moe_kernel,
        grid=(n // tm, n_e, dff // tf),
        in_specs=[pl.BlockSpec((tm, d), lambda i, e, f: (i, 0)),
                  pl.BlockSpec((1, d), lambda i, e, f: (0, 0)),
                  pl.BlockSpec(wr_hi.shape, lambda i, e, f: (0, 0)),
                  pl.BlockSpec(wr_lo.shape, lambda i, e, f: (0, 0)),
                  pl.BlockSpec((1, d, tf), lambda i, e, f: (e, 0, f)),
                  pl.BlockSpec((1, d, tf), lambda i, e, f: (e, 0, f)),
                  pl.BlockSpec((1, tf, d), lambda i, e, f: (e, f, 0))],
        out_specs=pl.BlockSpec((tm, d), lambda i, e, f: (i, 0)),
        out_shape=jax.ShapeDtypeStruct((n, d), F32),
        scratch_shapes=[pltpu.VMEM((tm, d), BF16), pltpu.VMEM((tm, LANES), F32),
                        pltpu.VMEM((tm, d), F32)],
        compiler_params=_cparams(("parallel", "arbitrary", "arbitrary")),
        name="moe_ffn",
    )(h, g, wr_hi, wr_lo, wg, wu, wd)


def _gather_kernel(pt_ref, *refs, n_pg):
    del pt_ref
    page_refs, tail_ref, o_ref = refs[:n_pg], refs[n_pg], refs[n_pg + 1]
    s = pl.program_id(1)
    last = pl.num_programs(1) - 1

    @pl.when(s < last)
    def _():
        for i, p_ref in enumerate(page_refs):
            o_ref[0, i * PAGE:(i + 1) * PAGE, :] = p_ref[0]

    @pl.when(s == last)
    def _():
        o_ref[0, 0:tail_ref.shape[1], :] = tail_ref[0]


def _gather_ctx(pool, page_table, tail, n_pg):
    b, n_pages = page_table.shape
    w = pool.shape[2]
    n_tail = tail.shape[1]
    steps = n_pages // n_pg

    def page_spec(i):
        return pl.BlockSpec(
            (1, PAGE, w),
            lambda bi, s, pt: (pt[bi, jnp.minimum(s, steps - 1) * n_pg + i], 0, 0))

    return pl.pallas_call(
        functools.partial(_gather_kernel, n_pg=n_pg),
        grid_spec=pltpu.PrefetchScalarGridSpec(
            num_scalar_prefetch=1,
            grid=(b, steps + 1),
            in_specs=[page_spec(i) for i in range(n_pg)]
                     + [pl.BlockSpec((1, n_tail, w), lambda bi, s, pt: (bi, 0, 0))],
            out_specs=pl.BlockSpec((1, n_pg * PAGE, w), lambda bi, s, pt: (bi, s, 0)),
        ),
        out_shape=jax.ShapeDtypeStruct((b, n_pages * PAGE + n_tail, w), F32),
        compiler_params=_cparams(("parallel", "arbitrary")),
        name="paged_gather",
    )(page_table, *([pool] * n_pg), tail)


def _compress_kernel(pt_ref, *refs, n_pg):
    del pt_ref
    k_refs, v_refs = refs[:n_pg], refs[n_pg:2 * n_pg]
    (w1k_ref, w1v_ref, w2_ref, pe_ref, w1_ref, kc_ref, vc_ref,
     ak_scr, av_scr) = refs[2 * n_pg:]
    s = pl.program_id(1)
    n_ch = ak_scr.shape[0]
    per_page = PAGE // NSA_STRIDE

    for p_refs, a_scr in ((k_refs, ak_scr), (v_refs, av_scr)):
        for i, p_ref in enumerate(p_refs):
            base = pl.multiple_of((s * n_pg + i) * per_page, per_page)
            for j in range(NSA_STRIDE):
                a_scr[pl.ds(base, per_page), j * LANES:(j + 1) * LANES] = (
                    p_ref[0, pl.ds(j, per_page, stride=NSA_STRIDE), :])

    @pl.when(s == pl.num_programs(1) - 1)
    def _():
        for kv, (a_scr, w1p_ref, o_ref) in enumerate(((ak_scr, w1k_ref, kc_ref),
                                                      (av_scr, w1v_ref, vc_ref))):
            r = _dot(a_scr[...].astype(BF16), w1p_ref[...])
            bias = _dot(pe_ref[kv], w1_ref[kv])[0:1, :]
            hid = []
            for g in range(2):
                a = r[:, (2 * g) * NSA_HID:(2 * g + 1) * NSA_HID]
                b_next = pltpu.roll(r[:, (2 * g + 1) * NSA_HID:(2 * g + 2) * NSA_HID],
                                    n_ch - 1, 0)
                pre = a + b_next + bias
                hid.append(pre * _sigmoid(pre))
            hid = jnp.concatenate(hid, axis=1).astype(BF16)
            o_ref[0] = _dot(hid, w2_ref[kv])


def _compress(pool, page_table, w1k, w1v, w2, pe, w1, n_pg):
    b, n_pages = page_table.shape
    n_ch = n_pages * (PAGE // NSA_STRIDE)
    steps = n_pages // n_pg
    full = lambda a: pl.BlockSpec(a.shape, lambda bi, s, pt: (0,) * a.ndim)

    def page_spec(i, kv):
        return pl.BlockSpec((1, PAGE, LANES), lambda bi, s, pt: (pt[bi, s * n_pg + i], 0, kv))

    out_spec = pl.BlockSpec((1, n_ch, LANES), lambda bi, s, pt: (bi, 0, 0))
    return pl.pallas_call(
        functools.partial(_compress_kernel, n_pg=n_pg),
        grid_spec=pltpu.PrefetchScalarGridSpec(
            num_scalar_prefetch=1,
            grid=(b, steps),
            in_specs=[page_spec(i, kv) for kv in range(2) for i in range(n_pg)]
                     + [full(w1k), full(w1v), full(w2), full(pe), full(w1)],
            out_specs=[out_spec, out_spec],
            scratch_shapes=[pltpu.VMEM((n_ch, NSA_STRIDE * LANES), F32),
                            pltpu.VMEM((n_ch, NSA_STRIDE * LANES), F32)],
        ),
        out_shape=[jax.ShapeDtypeStruct((b, n_ch, LANES), F32)] * 2,
        compiler_params=_cparams(("parallel", "arbitrary")),
        name="nsa_compress",
    )(page_table, *([pool] * (2 * n_pg)), w1k, w1v, w2, pe, w1)


def _flash_step(state, s, valid, v):
    m, l, acc = state
    s = jnp.where(valid, s, NEG)
    m_new = jnp.maximum(m, jnp.max(s, axis=1, keepdims=True))
    alpha = jnp.exp(m - m_new)
    p = jnp.where(valid, jnp.exp(s - m_new), 0.0)
    l = alpha * l + jnp.sum(p, axis=1, keepdims=True)
    acc = alpha * acc + _dot(p.astype(BF16), v)
    return m_new, l, acc


def _flash_init(rows):
    return (jnp.full((rows, 1), NEG, F32), jnp.zeros((rows, 1), F32),
            jnp.zeros((rows, LANES), F32))


def _flash_out(state):
    _, l, acc = state
    return jnp.where(l > 0.0, acc / jnp.where(l > 0.0, l, 1.0), 0.0)


def _stack(x, times):
    return jnp.concatenate([x] * times, axis=0)


N_WIN_TILES = NSA_WINDOW // LANES + 1


def _nsa_kernel(qp_ref, qr_ref, gt_ref, kc_ref, vc_ref, sel_ref, ov_ref, ex_ref, *refs,
                tq, tk, pos0, wpos0):
    win_refs, o_ref = refs[:N_WIN_TILES], refs[N_WIN_TILES]
    i = pl.program_id(1)
    q0 = pos0 + i * tq
    scale = HEAD_DIM ** -0.5
    r4 = NSA_GROUP
    qpos = q0 + lax.broadcasted_iota(jnp.int32, (tq, 1), 0)
    qpos4 = _stack(qpos, r4)
    qblk = qpos // NSA_SEL_BLOCK
    lane = lax.broadcasted_iota(jnp.int32, (1, LANES), 1)
    lane_f = lane.astype(F32)
    n_ch = kc_ref.shape[1]
    c_end = lax.broadcasted_iota(jnp.int32, (1, n_ch), 1) * NSA_STRIDE + (2 * NSA_STRIDE - 1)
    kc = kc_ref[0].astype(BF16)
    vc = vc_ref[0].astype(BF16)
    gates = gt_ref[0]
    n_sel_tiles = (q0 + tq - 1) // tk + 1
    win_start = (q0 // LANES) * LANES - NSA_WINDOW

    outs = []
    for g in range(2):
        heads = range(g * r4, (g + 1) * r4)
        q_c = jnp.concatenate([qp_ref[0, :, h * LANES:(h + 1) * LANES] for h in heads], axis=0)
        q_r = jnp.concatenate([qr_ref[0, :, h * LANES:(h + 1) * LANES] for h in heads], axis=0)
        q_c = (q_c * scale).astype(BF16)
        q_r = (q_r * scale).astype(BF16)

        s = _dot_nt(q_c, kc)
        valid = c_end <= qpos4
        s = jnp.where(valid, s, NEG)
        e = jnp.where(valid, jnp.exp(s - jnp.max(s, axis=1, keepdims=True)), 0.0)
        l = jnp.sum(e, axis=1, keepdims=True)
        p = jnp.where(l > 0.0, e / jnp.where(l > 0.0, l, 1.0), 0.0)
        o_cmp = _dot(p.astype(BF16), vc)

        p_sum = p[0:tq]
        for r in range(1, r4):
            p_sum = p_sum + p[r * tq:(r + 1) * tq]
        p_hi, p_lo = _split(p_sum)
        imp = _dot(p_hi, ov_ref[...]) + _dot(p_lo, ov_ref[...])

        cand = (lane <= qblk) & (lane != 0) & (lane != qblk) & (lane != qblk - 1)
        work = jnp.where(cand, imp, -1.0)
        picked = jnp.zeros((tq, LANES), F32)
        for _ in range(NSA_PICKS):
            top = jnp.max(work, axis=1, keepdims=True)
            first = jnp.min(jnp.where(work == top, lane_f, float(LANES)), axis=1,
                            keepdims=True)
            hit = (lane_f == first) & (top >= 0.0)
            picked = jnp.where(hit, 1.0, picked)
            work = jnp.where(hit, -1.0, work)
        picked = picked.astype(BF16)

        def sel_body(t, state):
            k0 = pl.multiple_of(t * tk, tk)
            k = sel_ref[0, pl.ds(k0, tk), 0:LANES].astype(BF16)
            v = sel_ref[0, pl.ds(k0, tk), LANES:2 * LANES].astype(BF16)
            kpos = k0 + lax.broadcasted_iota(jnp.int32, (1, tk), 1)
            kblk = kpos // NSA_SEL_BLOCK
            chosen = _dot(picked, ex_ref[t]) > 0.5
            forced = (kblk == 0) | (kblk == qblk) | (kblk == qblk - 1)
            keep = jnp.where((chosen | forced) & (kpos <= qpos), 1.0, 0.0)
            return _flash_step(state, _dot_nt(q_r, k), _stack(keep, r4) > 0.5, v)

        o_sel = _flash_out(lax.fori_loop(0, n_sel_tiles, sel_body, _flash_init(r4 * tq)))

        state = _flash_init(r4 * tq)
        for jt, w_ref in enumerate(win_refs):
            k = w_ref[0, :, 0:LANES].astype(BF16)
            v = w_ref[0, :, LANES:2 * LANES].astype(BF16)
            wpos = win_start + jt * LANES + lane
            keep = jnp.where((wpos >= wpos0) & (wpos <= qpos) & (qpos - wpos < NSA_WINDOW),
                             1.0, 0.0)
            state = _flash_step(state, _dot_nt(q_r, k), _stack(keep, r4) > 0.5, v)
        o_win = _flash_out(state)

        for r, h in enumerate(heads):
            rows = slice(r * tq, (r + 1) * tq)
            o = (gates[:, 3 * h:3 * h + 1] * o_cmp[rows]
                 + gates[:, 3 * h + 1:3 * h + 2] * o_sel[rows]
                 + gates[:, 3 * h + 2:3 * h + 3] * o_win[rows])
            if h % 2 != g:
                o = pltpu.roll(o, HEAD_DIM, 1)
            outs.append(o)

    for pair in range(NSA_HEADS // 2):
        o_ref[0, :, pair * LANES:(pair + 1) * LANES] = jnp.where(
            lane < HEAD_DIM, outs[2 * pair], outs[2 * pair + 1])


def _nsa_attn(qp, qr, gates, kc, vc, sel_ctx, win_ctx, ov, ex, tq, tk, pos0, wpos0):
    b, t_q, _ = qp.shape
    t_c = sel_ctx.shape[1]
    n_win_tiles_total = win_ctx.shape[1] // LANES
    tile0 = pos0 // LANES - NSA_WINDOW // LANES - wpos0 // LANES

    def win_spec(jt):
        def idx(bi, i):
            t = tile0 + (i * tq) // LANES + jt
            return (bi, jnp.clip(t, 0, n_win_tiles_total - 1), 0)
        return pl.BlockSpec((1, LANES, 2 * LANES), idx)

    qspec = pl.BlockSpec((1, tq, NSA_HEADS * LANES), lambda bi, i: (bi, i, 0))
    return pl.pallas_call(
        functools.partial(_nsa_kernel, tq=tq, tk=tk, pos0=pos0, wpos0=wpos0),
        grid=(b, t_q // tq),
        in_specs=[qspec, qspec,
                  pl.BlockSpec((1, tq, LANES), lambda bi, i: (bi, i, 0)),
                  pl.BlockSpec((1,) + kc.shape[1:], lambda bi, i: (bi, 0, 0)),
                  pl.BlockSpec((1,) + vc.shape[1:], lambda bi, i: (bi, 0, 0)),
                  pl.BlockSpec((1, t_c, 2 * LANES), lambda bi, i: (bi, 0, 0)),
                  pl.BlockSpec(ov.shape, lambda bi, i: (0, 0)),
                  pl.BlockSpec(ex.shape, lambda bi, i: (0, 0, 0))]
                 + [win_spec(jt) for jt in range(N_WIN_TILES)],
        out_specs=pl.BlockSpec((1, tq, NSA_HEADS * HEAD_DIM), lambda bi, i: (bi, i, 0)),
        out_shape=jax.ShapeDtypeStruct((b, t_q, NSA_HEADS * HEAD_DIM), F32),
        compiler_params=_cparams(("parallel", "arbitrary")),
        name="nsa_attn",
    )(qp, qr, gates, kc, vc, sel_ctx, ov, ex, *([win_ctx] * N_WIN_TILES))


def _sb_kernel(q_ref, k_ref, v_ref, tri_ref, o_ref, *, tq, tk, pos0):
    i = pl.program_id(2)
    q0 = pos0 + i * tq
    scale = HEAD_DIM ** -0.5
    qpos = q0 + lax.broadcasted_iota(jnp.int32, (tq, 1), 0)
    lane = lax.broadcasted_iota(jnp.int32, (1, LANES), 1)
    n_tiles = (q0 + tq - 2) // tk + 1
    tri = tri_ref[...]

    outs = []
    for hh in range(2):
        q = (q_ref[0, :, hh * LANES:(hh + 1) * LANES] * scale).astype(BF16)

        def body(step, carry):
            acc, run = carry
            t = n_tiles - 1 - step
            k0 = pl.multiple_of(t * tk, tk)
            k = k_ref[0, pl.ds(k0, tk), :].astype(BF16)
            v = v_ref[0, pl.ds(k0, tk), :].astype(BF16)
            z = _dot_nt(q, k)
            valid = (k0 + lax.broadcasted_iota(jnp.int32, (1, tk), 1)) < qpos
            soft = jnp.log(1.0 + jnp.exp(-jnp.abs(z)))
            log_break = jnp.minimum(z, 0.0) - soft
            log_stay = jnp.where(valid, log_break - z, 0.0)
            s_hi, s_lo = _split(log_stay)
            after = _dot(s_hi, tri) + _dot(s_lo, tri)
            a = jnp.where(valid, jnp.exp(log_break + after + run), 0.0)
            acc = acc + _dot(a.astype(BF16), v)
            run = run + jnp.sum(log_stay, axis=1, keepdims=True)
            return acc, run

        acc, _ = lax.fori_loop(0, n_tiles, body,
                               (jnp.zeros((tq, LANES), F32), jnp.zeros((tq, 1), F32)))
        outs.append(acc)
    o_ref[0] = jnp.where(lane < HEAD_DIM, outs[0], outs[1])


def _sb_attn(q, kv, tri, tq, tk, pos0):
    b, t_q, _ = q.shape
    t_c = kv.shape[1]
    n_pairs = SB_HEADS // 2
    return pl.pallas_call(
        functools.partial(_sb_kernel, tq=tq, tk=tk, pos0=pos0),
        grid=(b, n_pairs, t_q // tq),
        in_specs=[pl.BlockSpec((1, tq, 2 * LANES), lambda bi, pr, i: (bi, i, pr)),
                  pl.BlockSpec((1, t_c, LANES), lambda bi, pr, i: (bi, 0, pr)),
                  pl.BlockSpec((1, t_c, LANES), lambda bi, pr, i: (bi, 0, n_pairs + pr)),
                  pl.BlockSpec(tri.shape, lambda bi, pr, i: (0, 0))],
        out_specs=pl.BlockSpec((1, tq, LANES), lambda bi, pr, i: (bi, i, pr)),
        out_shape=jax.ShapeDtypeStruct((b, t_q, SB_HEADS * HEAD_DIM), F32),
        compiler_params=_cparams(("parallel", "parallel", "arbitrary")),
        name="sb_attn",
    )(q, kv, kv, tri)


def _moba_kernel(q_ref, k_ref, v_ref, o_ref, km_scr, *, tq, pos0, n_blocks):
    i = pl.program_id(2)
    tk = MOBA_BLOCK
    q0 = pos0 + i * tq
    scale = HEAD_DIM ** -0.5
    qpos = q0 + lax.broadcasted_iota(jnp.int32, (tq, 1), 0)
    own = qpos // MOBA_BLOCK
    lane = lax.broadcasted_iota(jnp.int32, (1, LANES), 1)
    lane_f = lane.astype(F32)
    n_tiles = (q0 + tq - 1) // tk + 1

    @pl.when(i == 0)
    def _():
        km_scr[...] = jnp.zeros_like(km_scr)
        for n in range(n_blocks):
            km_scr[n:n + 1, :] = jnp.sum(k_ref[0, n * tk:(n + 1) * tk, :], axis=0,
                                         keepdims=True) * (1.0 / tk)

    km_hi, km_lo = _split(km_scr[...])
    outs = []
    for hh in range(2):
        q = q_ref[0, :, hh * LANES:(hh + 1) * LANES]
        q_hi, q_lo = _split(q)
        gate = _dot_nt(q_hi, km_hi) + _dot_nt(q_lo, km_hi) + _dot_nt(q_hi, km_lo)
        work = jnp.where(lane < own, gate, NEG)
        picked = jnp.zeros((tq, LANES), F32)
        for _ in range(MOBA_TOPK):
            top = jnp.max(work, axis=1, keepdims=True)
            first = jnp.min(jnp.where(work == top, lane_f, float(LANES)), axis=1,
                            keepdims=True)
            hit = (lane_f == first) & (top > 0.5 * NEG)
            picked = jnp.where(hit, 1.0, picked)
            work = jnp.where(hit, NEG, work)
        q_s = (q * scale).astype(BF16)

        def body(t, state):
            k0 = pl.multiple_of(t * tk, tk)
            k = k_ref[0, pl.ds(k0, tk), :].astype(BF16)
            v = v_ref[0, pl.ds(k0, tk), :].astype(BF16)
            kpos = k0 + lax.broadcasted_iota(jnp.int32, (1, tk), 1)
            chosen = jnp.sum(jnp.where(lane == t, picked, 0.0), axis=1, keepdims=True) > 0.5
            valid = (chosen | (own == t)) & (kpos <= qpos)
            return _flash_step(state, _dot_nt(q_s, k), valid, v)

        outs.append(_flash_out(lax.fori_loop(0, n_tiles, body, _flash_init(tq))))
    o_ref[0] = jnp.where(lane < HEAD_DIM, outs[0], outs[1])


def _moba_attn(q, kv, tq, pos0):
    b, t_q, _ = q.shape
    t_c = kv.shape[1]
    n_pairs = MOBA_HEADS // 2
    return pl.pallas_call(
        functools.partial(_moba_kernel, tq=tq, pos0=pos0, n_blocks=t_c // MOBA_BLOCK),
        grid=(b, n_pairs, t_q // tq),
        in_specs=[pl.BlockSpec((1, tq, 2 * LANES), lambda bi, pr, i: (bi, i, pr)),
                  pl.BlockSpec((1, t_c, LANES), lambda bi, pr, i: (bi, 0, pr)),
                  pl.BlockSpec((1, t_c, LANES), lambda bi, pr, i: (bi, 0, n_pairs + pr))],
        out_specs=pl.BlockSpec((1, tq, LANES), lambda bi, pr, i: (bi, i, pr)),
        out_shape=jax.ShapeDtypeStruct((b, t_q, MOBA_HEADS * HEAD_DIM), F32),
        scratch_shapes=[pltpu.VMEM((LANES, LANES), F32)],
        compiler_params=_cparams(("parallel", "parallel", "arbitrary")),
        name="moba_attn",
    )(q, kv, kv)


def _pad_heads(w, halves):
    d = w.shape[0]
    n_h = len(halves)
    onehot = jax.nn.one_hot(jnp.asarray(halves), 2, dtype=w.dtype)
    return jnp.einsum("dhe,hs->dhse", w.reshape(d, n_h, HEAD_DIM), onehot).reshape(d, n_h * LANES)


def _rope_tables(pos):
    half = HEAD_DIM // 2
    inv = 1.0 / (ROPE_THETA ** (jnp.arange(half, dtype=F32) / half))
    ang = pos.astype(F32)[:, None] * inv[None, :]
    cos = jnp.tile(jnp.cos(ang), (1, LANES // half))
    sin = jnp.tile(jnp.sin(ang), (1, LANES // half))
    upper = (jnp.arange(LANES) % HEAD_DIM) >= half
    return cos, jnp.where(upper, sin, 0.0), jnp.where(upper, 0.0, -sin)


def _overlap_matrix(n_ch):
    c = jnp.arange(n_ch)[:, None]
    n = jnp.arange(LANES)[None, :]
    return ((c >= 4 * n - 1) & (c <= 4 * n + 3)).astype(BF16)


def _expand_matrix(t_c, tk):
    blk = (jnp.arange(t_c) // NSA_SEL_BLOCK).reshape(t_c // tk, 1, tk)
    return (jnp.arange(LANES)[None, :, None] == blk).astype(BF16)


def _compress_weights(w1, w2):
    half = NSA_STRIDE * HEAD_DIM
    out = []
    for kv in range(2):
        wa = w1[kv, :half].reshape(NSA_STRIDE, HEAD_DIM, NSA_HID)
        wb = w1[kv, half:].reshape(NSA_STRIDE, HEAD_DIM, NSA_HID)
        ab = jnp.concatenate([wa, wb], axis=2)
        z = jnp.zeros_like(ab)
        g0 = jnp.concatenate([ab, z], axis=1)
        g1 = jnp.concatenate([z, ab], axis=1)
        out.append(jnp.concatenate([g0, g1], axis=2).reshape(NSA_STRIDE * LANES, 4 * NSA_HID))
    z2 = jnp.zeros_like(w2)
    w2p = jnp.concatenate([jnp.concatenate([w2, z2], axis=2),
                           jnp.concatenate([z2, w2], axis=2)], axis=1)
    return out[0].astype(BF16), out[1].astype(BF16), w2p.astype(BF16)


def _prep_weights(w_in_even, w_out_even, nsa_cmp_pe, nsa_cmp_w1, nsa_cmp_w2, w_ffn_gate,
                  w_ffn_up, w_ffn_down, w_in_odd, w_out_odd, w_router, w_exp_gate, w_exp_up,
                  w_exp_down, w_ple_proj, w_ple_gate):
    bf = lambda a: a.astype(BF16)
    we = w_in_even[0]
    qw = NSA_HEADS * HEAD_DIM
    kvw = 2 * LANES
    c0 = qw + 3 * kvw
    n_gate = 3 * NSA_HEADS
    s0 = c0 + n_gate
    sbw = SB_HEADS * HEAD_DIM
    even = [
        bf(_pad_heads(we[:, :qw], [h // NSA_GROUP for h in range(NSA_HEADS)])),
        bf(we[:, qw:qw + kvw]),
        bf(we[:, qw + kvw:qw + 2 * kvw]),
        bf(we[:, qw + 2 * kvw:c0]),
        bf(jnp.pad(we[:, c0:s0], ((0, 0), (0, LANES - n_gate)))),
        bf(_pad_heads(we[:, s0:s0 + sbw], [h % 2 for h in range(SB_HEADS)])),
        bf(we[:, s0 + sbw:]),
    ]
    wo = w_in_odd[0]
    mw = MOBA_HEADS * HEAD_DIM
    odd = [bf(_pad_heads(wo[:, :mw], [h % 2 for h in range(MOBA_HEADS)])), bf(wo[:, mw:])]
    w1k, w1v, w2p = _compress_weights(nsa_cmp_w1[0], nsa_cmp_w2[0])
    pe = bf(jnp.broadcast_to(nsa_cmp_pe[0].reshape(2, 1, -1), (2, 8, 2 * NSA_STRIDE * HEAD_DIM)))
    wr = jnp.pad(w_router[0], ((0, 0), (0, LANES - N_EXPERTS)))
    wr_hi, wr_lo = _split(wr)
    return dict(
        even=even, odd=odd, w1k=w1k, w1v=w1v, w2p=w2p, pe=pe, w1=bf(nsa_cmp_w1[0]),
        wo_a=bf(w_out_even[0][:qw]), wo_b=bf(w_out_even[0][qw:]),
        ffn=(bf(w_ffn_gate[0]), bf(w_ffn_up[0]), bf(w_ffn_down[0])),
        wo_c=bf(w_out_odd[0]), wr_hi=wr_hi, wr_lo=wr_lo,
        exp=(bf(w_exp_gate[0]), bf(w_exp_up[0]), bf(w_exp_down[0])),
        ple_proj=bf(w_ple_proj), ple_gate=bf(w_ple_gate))


EVEN_SEGS = ("rope_dual", "plain", ("rope", "none"), ("rope", "none"), ("sigmoid",), "plain",
             "plain")
ODD_SEGS = (("rope",) * 16, ("rope",) * 8 + ("none",) * 8)


def _trunk(x, p, pos0, past, W, norms, sizes):
    b, t, d = x.shape
    n = b * t
    tm, tq, tk_sel, tk_sb, tm_moe = sizes
    norm_mix, norm_ffn, norm_ple, norm_final = norms
    pos = pos0 + jnp.arange(t, dtype=jnp.int32)
    tabs = _rope_tables(pos)
    if t < tm:
        tabs = [jnp.tile(a, (tm // t, 1)) for a in tabs]
    h = x.reshape(n, d)
    row = lambda a: a.reshape(1, d)
    b3 = lambda a: a.reshape(b, t, a.shape[-1])

    qp, qr, cmp_r, sel_r, win_r, gates, sbq, sbkv = _proj(
        h, row(norm_mix[0]), tabs, W["even"], EVEN_SEGS, tm)
    if past is None:
        n_pages = t // PAGE
        ident = jnp.arange(b * n_pages, dtype=jnp.int32).reshape(b, n_pages)
        kc, vc = _compress(cmp_r.reshape(b * n_pages, PAGE, 2 * LANES), ident,
                           W["w1k"], W["w1v"], W["w2p"], W["pe"], W["w1"], 8)
        sel_ctx, win_ctx, sb_ctx = b3(sel_r), b3(win_r), b3(sbkv)
        wpos0 = 0
        win_state = win_ctx[:, -NSA_WINDOW:]
    else:
        pt = past["page_table"]
        tail = lambda a: jnp.pad(b3(a), ((0, 0), (0, MOBA_BLOCK - t), (0, 0)))
        pool = lambda c: c[0].reshape(c.shape[1], PAGE, -1)
        kc, vc = _compress(pool(past["cache_nsa_cmp"]), pt,
                           W["w1k"], W["w1v"], W["w2p"], W["pe"], W["w1"], 8)
        sel_ctx = _gather_ctx(pool(past["cache_nsa_sel"]), pt, tail(sel_r), 8)
        sb_ctx = _gather_ctx(pool(past["cache_sb"]), pt, tail(sbkv), 8)
        state = past["state_nsa_win"][0].reshape(b, NSA_WINDOW, 2 * LANES)
        win_all = jnp.concatenate([state, b3(win_r)], axis=1)
        win_ctx = jnp.pad(win_all, ((0, 0), (0, LANES - t), (0, 0)))
        wpos0 = pos0 - NSA_WINDOW
        win_state = win_all[:, -NSA_WINDOW:]
    t_c = sel_ctx.shape[1]
    o_a = _nsa_attn(b3(qp), b3(qr), b3(gates), kc, vc, sel_ctx, win_ctx,
                    _overlap_matrix(kc.shape[1]), _expand_matrix(t_c, tk_sel),
                    tq, tk_sel, pos0, wpos0)
    o_b = _sb_attn(b3(sbq), sb_ctx, jnp.tril(jnp.ones((tk_sb, tk_sb), BF16), -1),
                   tq, tk_sb, pos0)
    h = _outproj(h, o_a.reshape(n, -1), o_b.reshape(n, -1), W["wo_a"], W["wo_b"], tm)
    h = _ffn(h, row(norm_ffn[0]), *W["ffn"], tm, W["ffn"][0].shape[1] // 2)
    h = _ple(h, p[0].reshape(n, -1), row(norm_ple[0]), row(norm_final),
             W["ple_gate"][0], W["ple_proj"][0], tm, False)

    mq, mkv = _proj(h, row(norm_mix[1]), tabs, W["odd"], ODD_SEGS, tm)
    if past is None:
        moba_ctx = b3(mkv)
    else:
        moba_ctx = _gather_ctx(pool(past["cache_moba"]), pt, tail(mkv), 4)
    o_c = _moba_attn(b3(mq), moba_ctx, tq, pos0)
    half = o_c.shape[-1] // 2
    o_c = o_c.reshape(n, -1)
    h = _outproj(h, o_c[:, :half], o_c[:, half:], W["wo_c"][:half], W["wo_c"][half:], tm)
    h = _moe(h, row(norm_ffn[1]), W["wr_hi"], W["wr_lo"], *W["exp"], tm_moe, 512)
    y = _ple(h, p[1].reshape(n, -1), row(norm_ple[1]), row(norm_final),
             W["ple_gate"][1], W["ple_proj"][1], tm, True)

    kv5 = lambda a, heads: a.reshape(1, b, -1, 2, heads, HEAD_DIM)
    return (y.reshape(b, t, d), kv5(cmp_r, 2), kv5(sel_r, 2), kv5(win_state, 2),
            kv5(sbkv, SB_HEADS), kv5(mkv, MOBA_HEADS))


def kernel(x_prompt, x_sample, cache_nsa_cmp, cache_nsa_sel, state_nsa_win, cache_sb, cache_moba,
           page_table, p_prompt, p_sample, norm_mix, norm_ffn, norm_ple, norm_final, w_in_even,
           w_out_even, nsa_cmp_pe, nsa_cmp_w1, nsa_cmp_w2, w_ffn_gate, w_ffn_up, w_ffn_down,
           w_in_odd, w_out_odd, w_router, w_exp_gate, w_exp_up, w_exp_down, w_ple_proj,
           w_ple_gate):
    W = _prep_weights(w_in_even, w_out_even, nsa_cmp_pe, nsa_cmp_w1, nsa_cmp_w2, w_ffn_gate,
                      w_ffn_up, w_ffn_down, w_in_odd, w_out_odd, w_router, w_exp_gate, w_exp_up,
                      w_exp_down, w_ple_proj, w_ple_gate)
    norms = (norm_mix, norm_ffn, norm_ple, norm_final)
    past = dict(cache_nsa_cmp=cache_nsa_cmp, cache_nsa_sel=cache_nsa_sel,
                state_nsa_win=state_nsa_win, cache_sb=cache_sb, cache_moba=cache_moba,
                page_table=page_table)
    past_len = page_table.shape[1] * cache_sb.shape[2]
    t_dec = x_sample.shape[1]
    n_dec = x_sample.shape[0] * t_dec
    y_p, cmp_p, sel_p, win_p, sb_p, moba_p = _trunk(
        x_prompt, p_prompt, 0, None, W, norms, (256, 128, 256, 128, 1024))
    y_s, cmp_s, sel_s, win_s, sb_s, moba_s = _trunk(
        x_sample, p_sample, past_len, past, W, norms, (n_dec, t_dec, 256, 128, n_dec))
    return (y_p, y_s, cmp_p, cmp_s, sel_p, sel_s, win_p, win_s, sb_p, sb_s, moba_p, moba_s)
```

```python
import functools

import jax
import jax.numpy as jnp
from jax import lax
from jax.experimental import pallas as pl
from jax.experimental.pallas import tpu as pltpu

F32 = jnp.float32
BF16 = jnp.bfloat16

LANES = 128
HEAD_DIM = 64
PAGE = 128
RMS_EPS = 1e-6
ROPE_THETA = 10000.0
NEG = -1e30

NSA_HEADS = 8
NSA_GROUP = 4
NSA_STRIDE = 16
NSA_HID = 128
NSA_SEL_BLOCK = 64
NSA_PICKS = 13
NSA_WINDOW = 512
SB_HEADS = 8
MOBA_HEADS = 16
MOBA_BLOCK = 256
MOBA_TOPK = 3
N_EXPERTS = 8
SB_DEAD = -110.0

VMEM_LIMIT = 56 * 1024 * 1024


def _cparams(sem):
    return pltpu.CompilerParams(dimension_semantics=sem, vmem_limit_bytes=VMEM_LIMIT)


def _dot(a, b):
    return jnp.dot(a, b, preferred_element_type=F32)


def _dot_nt(a, b):
    return lax.dot_general(a, b, (((1,), (1,)), ((), ())), preferred_element_type=F32)


def _split(x):
    hi = x.astype(BF16)
    lo = (x - hi.astype(F32)).astype(BF16)
    return hi, lo


def _rmsnorm(x, g):
    return x * lax.rsqrt(jnp.mean(x * x, axis=-1, keepdims=True) + RMS_EPS) * g


def _sigmoid(x):
    return 1.0 / (1.0 + jnp.exp(-x))


def _rope_tile(y, cos, sa, sb):
    return y * cos + pltpu.roll(y, 32, 1) * sa + pltpu.roll(y, 96, 1) * sb


def _proj_kernel(x_ref, g_ref, cos_ref, sa_ref, sb_ref, *refs, segs):
    n_seg = len(segs)
    w_refs = refs[:n_seg]
    o_refs = list(refs[n_seg:])
    nb = _rmsnorm(x_ref[...], g_ref[...]).astype(BF16)
    cos, sa, sb = cos_ref[...], sa_ref[...], sb_ref[...]
    for w_ref, kinds in zip(w_refs, segs):
        y = _dot(nb, w_ref[...])
        if kinds == "plain":
            o_refs.pop(0)[...] = y
            continue
        if kinds == "rope_dual":
            o_refs.pop(0)[...] = y
            kinds = ("rope",) * (y.shape[1] // LANES)
        o_ref = o_refs.pop(0)
        for t, kind in enumerate(kinds):
            yt = y[:, t * LANES:(t + 1) * LANES]
            if kind == "rope":
                yt = _rope_tile(yt, cos, sa, sb)
            elif kind == "sigmoid":
                yt = _sigmoid(yt)
            o_ref[:, t * LANES:(t + 1) * LANES] = yt


def _proj(x, g, tabs, weights, segs, tm):
    n, d = x.shape
    nblk = tabs[0].shape[0] // tm
    out_shape, out_specs = [], []
    for w, kinds in zip(weights, segs):
        for _ in range(2 if kinds == "rope_dual" else 1):
            out_shape.append(jax.ShapeDtypeStruct((n, w.shape[1]), F32))
            out_specs.append(pl.BlockSpec((tm, w.shape[1]), lambda i: (i, 0)))
    tab_spec = pl.BlockSpec((tm, LANES), lambda i: (i % nblk, 0))
    return pl.pallas_call(
        functools.partial(_proj_kernel, segs=tuple(segs)),
        grid=(n // tm,),
        in_specs=[pl.BlockSpec((tm, d), lambda i: (i, 0)),
                  pl.BlockSpec((1, d), lambda i: (0, 0)),
                  tab_spec, tab_spec, tab_spec]
                 + [pl.BlockSpec(w.shape, lambda i: (0, 0)) for w in weights],
        out_specs=out_specs,
        out_shape=out_shape,
        compiler_params=_cparams(("parallel",)),
        name="norm_proj",
    )(x, g, *tabs, *weights)


def _outproj_kernel(h_ref, a_ref, b_ref, wa_ref, wb_ref, o_ref):
    o_ref[...] = (h_ref[...] + _dot(a_ref[...].astype(BF16), wa_ref[...])
                  + _dot(b_ref[...].astype(BF16), wb_ref[...]))


def _outproj(h, a, b, wa, wb, tm):
    n, d = h.shape
    row = lambda w: pl.BlockSpec((tm, w), lambda i: (i, 0))
    full = lambda w: pl.BlockSpec(w.shape, lambda i: (0, 0))
    return pl.pallas_call(
        _outproj_kernel,
        grid=(n // tm,),
        in_specs=[row(d), row(a.shape[1]), row(b.shape[1]), full(wa), full(wb)],
        out_specs=row(d),
        out_shape=jax.ShapeDtypeStruct((n, d), F32),
        compiler_params=_cparams(("parallel",)),
        name="out_proj",
    )(h, a, b, wa, wb)


def _ffn_kernel(h_ref, g_ref, wg_ref, wu_ref, wd_ref, o_ref, n_scr, acc_scr):
    f = pl.program_id(1)

    @pl.when(f == 0)
    def _():
        n_scr[...] = _rmsnorm(h_ref[...], g_ref[...]).astype(BF16)
        acc_scr[...] = jnp.zeros_like(acc_scr)

    nb = n_scr[...]
    gate = _dot(nb, wg_ref[...])
    up = _dot(nb, wu_ref[...])
    hid = (gate * _sigmoid(gate) * up).astype(BF16)
    acc_scr[...] += _dot(hid, wd_ref[...])

    @pl.when(f == pl.num_programs(1) - 1)
    def _():
        o_ref[...] = h_ref[...] + acc_scr[...]


def _ffn(h, g, wg, wu, wd, tm, tf):
    n, d = h.shape
    dff = wg.shape[1]
    return pl.pallas_call(
        _ffn_kernel,
        grid=(n // tm, dff // tf),
        in_specs=[pl.BlockSpec((tm, d), lambda i, f: (i, 0)),
                  pl.BlockSpec((1, d), lambda i, f: (0, 0)),
                  pl.BlockSpec((d, tf), lambda i, f: (0, f)),
                  pl.BlockSpec((d, tf), lambda i, f: (0, f)),
                  pl.BlockSpec((tf, d), lambda i, f: (f, 0))],
        out_specs=pl.BlockSpec((tm, d), lambda i, f: (i, 0)),
        out_shape=jax.ShapeDtypeStruct((n, d), F32),
        scratch_shapes=[pltpu.VMEM((tm, d), BF16), pltpu.VMEM((tm, d), F32)],
        compiler_params=_cparams(("parallel", "arbitrary")),
        name="swiglu_ffn",
    )(h, g, wg, wu, wd)


def _ple_kernel(h_ref, p_ref, g_ref, gf_ref, wg_ref, wp_ref, o_ref, *, final):
    h = h_ref[...]
    gate = _sigmoid(_dot(_rmsnorm(h, g_ref[...]).astype(BF16), wg_ref[...]))
    out = h + gate * _dot(p_ref[...].astype(BF16), wp_ref[...])
    if final:
        out = _rmsnorm(out, gf_ref[...])
    o_ref[...] = out


def _ple(h, p, g, gf, wg, wp, tm, final):
    n, d = h.shape
    row = lambda w: pl.BlockSpec((tm, w), lambda i: (i, 0))
    full = lambda a: pl.BlockSpec(a.shape, lambda i: (0, 0))
    return pl.pallas_call(
        functools.partial(_ple_kernel, final=final),
        grid=(n // tm,),
        in_specs=[row(d), row(p.shape[1]), full(g), full(gf), full(wg), full(wp)],
        out_specs=row(d),
        out_shape=jax.ShapeDtypeStruct((n, d), F32),
        compiler_params=_cparams(("parallel",)),
        name="ple",
    )(h, p, g, gf, wg, wp)


def _moe_kernel(h_ref, g_ref, wr_hi_ref, wr_lo_ref, wg_ref, wu_ref, wd_ref, o_ref,
                n_scr, gw_scr, acc_scr):
    e = pl.program_id(1)
    f = pl.program_id(2)
    lane = lax.broadcasted_iota(jnp.int32, (1, LANES), 1)

    @pl.when((e == 0) & (f == 0))
    def _():
        n = _rmsnorm(h_ref[...], g_ref[...])
        n_hi, n_lo = _split(n)
        n_scr[...] = n_hi
        logits = (_dot(n_hi, wr_hi_ref[...]) + _dot(n_lo, wr_hi_ref[...])
                  + _dot(n_hi, wr_lo_ref[...]))
        logits = jnp.where(lane < N_EXPERTS, logits, NEG)
        lane_f = lane.astype(F32)
        v1 = jnp.max(logits, axis=1, keepdims=True)
        i1 = jnp.min(jnp.where(logits == v1, lane_f, float(LANES)), axis=1, keepdims=True)
        rest = jnp.where(lane_f == i1, NEG, logits)
        v2 = jnp.max(rest, axis=1, keepdims=True)
        i2 = jnp.min(jnp.where(rest == v2, lane_f, float(LANES)), axis=1, keepdims=True)
        e2 = jnp.exp(v2 - v1)
        g1 = 1.0 / (1.0 + e2)
        gw_scr[...] = jnp.where(lane_f == i1, g1, jnp.where(lane_f == i2, e2 * g1, 0.0))
        acc_scr[...] = jnp.zeros_like(acc_scr)

    nb = n_scr[...]
    gate = _dot(nb, wg_ref[0])
    up = _dot(nb, wu_ref[0])
    hid = (gate * _sigmoid(gate) * up).astype(BF16)
    col = jnp.sum(jnp.where(lane == e, gw_scr[...], 0.0), axis=1, keepdims=True)
    acc_scr[...] += col * _dot(hid, wd_ref[0])

    @pl.when((e == pl.num_programs(1) - 1) & (f == pl.num_programs(2) - 1))
    def _():
        o_ref[...] = h_ref[...] + acc_scr[...]


def _moe(h, g, wr_hi, wr_lo, wg, wu, wd, tm, tf):
    n, d = h.shape
    n_e, _, dff = wg.shape
    return pl.pallas_call(
        _moe_kernel,
        grid=(n // tm, n_e, dff // tf),
        in_specs=[pl.BlockSpec((tm, d), lambda i, e, f: (i, 0)),
                  pl.BlockSpec((1, d), lambda i, e, f: (0, 0)),
                  pl.BlockSpec(wr_hi.shape, lambda i, e, f: (0, 0)),
                  pl.BlockSpec(wr_lo.shape, lambda i, e, f: (0, 0)),
                  pl.BlockSpec((1, d, tf), lambda i, e, f: (e, 0, f)),
                  pl.BlockSpec((1, d, tf), lambda i, e, f: (e, 0, f)),
                  pl.BlockSpec((1, tf, d), lambda i, e, f: (e, f, 0))],
        out_specs=pl.BlockSpec((tm, d), lambda i, e, f: (i, 0)),
        out_shape=jax.ShapeDtypeStruct((n, d), F32),
        scratch_shapes=[pltpu.VMEM((tm, d), BF16), pltpu.VMEM((tm, LANES), F32),
                        pltpu.VMEM((tm, d), F32)],
        compiler_params=_cparams(("parallel", "arbitrary", "arbitrary")),
        name="moe_ffn",
    )(h, g, wr_hi, wr_lo, wg, wu, wd)


def _gather_kernel(pt_ref, *refs, n_pg):
    del pt_ref
    page_refs, tail_ref, o_ref = refs[:n_pg], refs[n_pg], refs[n_pg + 1]
    s = pl.program_id(1)
    last = pl.num_programs(1) - 1

    @pl.when(s < last)
    def _():
        for i, p_ref in enumerate(page_refs):
            o_ref[0, i * PAGE:(i + 1) * PAGE, :] = p_ref[0]

    @pl.when(s == last)
    def _():
        o_ref[0, 0:tail_ref.shape[1], :] = tail_ref[0]


def _gather_ctx(pool, page_table, tail, n_pg):
    b, n_pages = page_table.shape
    w = pool.shape[2]
    n_tail = tail.shape[1]
    steps = n_pages // n_pg

    def page_spec(i):
        return pl.BlockSpec(
            (1, PAGE, w),
            lambda bi, s, pt: (pt[bi, jnp.minimum(s, steps - 1) * n_pg + i], 0, 0))

    return pl.pallas_call(
        functools.partial(_gather_kernel, n_pg=n_pg),
        grid_spec=pltpu.PrefetchScalarGridSpec(
            num_scalar_prefetch=1,
            grid=(b, steps + 1),
            in_specs=[page_spec(i) for i in range(n_pg)]
                     + [pl.BlockSpec((1, n_tail, w), lambda bi, s, pt: (bi, 0, 0))],
            out_specs=pl.BlockSpec((1, n_pg * PAGE, w), lambda bi, s, pt: (bi, s, 0)),
        ),
        out_shape=jax.ShapeDtypeStruct((b, n_pages * PAGE + n_tail, w), F32),
        compiler_params=_cparams(("parallel", "arbitrary")),
        name="paged_gather",
    )(page_table, *([pool] * n_pg), tail)


def _compress_kernel(pt_ref, *refs, n_pg):
    del pt_ref
    k_refs, v_refs = refs[:n_pg], refs[n_pg:2 * n_pg]
    (w1k_ref, w1v_ref, w2_ref, pe_ref, w1_ref, kc_ref, vc_ref,
     ak_scr, av_scr) = refs[2 * n_pg:]
    s = pl.program_id(1)
    n_ch = ak_scr.shape[0]
    per_page = PAGE // NSA_STRIDE

    for p_refs, a_scr in ((k_refs, ak_scr), (v_refs, av_scr)):
        for i, p_ref in enumerate(p_refs):
            base = pl.multiple_of((s * n_pg + i) * per_page, per_page)
            for j in range(NSA_STRIDE):
                a_scr[pl.ds(base, per_page), j * LANES:(j + 1) * LANES] = (
                    p_ref[0, pl.ds(j, per_page, stride=NSA_STRIDE), :])

    @pl.when(s == pl.num_programs(1) - 1)
    def _():
        for kv, (a_scr, w1p_ref, o_ref) in enumerate(((ak_scr, w1k_ref, kc_ref),
                                                      (av_scr, w1v_ref, vc_ref))):
            r = _dot(a_scr[...].astype(BF16), w1p_ref[...])
            bias = _dot(pe_ref[kv], w1_ref[kv])[0:1, :]
            hid = []
            for g in range(2):
                a = r[:, (2 * g) * NSA_HID:(2 * g + 1) * NSA_HID]
                b_next = pltpu.roll(r[:, (2 * g + 1) * NSA_HID:(2 * g + 2) * NSA_HID],
                                    n_ch - 1, 0)
                pre = a + b_next + bias
                hid.append(pre * _sigmoid(pre))
            hid = jnp.concatenate(hid, axis=1).astype(BF16)
            o_ref[0] = _dot(hid, w2_ref[kv])


def _compress(pool, page_table, w1k, w1v, w2, pe, w1, n_pg):
    b, n_pages = page_table.shape
    n_ch = n_pages * (PAGE // NSA_STRIDE)
    steps = n_pages // n_pg
    full = lambda a: pl.BlockSpec(a.shape, lambda bi, s, pt: (0,) * a.ndim)

    def page_spec(i, kv):
        return pl.BlockSpec((1, PAGE, LANES), lambda bi, s, pt: (pt[bi, s * n_pg + i], 0, kv))

    out_spec = pl.BlockSpec((1, n_ch, LANES), lambda bi, s, pt: (bi, 0, 0))
    return pl.pallas_call(
        functools.partial(_compress_kernel, n_pg=n_pg),
        grid_spec=pltpu.PrefetchScalarGridSpec(
            num_scalar_prefetch=1,
            grid=(b, steps),
            in_specs=[page_spec(i, kv) for kv in range(2) for i in range(n_pg)]
                     + [full(w1k), full(w1v), full(w2), full(pe), full(w1)],
            out_specs=[out_spec, out_spec],
            scratch_shapes=[pltpu.VMEM((n_ch, NSA_STRIDE * LANES), F32),
                            pltpu.VMEM((n_ch, NSA_STRIDE * LANES), F32)],
        ),
        out_shape=[jax.ShapeDtypeStruct((b, n_ch, LANES), F32)] * 2,
        compiler_params=_cparams(("parallel", "arbitrary")),
        name="nsa_compress",
    )(page_table, *([pool] * (2 * n_pg)), w1k, w1v, w2, pe, w1)


def _flash_step(state, s, valid, v):
    m, l, acc = state
    s = jnp.where(valid, s, NEG)
    m_new = jnp.maximum(m, jnp.max(s, axis=1, keepdims=True))
    alpha = jnp.exp(m - m_new)
    p = jnp.where(valid, jnp.exp(s - m_new), 0.0)
    l = alpha * l + jnp.sum(p, axis=1, keepdims=True)
    acc = alpha * acc + _dot(p.astype(BF16), v)
    return m_new, l, acc


def _flash_init(rows):
    return (jnp.full((rows, 1), NEG, F32), jnp.zeros((rows, 1), F32),
            jnp.zeros((rows, LANES), F32))


def _flash_out(state):
    _, l, acc = state
    return jnp.where(l > 0.0, acc / jnp.where(l > 0.0, l, 1.0), 0.0)


def _stack(x, times):
    return jnp.concatenate([x] * times, axis=0)


N_WIN_TILES = NSA_WINDOW // LANES + 1


def _nsa_kernel(qp_ref, qr_ref, gt_ref, kc_ref, vc_ref, sel_ref, ov_ref, ex_ref, *refs,
                tq, tk, pos0, wpos0):
    win_refs, o_ref = refs[:N_WIN_TILES], refs[N_WIN_TILES]
    i = pl.program_id(1)
    q0 = pos0 + i * tq
    scale = HEAD_DIM ** -0.5
    r4 = NSA_GROUP
    qpos = q0 + lax.broadcasted_iota(jnp.int32, (tq, 1), 0)
    qpos4 = _stack(qpos, r4)
    qblk = qpos // NSA_SEL_BLOCK
    lane = lax.broadcasted_iota(jnp.int32, (1, LANES), 1)
    lane_f = lane.astype(F32)
    n_ch = kc_ref.shape[1]
    c_end = lax.broadcasted_iota(jnp.int32, (1, n_ch), 1) * NSA_STRIDE + (2 * NSA_STRIDE - 1)
    kc = kc_ref[0].astype(BF16)
    vc = vc_ref[0].astype(BF16)
    gates = gt_ref[0]
    n_sel_tiles = (q0 + tq - 1) // tk + 1
    win_start = (q0 // LANES) * LANES - NSA_WINDOW

    outs = []
    for g in range(2):
        heads = range(g * r4, (g + 1) * r4)
        q_c = jnp.concatenate([qp_ref[0, :, h * LANES:(h + 1) * LANES] for h in heads], axis=0)
        q_r = jnp.concatenate([qr_ref[0, :, h * LANES:(h + 1) * LANES] for h in heads], axis=0)
        q_c = (q_c * scale).astype(BF16)
        q_r = (q_r * scale).astype(BF16)

        s = _dot_nt(q_c, kc)
        valid = c_end <= qpos4
        s = jnp.where(valid, s, NEG)
        e = jnp.where(valid, jnp.exp(s - jnp.max(s, axis=1, keepdims=True)), 0.0)
        l = jnp.sum(e, axis=1, keepdims=True)
        p = jnp.where(l > 0.0, e / jnp.where(l > 0.0, l, 1.0), 0.0)
        o_cmp = _dot(p.astype(BF16), vc)

        p_sum = p[0:tq]
        for r in range(1, r4):
            p_sum = p_sum + p[r * tq:(r + 1) * tq]
        p_hi, p_lo = _split(p_sum)
        imp = _dot(p_hi, ov_ref[...]) + _dot(p_lo, ov_ref[...])

        cand = (lane <= qblk) & (lane != 0) & (lane != qblk) & (lane != qblk - 1)
        work = jnp.where(cand, imp, -1.0)
        picked = jnp.zeros((tq, LANES), F32)
        for _ in range(NSA_PICKS):
            top = jnp.max(work, axis=1, keepdims=True)
            first = jnp.min(jnp.where(work == top, lane_f, float(LANES)), axis=1,
                            keepdims=True)
            hit = (lane_f == first) & (top >= 0.0)
            picked = jnp.where(hit, 1.0, picked)
            work = jnp.where(hit, -1.0, work)
        picked = picked.astype(BF16)

        def sel_body(t, state):
            k0 = pl.multiple_of(t * tk, tk)
            k = sel_ref[0, pl.ds(k0, tk), 0:LANES].astype(BF16)
            v = sel_ref[0, pl.ds(k0, tk), LANES:2 * LANES].astype(BF16)
            kpos = k0 + lax.broadcasted_iota(jnp.int32, (1, tk), 1)
            kblk = kpos // NSA_SEL_BLOCK
            chosen = _dot(picked, ex_ref[t]) > 0.5
            forced = (kblk == 0) | (kblk == qblk) | (kblk == qblk - 1)
            keep = jnp.where((chosen | forced) & (kpos <= qpos), 1.0, 0.0)
            return _flash_step(state, _dot_nt(q_r, k), _stack(keep, r4) > 0.5, v)

        o_sel = _flash_out(lax.fori_loop(0, n_sel_tiles, sel_body, _flash_init(r4 * tq)))

        state = _flash_init(r4 * tq)
        for jt, w_ref in enumerate(win_refs):
            k = w_ref[0, :, 0:LANES].astype(BF16)
            v = w_ref[0, :, LANES:2 * LANES].astype(BF16)
            wpos = win_start + jt * LANES + lane
            keep = jnp.where((wpos >= wpos0) & (wpos <= qpos) & (qpos - wpos < NSA_WINDOW),
                             1.0, 0.0)
            state = _flash_step(state, _dot_nt(q_r, k), _stack(keep, r4) > 0.5, v)
        o_win = _flash_out(state)

        for r, h in enumerate(heads):
            rows = slice(r * tq, (r + 1) * tq)
            o = (gates[:, 3 * h:3 * h + 1] * o_cmp[rows]
                 + gates[:, 3 * h + 1:3 * h + 2] * o_sel[rows]
                 + gates[:, 3 * h + 2:3 * h + 3] * o_win[rows])
            if h % 2 != g:
                o = pltpu.roll(o, HEAD_DIM, 1)
            outs.append(o)

    for pair in range(NSA_HEADS // 2):
        o_ref[0, :, pair * LANES:(pair + 1) * LANES] = jnp.where(
            lane < HEAD_DIM, outs[2 * pair], outs[2 * pair + 1])


def _nsa_attn(qp, qr, gates, kc, vc, sel_ctx, win_ctx, ov, ex, tq, tk, pos0, wpos0):
    b, t_q, _ = qp.shape
    t_c = sel_ctx.shape[1]
    n_win_tiles_total = win_ctx.shape[1] // LANES
    tile0 = pos0 // LANES - NSA_WINDOW // LANES - wpos0 // LANES

    def win_spec(jt):
        def idx(bi, i):
            t = tile0 + (i * tq) // LANES + jt
            return (bi, jnp.clip(t, 0, n_win_tiles_total - 1), 0)
        return pl.BlockSpec((1, LANES, 2 * LANES), idx)

    qspec = pl.BlockSpec((1, tq, NSA_HEADS * LANES), lambda bi, i: (bi, i, 0))
    return pl.pallas_call(
        functools.partial(_nsa_kernel, tq=tq, tk=tk, pos0=pos0, wpos0=wpos0),
        grid=(b, t_q // tq),
        in_specs=[qspec, qspec,
                  pl.BlockSpec((1, tq, LANES), lambda bi, i: (bi, i, 0)),
                  pl.BlockSpec((1,) + kc.shape[1:], lambda bi, i: (bi, 0, 0)),
                  pl.BlockSpec((1,) + vc.shape[1:], lambda bi, i: (bi, 0, 0)),
                  pl.BlockSpec((1, t_c, 2 * LANES), lambda bi, i: (bi, 0, 0)),
                  pl.BlockSpec(ov.shape, lambda bi, i: (0, 0)),
                  pl.BlockSpec(ex.shape, lambda bi, i: (0, 0, 0))]
                 + [win_spec(jt) for jt in range(N_WIN_TILES)],
        out_specs=pl.BlockSpec((1, tq, NSA_HEADS * HEAD_DIM), lambda bi, i: (bi, i, 0)),
        out_shape=jax.ShapeDtypeStruct((b, t_q, NSA_HEADS * HEAD_DIM), F32),
        compiler_params=_cparams(("parallel", "arbitrary")),
        name="nsa_attn",
    )(qp, qr, gates, kc, vc, sel_ctx, ov, ex, *([win_ctx] * N_WIN_TILES))


def _sb_kernel(q_ref, k_ref, v_ref, tri_ref, o_ref, *, tq, tk, pos0):
    i = pl.program_id(2)
    q0 = pos0 + i * tq
    scale = HEAD_DIM ** -0.5
    qpos = q0 + lax.broadcasted_iota(jnp.int32, (tq, 1), 0)
    lane = lax.broadcasted_iota(jnp.int32, (1, LANES), 1)
    n_tiles = (q0 + tq - 2) // tk + 1
    tri = tri_ref[...]
    qs = [(q_ref[0, :, hh * LANES:(hh + 1) * LANES] * scale).astype(BF16) for hh in range(2)]

    def tile(q, k, v, valid, acc, run):
        z = _dot_nt(q, k)
        soft = jnp.log(1.0 + jnp.exp(-jnp.abs(z)))
        log_break = jnp.minimum(z, 0.0) - soft
        log_stay = jnp.where(valid, log_break - z, 0.0)
        s_hi, s_lo = _split(log_stay)
        after = _dot(s_hi, tri) + _dot(s_lo, tri)
        a = jnp.where(valid, jnp.exp(log_break + after + run), 0.0)
        return acc + _dot(a.astype(BF16), v), run + jnp.sum(log_stay, axis=1, keepdims=True)

    def cond(carry):
        return (carry[0] < n_tiles) & (carry[1] > 0)

    def body(carry):
        step, _, acc0, run0, acc1, run1 = carry
        k0 = pl.multiple_of((n_tiles - 1 - step) * tk, tk)
        k = k_ref[0, pl.ds(k0, tk), :].astype(BF16)
        v = v_ref[0, pl.ds(k0, tk), :].astype(BF16)
        valid = (k0 + lax.broadcasted_iota(jnp.int32, (1, tk), 1)) < qpos
        acc0, run0 = tile(qs[0], k, v, valid, acc0, run0)
        acc1, run1 = tile(qs[1], k, v, valid, acc1, run1)
        live = jnp.maximum(jnp.max(run0), jnp.max(run1)) > SB_DEAD
        return step + 1, live.astype(jnp.int32), acc0, run0, acc1, run1

    zero_acc = jnp.zeros((tq, LANES), F32)
    zero_run = jnp.zeros((tq, 1), F32)
    _, _, acc0, _, acc1, _ = lax.while_loop(
        cond, body, (jnp.int32(0), jnp.int32(1), zero_acc, zero_run, zero_acc, zero_run))
    o_ref[0] = jnp.where(lane < HEAD_DIM, acc0, acc1)


def _sb_attn(q, kv, tri, tq, tk, pos0):
    b, t_q, _ = q.shape
    t_c = kv.shape[1]
    n_pairs = SB_HEADS // 2
    return pl.pallas_call(
        functools.partial(_sb_kernel, tq=tq, tk=tk, pos0=pos0),
        grid=(b, n_pairs, t_q // tq),
        in_specs=[pl.BlockSpec((1, tq, 2 * LANES), lambda bi, pr, i: (bi, i, pr)),
                  pl.BlockSpec((1, t_c, LANES), lambda bi, pr, i: (bi, 0, pr)),
                  pl.BlockSpec((1, t_c, LANES), lambda bi, pr, i: (bi, 0, n_pairs + pr)),
                  pl.BlockSpec(tri.shape, lambda bi, pr, i: (0, 0))],
        out_specs=pl.BlockSpec((1, tq, LANES), lambda bi, pr, i: (bi, i, pr)),
        out_shape=jax.ShapeDtypeStruct((b, t_q, SB_HEADS * HEAD_DIM), F32),
        compiler_params=_cparams(("parallel", "parallel", "arbitrary")),
        name="sb_attn",
    )(q, kv, kv, tri)


def _moba_kernel(q_ref, k_ref, v_ref, o_ref, km_scr, *, tq, pos0, n_blocks, bps):
    i = pl.program_id(2)
    blk = MOBA_BLOCK
    q0 = pos0 + i * tq
    own = q0 // blk
    scale = HEAD_DIM ** -0.5
    qpos = q0 + lax.broadcasted_iota(jnp.int32, (tq, 1), 0)
    lane = lax.broadcasted_iota(jnp.int32, (1, LANES), 1)
    lane_f = lane.astype(F32)

    @pl.when(i == 0)
    def _():
        km_scr[...] = jnp.zeros_like(km_scr)
        for n in range(n_blocks):
            km_scr[n:n + 1, :] = jnp.sum(k_ref[0, n * blk:(n + 1) * blk, :], axis=0,
                                         keepdims=True) * (1.0 / blk)

    km_hi, km_lo = _split(km_scr[...])
    q_s, picked, states = [], [], []
    own0 = pl.multiple_of(own * blk, blk)
    k_own = k_ref[0, pl.ds(own0, blk), :].astype(BF16)
    v_own = v_ref[0, pl.ds(own0, blk), :].astype(BF16)
    causal = (own0 + lax.broadcasted_iota(jnp.int32, (1, blk), 1)) <= qpos
    for hh in range(2):
        q = q_ref[0, :, hh * LANES:(hh + 1) * LANES]
        q_hi, q_lo = _split(q)
        gate = _dot_nt(q_hi, km_hi) + _dot_nt(q_lo, km_hi) + _dot_nt(q_hi, km_lo)
        work = jnp.where(lane < own, gate, NEG)
        pick = jnp.zeros((tq, LANES), F32)
        for _ in range(MOBA_TOPK):
            top = jnp.max(work, axis=1, keepdims=True)
            first = jnp.min(jnp.where(work == top, lane_f, float(LANES)), axis=1,
                            keepdims=True)
            hit = (lane_f == first) & (top > 0.5 * NEG)
            pick = jnp.where(hit, 1.0, pick)
            work = jnp.where(hit, NEG, work)
        picked.append(pick)
        q_s.append((q * scale).astype(BF16))
        states.append(_flash_step(_flash_init(tq), _dot_nt(q_s[hh], k_own), causal, v_own))

    def body(t, carry):
        k0 = pl.multiple_of(t * (bps * blk), bps * blk)
        k = k_ref[0, pl.ds(k0, bps * blk), :].astype(BF16)
        v = v_ref[0, pl.ds(k0, bps * blk), :].astype(BF16)
        out = []
        for hh in range(2):
            m, l, acc = carry[3 * hh:3 * hh + 3]
            s = _dot_nt(q_s[hh], k)
            rows = [jnp.sum(jnp.where(lane == t * bps + j, picked[hh], 0.0), axis=1,
                            keepdims=True) > 0.5 for j in range(bps)]
            parts = [s[:, j * blk:(j + 1) * blk] for j in range(bps)]
            m_new = m
            for j in range(bps):
                m_new = jnp.maximum(m_new, jnp.where(
                    rows[j], jnp.max(parts[j], axis=1, keepdims=True), NEG))
            alpha = jnp.exp(m - m_new)
            p = [jnp.exp(parts[j] - jnp.where(rows[j], m_new, -NEG)) for j in range(bps)]
            p = p[0] if bps == 1 else jnp.concatenate(p, axis=1)
            l = alpha * l + jnp.sum(p, axis=1, keepdims=True)
            acc = alpha * acc + _dot(p.astype(BF16), v)
            out += [m_new, l, acc]
        return tuple(out)

    n_steps = (own + bps - 1) // bps
    final = lax.fori_loop(0, n_steps, body, tuple(states[0]) + tuple(states[1]))
    o_ref[0] = jnp.where(lane < HEAD_DIM, _flash_out(final[0:3]), _flash_out(final[3:6]))


def _moba_attn(q, kv, tq, pos0, bps):
    b, t_q, _ = q.shape
    t_c = kv.shape[1]
    n_pairs = MOBA_HEADS // 2
    assert MOBA_BLOCK % tq == 0 and pos0 % tq == 0 and (pos0 // MOBA_BLOCK) % bps == 0
    return pl.pallas_call(
        functools.partial(_moba_kernel, tq=tq, pos0=pos0, n_blocks=t_c // MOBA_BLOCK, bps=bps),
        grid=(b, n_pairs, t_q // tq),
        in_specs=[pl.BlockSpec((1, tq, 2 * LANES), lambda bi, pr, i: (bi, i, pr)),
                  pl.BlockSpec((1, t_c, LANES), lambda bi, pr, i: (bi, 0, pr)),
                  pl.BlockSpec((1, t_c, LANES), lambda bi, pr, i: (bi, 0, n_pairs + pr))],
        out_specs=pl.BlockSpec((1, tq, LANES), lambda bi, pr, i: (bi, i, pr)),
        out_shape=jax.ShapeDtypeStruct((b, t_q, MOBA_HEADS * HEAD_DIM), F32),
        scratch_shapes=[pltpu.VMEM((LANES, LANES), F32)],
        compiler_params=_cparams(("parallel", "parallel", "arbitrary")),
        name="moba_attn",
    )(q, kv, kv)


def _pad_heads(w, halves):
    d = w.shape[0]
    n_h = len(halves)
    onehot = jax.nn.one_hot(jnp.asarray(halves), 2, dtype=w.dtype)
    return jnp.einsum("dhe,hs->dhse", w.reshape(d, n_h, HEAD_DIM), onehot).reshape(d, n_h * LANES)


def _rope_tables(pos):
    half = HEAD_DIM // 2
    inv = 1.0 / (ROPE_THETA ** (jnp.arange(half, dtype=F32) / half))
    ang = pos.astype(F32)[:, None] * inv[None, :]
    cos = jnp.tile(jnp.cos(ang), (1, LANES // half))
    sin = jnp.tile(jnp.sin(ang), (1, LANES // half))
    upper = (jnp.arange(LANES) % HEAD_DIM) >= half
    return cos, jnp.where(upper, sin, 0.0), jnp.where(upper, 0.0, -sin)


def _overlap_matrix(n_ch):
    c = jnp.arange(n_ch)[:, None]
    n = jnp.arange(LANES)[None, :]
    return ((c >= 4 * n - 1) & (c <= 4 * n + 3)).astype(BF16)


def _expand_matrix(t_c, tk):
    blk = (jnp.arange(t_c) // NSA_SEL_BLOCK).reshape(t_c // tk, 1, tk)
    return (jnp.arange(LANES)[None, :, None] == blk).astype(BF16)


def _compress_weights(w1, w2):
    half = NSA_STRIDE * HEAD_DIM
    out = []
    for kv in range(2):
        wa = w1[kv, :half].reshape(NSA_STRIDE, HEAD_DIM, NSA_HID)
        wb = w1[kv, half:].reshape(NSA_STRIDE, HEAD_DIM, NSA_HID)
        ab = jnp.concatenate([wa, wb], axis=2)
        z = jnp.zeros_like(ab)
        g0 = jnp.concatenate([ab, z], axis=1)
        g1 = jnp.concatenate([z, ab], axis=1)
        out.append(jnp.concatenate([g0, g1], axis=2).reshape(NSA_STRIDE * LANES, 4 * NSA_HID))
    z2 = jnp.zeros_like(w2)
    w2p = jnp.concatenate([jnp.concatenate([w2, z2], axis=2),
                           jnp.concatenate([z2, w2], axis=2)], axis=1)
    return out[0].astype(BF16), out[1].astype(BF16), w2p.astype(BF16)


def _prep_weights(w_in_even, w_out_even, nsa_cmp_pe, nsa_cmp_w1, nsa_cmp_w2, w_ffn_gate,
                  w_ffn_up, w_ffn_down, w_in_odd, w_out_odd, w_router, w_exp_gate, w_exp_up,
                  w_exp_down, w_ple_proj, w_ple_gate):
    bf = lambda a: a.astype(BF16)
    we = w_in_even[0]
    qw = NSA_HEADS * HEAD_DIM
    kvw = 2 * LANES
    c0 = qw + 3 * kvw
    n_gate = 3 * NSA_HEADS
    s0 = c0 + n_gate
    sbw = SB_HEADS * HEAD_DIM
    even = [
        bf(_pad_heads(we[:, :qw], [h // NSA_GROUP for h in range(NSA_HEADS)])),
        bf(we[:, qw:qw + kvw]),
        bf(we[:, qw + kvw:qw + 2 * kvw]),
        bf(we[:, qw + 2 * kvw:c0]),
        bf(jnp.pad(we[:, c0:s0], ((0, 0), (0, LANES - n_gate)))),
        bf(_pad_heads(we[:, s0:s0 + sbw], [h % 2 for h in range(SB_HEADS)])),
        bf(we[:, s0 + sbw:]),
    ]
    wo = w_in_odd[0]
    mw = MOBA_HEADS * HEAD_DIM
    odd = [bf(_pad_heads(wo[:, :mw], [h % 2 for h in range(MOBA_HEADS)])), bf(wo[:, mw:])]
    w1k, w1v, w2p = _compress_weights(nsa_cmp_w1[0], nsa_cmp_w2[0])
    pe = bf(jnp.broadcast_to(nsa_cmp_pe[0].reshape(2, 1, -1), (2, 8, 2 * NSA_STRIDE * HEAD_DIM)))
    wr = jnp.pad(w_router[0], ((0, 0), (0, LANES - N_EXPERTS)))
    wr_hi, wr_lo = _split(wr)
    return dict(
        even=even, odd=odd, w1k=w1k, w1v=w1v, w2p=w2p, pe=pe, w1=bf(nsa_cmp_w1[0]),
        wo_a=bf(w_out_even[0][:qw]), wo_b=bf(w_out_even[0][qw:]),
        ffn=(bf(w_ffn_gate[0]), bf(w_ffn_up[0]), bf(w_ffn_down[0])),
        wo_c=bf(w_out_odd[0]), wr_hi=wr_hi, wr_lo=wr_lo,
        exp=(bf(w_exp_gate[0]), bf(w_exp_up[0]), bf(w_exp_down[0])),
        ple_proj=bf(w_ple_proj), ple_gate=bf(w_ple_gate))


EVEN_SEGS = ("rope_dual", "plain", ("rope", "none"), ("rope", "none"), ("sigmoid",), "plain",
             "plain")
ODD_SEGS = (("rope",) * 16, ("rope",) * 8 + ("none",) * 8)


def _trunk(x, p, pos0, past, W, norms, sizes):
    b, t, d = x.shape
    n = b * t
    tm, tq, tk_sel, tk_sb, tm_moe, tq_moba, moba_bps = sizes
    norm_mix, norm_ffn, norm_ple, norm_final = norms
    pos = pos0 + jnp.arange(t, dtype=jnp.int32)
    tabs = _rope_tables(pos)
    if t < tm:
        tabs = [jnp.tile(a, (tm // t, 1)) for a in tabs]
    h = x.reshape(n, d)
    row = lambda a: a.reshape(1, d)
    b3 = lambda a: a.reshape(b, t, a.shape[-1])

    qp, qr, cmp_r, sel_r, win_r, gates, sbq, sbkv = _proj(
        h, row(norm_mix[0]), tabs, W["even"], EVEN_SEGS, tm)
    if past is None:
        n_pages = t // PAGE
        ident = jnp.arange(b * n_pages, dtype=jnp.int32).reshape(b, n_pages)
        kc, vc = _compress(cmp_r.reshape(b * n_pages, PAGE, 2 * LANES), ident,
                           W["w1k"], W["w1v"], W["w2p"], W["pe"], W["w1"], 8)
        sel_ctx, win_ctx, sb_ctx = b3(sel_r), b3(win_r), b3(sbkv)
        wpos0 = 0
        win_state = win_ctx[:, -NSA_WINDOW:]
    else:
        pt = past["page_table"]
        tail = lambda a: jnp.pad(b3(a), ((0, 0), (0, MOBA_BLOCK - t), (0, 0)))
        pool = lambda c: c[0].reshape(c.shape[1], PAGE, -1)
        kc, vc = _compress(pool(past["cache_nsa_cmp"]), pt,
                           W["w1k"], W["w1v"], W["w2p"], W["pe"], W["w1"], 8)
        sel_ctx = _gather_ctx(pool(past["cache_nsa_sel"]), pt, tail(sel_r), 8)
        sb_ctx = _gather_ctx(pool(past["cache_sb"]), pt, tail(sbkv), 8)
        state = past["state_nsa_win"][0].reshape(b, NSA_WINDOW, 2 * LANES)
        win_all = jnp.concatenate([state, b3(win_r)], axis=1)
        win_ctx = jnp.pad(win_all, ((0, 0), (0, LANES - t), (0, 0)))
        wpos0 = pos0 - NSA_WINDOW
        win_state = win_all[:, -NSA_WINDOW:]
    t_c = sel_ctx.shape[1]
    o_a = _nsa_attn(b3(qp), b3(qr), b3(gates), kc, vc, sel_ctx, win_ctx,
                    _overlap_matrix(kc.shape[1]), _expand_matrix(t_c, tk_sel),
                    tq, tk_sel, pos0, wpos0)
    o_b = _sb_attn(b3(sbq), sb_ctx, jnp.tril(jnp.ones((tk_sb, tk_sb), BF16), -1),
                   tq, tk_sb, pos0)
    h = _outproj(h, o_a.reshape(n, -1), o_b.reshape(n, -1), W["wo_a"], W["wo_b"], tm)
    h = _ffn(h, row(norm_ffn[0]), *W["ffn"], tm, W["ffn"][0].shape[1] // 2)
    h = _ple(h, p[0].reshape(n, -1), row(norm_ple[0]), row(norm_final),
             W["ple_gate"][0], W["ple_proj"][0], tm, False)

    mq, mkv = _proj(h, row(norm_mix[1]), tabs, W["odd"], ODD_SEGS, tm)
    if past is None:
        moba_ctx = b3(mkv)
    else:
        moba_ctx = _gather_ctx(pool(past["cache_moba"]), pt, tail(mkv), 4)
    o_c = _moba_attn(b3(mq), moba_ctx, tq_moba, pos0, moba_bps)
    half = o_c.shape[-1] // 2
    o_c = o_c.reshape(n, -1)
    h = _outproj(h, o_c[:, :half], o_c[:, half:], W["wo_c"][:half], W["wo_c"][half:], tm)
    h = _moe(h, row(norm_ffn[1]), W["wr_hi"], W["wr_lo"], *W["exp"], tm_moe, 512)
    y = _ple(h, p[1].reshape(n, -1), row(norm_ple[1]), row(norm_final),
             W["ple_gate"][1], W["ple_proj"][1], tm, True)

    kv5 = lambda a, heads: a.reshape(1, b, -1, 2, heads, HEAD_DIM)
    return (y.reshape(b, t, d), kv5(cmp_r, 2), kv5(sel_r, 2), kv5(win_state, 2),
            kv5(sbkv, SB_HEADS), kv5(mkv, MOBA_HEADS))


def kernel(x_prompt, x_sample, cache_nsa_cmp, cache_nsa_sel, state_nsa_win, cache_sb, cache_moba,
           page_table, p_prompt, p_sample, norm_mix, norm_ffn, norm_ple, norm_final, w_in_even,
           w_out_even, nsa_cmp_pe, nsa_cmp_w1, nsa_cmp_w2, w_ffn_gate, w_ffn_up, w_ffn_down,
           w_in_odd, w_out_odd, w_router, w_exp_gate, w_exp_up, w_exp_down, w_ple_proj,
           w_ple_gate):
    W = _prep_weights(w_in_even, w_out_even, nsa_cmp_pe, nsa_cmp_w1, nsa_cmp_w2, w_ffn_gate,
                      w_ffn_up, w_ffn_down, w_in_odd, w_out_odd, w_router, w_exp_gate, w_exp_up,
                      w_exp_down, w_ple_proj, w_ple_gate)
    norms = (norm_mix, norm_ffn, norm_ple, norm_final)
    past = dict(cache_nsa_cmp=cache_nsa_cmp, cache_nsa_sel=cache_nsa_sel,
                state_nsa_win=state_nsa_win, cache_sb=cache_sb, cache_moba=cache_moba,
                page_table=page_table)
    past_len = page_table.shape[1] * cache_sb.shape[2]
    t_dec = x_sample.shape[1]
    n_dec = x_sample.shape[0] * t_dec
    y_p, cmp_p, sel_p, win_p, sb_p, moba_p = _trunk(
        x_prompt, p_prompt, 0, None, W, norms, (256, 128, 256, 128, 1024, MOBA_BLOCK, 1))
    y_s, cmp_s, sel_s, win_s, sb_s, moba_s = _trunk(
        x_sample, p_sample, past_len, past, W, norms, (n_dec, t_dec, 256, 128, n_dec, t_dec, 8))
    return (y_p, y_s, cmp_p, cmp_s, sel_p, sel_s, win_p, win_s, sb_p, sb_s, moba_p, moba_s)
```

```python
import functools

import jax
import jax.numpy as jnp
from jax import lax
from jax.experimental import pallas as pl
from jax.experimental.pallas import tpu as pltpu

F32 = jnp.float32
BF16 = jnp.bfloat16

LANES = 128
HEAD_DIM = 64
PAGE = 128
RMS_EPS = 1e-6
ROPE_THETA = 10000.0
NEG = -1e30

NSA_HEADS = 8
NSA_GROUP = 4
NSA_STRIDE = 16
NSA_HID = 128
NSA_SEL_BLOCK = 64
NSA_PICKS = 13
NSA_WINDOW = 512
SB_HEADS = 8
MOBA_HEADS = 16
MOBA_BLOCK = 256
MOBA_TOPK = 3
N_EXPERTS = 8
SB_DEAD = -110.0

VMEM_LIMIT = 56 * 1024 * 1024


def _cparams(sem):
    return pltpu.CompilerParams(dimension_semantics=sem, vmem_limit_bytes=VMEM_LIMIT)


def _dot(a, b):
    return jnp.dot(a, b, preferred_element_type=F32)


def _dot_nt(a, b):
    return lax.dot_general(a, b, (((1,), (1,)), ((), ())), preferred_element_type=F32)


def _split(x):
    hi = x.astype(BF16)
    lo = (x - hi.astype(F32)).astype(BF16)
    return hi, lo


def _rmsnorm(x, g):
    return x * lax.rsqrt(jnp.mean(x * x, axis=-1, keepdims=True) + RMS_EPS) * g


def _sigmoid(x):
    return 1.0 / (1.0 + jnp.exp(-x))


def _rope_tile(y, cos, sa, sb):
    return y * cos + pltpu.roll(y, 32, 1) * sa + pltpu.roll(y, 96, 1) * sb


def _proj_kernel(x_ref, g_ref, cos_ref, sa_ref, sb_ref, *refs, segs):
    n_seg = len(segs)
    w_refs = refs[:n_seg]
    o_refs = list(refs[n_seg:])
    nb = _rmsnorm(x_ref[...], g_ref[...]).astype(BF16)
    cos, sa, sb = cos_ref[...], sa_ref[...], sb_ref[...]
    for w_ref, kinds in zip(w_refs, segs):
        y = _dot(nb, w_ref[...])
        if kinds == "plain":
            o_refs.pop(0)[...] = y
            continue
        if kinds == "rope_dual":
            o_refs.pop(0)[...] = y
            kinds = ("rope",) * (y.shape[1] // LANES)
        o_ref = o_refs.pop(0)
        for t, kind in enumerate(kinds):
            yt = y[:, t * LANES:(t + 1) * LANES]
            if kind == "rope":
                yt = _rope_tile(yt, cos, sa, sb)
            elif kind == "sigmoid":
                yt = _sigmoid(yt)
            o_ref[:, t * LANES:(t + 1) * LANES] = yt


def _proj(x, g, tabs, weights, segs, tm):
    n, d = x.shape
    nblk = tabs[0].shape[0] // tm
    out_shape, out_specs = [], []
    for w, kinds in zip(weights, segs):
        for _ in range(2 if kinds == "rope_dual" else 1):
            out_shape.append(jax.ShapeDtypeStruct((n, w.shape[1]), F32))
            out_specs.append(pl.BlockSpec((tm, w.shape[1]), lambda i: (i, 0)))
    tab_spec = pl.BlockSpec((tm, LANES), lambda i: (i % nblk, 0))
    return pl.pallas_call(
        functools.partial(_proj_kernel, segs=tuple(segs)),
        grid=(n // tm,),
        in_specs=[pl.BlockSpec((tm, d), lambda i: (i, 0)),
                  pl.BlockSpec((1, d), lambda i: (0, 0)),
                  tab_spec, tab_spec, tab_spec]
                 + [pl.BlockSpec(w.shape, lambda i: (0, 0)) for w in weights],
        out_specs=out_specs,
        out_shape=out_shape,
        compiler_params=_cparams(("parallel",)),
        name="norm_proj",
    )(x, g, *tabs, *weights)


def _outproj_kernel(h_ref, a_ref, b_ref, wa_ref, wb_ref, o_ref):
    o_ref[...] = (h_ref[...] + _dot(a_ref[...].astype(BF16), wa_ref[...])
                  + _dot(b_ref[...].astype(BF16), wb_ref[...]))


def _outproj(h, a, b, wa, wb, tm):
    n, d = h.shape
    row = lambda w: pl.BlockSpec((tm, w), lambda i: (i, 0))
    full = lambda w: pl.BlockSpec(w.shape, lambda i: (0, 0))
    return pl.pallas_call(
        _outproj_kernel,
        grid=(n // tm,),
        in_specs=[row(d), row(a.shape[1]), row(b.shape[1]), full(wa), full(wb)],
        out_specs=row(d),
        out_shape=jax.ShapeDtypeStruct((n, d), F32),
        compiler_params=_cparams(("parallel",)),
        name="out_proj",
    )(h, a, b, wa, wb)


def _ffn_kernel(h_ref, g_ref, wg_ref, wu_ref, wd_ref, o_ref, n_scr, acc_scr):
    f = pl.program_id(1)

    @pl.when(f == 0)
    def _():
        n_scr[...] = _rmsnorm(h_ref[...], g_ref[...]).astype(BF16)
        acc_scr[...] = jnp.zeros_like(acc_scr)

    nb = n_scr[...]
    gate = _dot(nb, wg_ref[...])
    up = _dot(nb, wu_ref[...])
    hid = (gate * _sigmoid(gate) * up).astype(BF16)
    acc_scr[...] += _dot(hid, wd_ref[...])

    @pl.when(f == pl.num_programs(1) - 1)
    def _():
        o_ref[...] = h_ref[...] + acc_scr[...]


def _ffn(h, g, wg, wu, wd, tm, tf):
    n, d = h.shape
    dff = wg.shape[1]
    return pl.pallas_call(
        _ffn_kernel,
        grid=(n // tm, dff // tf),
        in_specs=[pl.BlockSpec((tm, d), lambda i, f: (i, 0)),
                  pl.BlockSpec((1, d), lambda i, f: (0, 0)),
                  pl.BlockSpec((d, tf), lambda i, f: (0, f)),
                  pl.BlockSpec((d, tf), lambda i, f: (0, f)),
                  pl.BlockSpec((tf, d), lambda i, f: (f, 0))],
        out_specs=pl.BlockSpec((tm, d), lambda i, f: (i, 0)),
        out_shape=jax.ShapeDtypeStruct((n, d), F32),
        scratch_shapes=[pltpu.VMEM((tm, d), BF16), pltpu.VMEM((tm, d), F32)],
        compiler_params=_cparams(("parallel", "arbitrary")),
        name="swiglu_ffn",
    )(h, g, wg, wu, wd)


def _ple_kernel(h_ref, p_ref, g_ref, gf_ref, wg_ref, wp_ref, o_ref, *, final):
    h = h_ref[...]
    gate = _sigmoid(_dot(_rmsnorm(h, g_ref[...]).astype(BF16), wg_ref[...]))
    out = h + gate * _dot(p_ref[...].astype(BF16), wp_ref[...])
    if final:
        out = _rmsnorm(out, gf_ref[...])
    o_ref[...] = out


def _ple(h, p, g, gf, wg, wp, tm, final):
    n, d = h.shape
    row = lambda w: pl.BlockSpec((tm, w), lambda i: (i, 0))
    full = lambda a: pl.BlockSpec(a.shape, lambda i: (0, 0))
    return pl.pallas_call(
        functools.partial(_ple_kernel, final=final),
        grid=(n // tm,),
        in_specs=[row(d), row(p.shape[1]), full(g), full(gf), full(wg), full(wp)],
        out_specs=row(d),
        out_shape=jax.ShapeDtypeStruct((n, d), F32),
        compiler_params=_cparams(("parallel",)),
        name="ple",
    )(h, p, g, gf, wg, wp)


def _moe_kernel(h_ref, g_ref, wr_hi_ref, wr_lo_ref, wg_ref, wu_ref, wd_ref, o_ref,
                n_scr, gw_scr, acc_scr):
    e = pl.program_id(1)
    f = pl.program_id(2)
    lane = lax.broadcasted_iota(jnp.int32, (1, LANES), 1)

    @pl.when((e == 0) & (f == 0))
    def _():
        n = _rmsnorm(h_ref[...], g_ref[...])
        n_hi, n_lo = _split(n)
        n_scr[...] = n_hi
        logits = (_dot(n_hi, wr_hi_ref[...]) + _dot(n_lo, wr_hi_ref[...])
                  + _dot(n_hi, wr_lo_ref[...]))
        logits = jnp.where(lane < N_EXPERTS, logits, NEG)
        lane_f = lane.astype(F32)
        v1 = jnp.max(logits, axis=1, keepdims=True)
        i1 = jnp.min(jnp.where(logits == v1, lane_f, float(LANES)), axis=1, keepdims=True)
        rest = jnp.where(lane_f == i1, NEG, logits)
        v2 = jnp.max(rest, axis=1, keepdims=True)
        i2 = jnp.min(jnp.where(rest == v2, lane_f, float(LANES)), axis=1, keepdims=True)
        e2 = jnp.exp(v2 - v1)
        g1 = 1.0 / (1.0 + e2)
        gw_scr[...] = jnp.where(lane_f == i1, g1, jnp.where(lane_f == i2, e2 * g1, 0.0))
        acc_scr[...] = jnp.zeros_like(acc_scr)

    nb = n_scr[...]
    gate = _dot(nb, wg_ref[0])
    up = _dot(nb, wu_ref[0])
    hid = (gate * _sigmoid(gate) * up).astype(BF16)
    col = jnp.sum(jnp.where(lane == e, gw_scr[...], 0.0), axis=1, keepdims=True)
    acc_scr[...] += col * _dot(hid, wd_ref[0])

    @pl.when((e == pl.num_programs(1) - 1) & (f == pl.num_programs(2) - 1))
    def _():
        o_ref[...] = h_ref[...] + acc_scr[...]


def _moe(h, g, wr_hi, wr_lo, wg, wu, wd, tm, tf):
    n, d = h.shape
    n_e, _, dff = wg.shape
    return pl.pallas_call(
        _moe_kernel,
        grid=(n // tm, n_e, dff // tf),
        in_specs=[pl.BlockSpec((tm, d), lambda i, e, f: (i, 0)),
                  pl.BlockSpec((1, d), lambda i, e, f: (0, 0)),
                  pl.BlockSpec(wr_hi.shape, lambda i, e, f: (0, 0)),
                  pl.BlockSpec(wr_lo.shape, lambda i, e, f: (0, 0)),
                  pl.BlockSpec((1, d, tf), lambda i, e, f: (e, 0, f)),
                  pl.BlockSpec((1, d, tf), lambda i, e, f: (e, 0, f)),
                  pl.BlockSpec((1, tf, d), lambda i, e, f: (e, f, 0))],
        out_specs=pl.BlockSpec((tm, d), lambda i, e, f: (i, 0)),
        out_shape=jax.ShapeDtypeStruct((n, d), F32),
        scratch_shapes=[pltpu.VMEM((tm, d), BF16), pltpu.VMEM((tm, LANES), F32),
                        pltpu.VMEM((tm, d), F32)],
        compiler_params=_cparams(("parallel", "arbitrary", "arbitrary")),
        name="moe_ffn",
    )(h, g, wr_hi, wr_lo, wg, wu, wd)


def _gather_kernel(pt_ref, *refs, n_pg):
    del pt_ref
    page_refs, tail_ref, o_ref = refs[:n_pg], refs[n_pg], refs[n_pg + 1]
    s = pl.program_id(1)
    last = pl.num_programs(1) - 1

    @pl.when(s < last)
    def _():
        for i, p_ref in enumerate(page_refs):
            o_ref[0, i * PAGE:(i + 1) * PAGE, :] = p_ref[0]

    @pl.when(s == last)
    def _():
        o_ref[0, 0:tail_ref.shape[1], :] = tail_ref[0]


def _gather_ctx(pool, page_table, tail, n_pg):
    b, n_pages = page_table.shape
    w = pool.shape[2]
    n_tail = tail.shape[1]
    steps = n_pages // n_pg

    def page_spec(i):
        return pl.BlockSpec(
            (1, PAGE, w),
            lambda bi, s, pt: (pt[bi, jnp.minimum(s, steps - 1) * n_pg + i], 0, 0))

    return pl.pallas_call(
        functools.partial(_gather_kernel, n_pg=n_pg),
        grid_spec=pltpu.PrefetchScalarGridSpec(
            num_scalar_prefetch=1,
            grid=(b, steps + 1),
            in_specs=[page_spec(i) for i in range(n_pg)]
                     + [pl.BlockSpec((1, n_tail, w), lambda bi, s, pt: (bi, 0, 0))],
            out_specs=pl.BlockSpec((1, n_pg * PAGE, w), lambda bi, s, pt: (bi, s, 0)),
        ),
        out_shape=jax.ShapeDtypeStruct((b, n_pages * PAGE + n_tail, w), F32),
        compiler_params=_cparams(("parallel", "arbitrary")),
        name="paged_gather",
    )(page_table, *([pool] * n_pg), tail)


def _compress_kernel(pt_ref, *refs, n_pg):
    del pt_ref
    k_refs, v_refs = refs[:n_pg], refs[n_pg:2 * n_pg]
    (w1k_ref, w1v_ref, w2_ref, pe_ref, w1_ref, kc_ref, vc_ref,
     ak_scr, av_scr) = refs[2 * n_pg:]
    s = pl.program_id(1)
    n_ch = ak_scr.shape[0]
    per_page = PAGE // NSA_STRIDE

    for p_refs, a_scr in ((k_refs, ak_scr), (v_refs, av_scr)):
        for i, p_ref in enumerate(p_refs):
            base = pl.multiple_of((s * n_pg + i) * per_page, per_page)
            for j in range(NSA_STRIDE):
                a_scr[pl.ds(base, per_page), j * LANES:(j + 1) * LANES] = (
                    p_ref[0, pl.ds(j, per_page, stride=NSA_STRIDE), :])

    @pl.when(s == pl.num_programs(1) - 1)
    def _():
        for kv, (a_scr, w1p_ref, o_ref) in enumerate(((ak_scr, w1k_ref, kc_ref),
                                                      (av_scr, w1v_ref, vc_ref))):
            r = _dot(a_scr[...].astype(BF16), w1p_ref[...])
            bias = _dot(pe_ref[kv], w1_ref[kv])[0:1, :]
            hid = []
            for g in range(2):
                a = r[:, (2 * g) * NSA_HID:(2 * g + 1) * NSA_HID]
                b_next = pltpu.roll(r[:, (2 * g + 1) * NSA_HID:(2 * g + 2) * NSA_HID],
                                    n_ch - 1, 0)
                pre = a + b_next + bias
                hid.append(pre * _sigmoid(pre))
            hid = jnp.concatenate(hid, axis=1).astype(BF16)
            o_ref[0] = _dot(hid, w2_ref[kv])


def _compress(pool, page_table, w1k, w1v, w2, pe, w1, n_pg):
    b, n_pages = page_table.shape
    n_ch = n_pages * (PAGE // NSA_STRIDE)
    steps = n_pages // n_pg
    full = lambda a: pl.BlockSpec(a.shape, lambda bi, s, pt: (0,) * a.ndim)

    def page_spec(i, kv):
        return pl.BlockSpec((1, PAGE, LANES), lambda bi, s, pt: (pt[bi, s * n_pg + i], 0, kv))

    out_spec = pl.BlockSpec((1, n_ch, LANES), lambda bi, s, pt: (bi, 0, 0))
    return pl.pallas_call(
        functools.partial(_compress_kernel, n_pg=n_pg),
        grid_spec=pltpu.PrefetchScalarGridSpec(
            num_scalar_prefetch=1,
            grid=(b, steps),
            in_specs=[page_spec(i, kv) for kv in range(2) for i in range(n_pg)]
                     + [full(w1k), full(w1v), full(w2), full(pe), full(w1)],
            out_specs=[out_spec, out_spec],
            scratch_shapes=[pltpu.VMEM((n_ch, NSA_STRIDE * LANES), F32),
                            pltpu.VMEM((n_ch, NSA_STRIDE * LANES), F32)],
        ),
        out_shape=[jax.ShapeDtypeStruct((b, n_ch, LANES), F32)] * 2,
        compiler_params=_cparams(("parallel", "arbitrary")),
        name="nsa_compress",
    )(page_table, *([pool] * (2 * n_pg)), w1k, w1v, w2, pe, w1)


def _flash_step(state, s, valid, v):
    m, l, acc = state
    s = jnp.where(valid, s, NEG)
    m_new = jnp.maximum(m, jnp.max(s, axis=1, keepdims=True))
    alpha = jnp.exp(m - m_new)
    p = jnp.where(valid, jnp.exp(s - m_new), 0.0)
    l = alpha * l + jnp.sum(p, axis=1, keepdims=True)
    acc = alpha * acc + _dot(p.astype(BF16), v)
    return m_new, l, acc


def _flash_init(rows):
    return (jnp.full((rows, 1), NEG, F32), jnp.zeros((rows, 1), F32),
            jnp.zeros((rows, LANES), F32))


def _flash_out(state):
    _, l, acc = state
    return jnp.where(l > 0.0, acc / jnp.where(l > 0.0, l, 1.0), 0.0)


def _stack(x, times):
    return jnp.concatenate([x] * times, axis=0)


N_WIN_TILES = NSA_WINDOW // LANES + 1


def _nsa_kernel(qp_ref, qr_ref, gt_ref, kc_ref, vc_ref, sel_ref, ov_ref, ex_ref, *refs,
                tq, tk, pos0, wpos0):
    win_refs, o_ref = refs[:N_WIN_TILES], refs[N_WIN_TILES]
    i = pl.program_id(1)
    q0 = pos0 + i * tq
    scale = HEAD_DIM ** -0.5
    r4 = NSA_GROUP
    qpos = q0 + lax.broadcasted_iota(jnp.int32, (tq, 1), 0)
    qpos4 = _stack(qpos, r4)
    qblk = qpos // NSA_SEL_BLOCK
    lane = lax.broadcasted_iota(jnp.int32, (1, LANES), 1)
    lane_f = lane.astype(F32)
    n_ch = kc_ref.shape[1]
    c_end = lax.broadcasted_iota(jnp.int32, (1, n_ch), 1) * NSA_STRIDE + (2 * NSA_STRIDE - 1)
    kc = kc_ref[0].astype(BF16)
    vc = vc_ref[0].astype(BF16)
    gates = gt_ref[0]
    n_sel_tiles = (q0 + tq - 1) // tk + 1
    win_start = (q0 // LANES) * LANES - NSA_WINDOW

    outs = []
    for g in range(2):
        heads = range(g * r4, (g + 1) * r4)
        q_c = jnp.concatenate([qp_ref[0, :, h * LANES:(h + 1) * LANES] for h in heads], axis=0)
        q_r = jnp.concatenate([qr_ref[0, :, h * LANES:(h + 1) * LANES] for h in heads], axis=0)
        q_c = (q_c * scale).astype(BF16)
        q_r = (q_r * scale).astype(BF16)

        s = _dot_nt(q_c, kc)
        valid = c_end <= qpos4
        s = jnp.where(valid, s, NEG)
        e = jnp.where(valid, jnp.exp(s - jnp.max(s, axis=1, keepdims=True)), 0.0)
        l = jnp.sum(e, axis=1, keepdims=True)
        p = jnp.where(l > 0.0, e / jnp.where(l > 0.0, l, 1.0), 0.0)
        o_cmp = _dot(p.astype(BF16), vc)

        p_sum = p[0:tq]
        for r in range(1, r4):
            p_sum = p_sum + p[r * tq:(r + 1) * tq]
        p_hi, p_lo = _split(p_sum)
        imp = _dot(p_hi, ov_ref[...]) + _dot(p_lo, ov_ref[...])

        cand = (lane <= qblk) & (lane != 0) & (lane != qblk) & (lane != qblk - 1)
        work = jnp.where(cand, imp, -1.0)
        picked = jnp.zeros((tq, LANES), F32)
        for _ in range(NSA_PICKS):
            top = jnp.max(work, axis=1, keepdims=True)
            first = jnp.min(jnp.where(work == top, lane_f, float(LANES)), axis=1,
                            keepdims=True)
            hit = (lane_f == first) & (top >= 0.0)
            picked = jnp.where(hit, 1.0, picked)
            work = jnp.where(hit, -1.0, work)
        picked = picked.astype(BF16)

        def sel_body(t, state):
            k0 = pl.multiple_of(t * tk, tk)
            k = sel_ref[0, pl.ds(k0, tk), 0:LANES].astype(BF16)
            v = sel_ref[0, pl.ds(k0, tk), LANES:2 * LANES].astype(BF16)
            kpos = k0 + lax.broadcasted_iota(jnp.int32, (1, tk), 1)
            kblk = kpos // NSA_SEL_BLOCK
            chosen = _dot(picked, ex_ref[t]) > 0.5
            forced = (kblk == 0) | (kblk == qblk) | (kblk == qblk - 1)
            keep = jnp.where((chosen | forced) & (kpos <= qpos), 1.0, 0.0)
            return _flash_step(state, _dot_nt(q_r, k), _stack(keep, r4) > 0.5, v)

        o_sel = _flash_out(lax.fori_loop(0, n_sel_tiles, sel_body, _flash_init(r4 * tq)))

        state = _flash_init(r4 * tq)
        for jt, w_ref in enumerate(win_refs):
            k = w_ref[0, :, 0:LANES].astype(BF16)
            v = w_ref[0, :, LANES:2 * LANES].astype(BF16)
            wpos = win_start + jt * LANES + lane
            keep = jnp.where((wpos >= wpos0) & (wpos <= qpos) & (qpos - wpos < NSA_WINDOW),
                             1.0, 0.0)
            state = _flash_step(state, _dot_nt(q_r, k), _stack(keep, r4) > 0.5, v)
        o_win = _flash_out(state)

        for r, h in enumerate(heads):
            rows = slice(r * tq, (r + 1) * tq)
            o = (gates[:, 3 * h:3 * h + 1] * o_cmp[rows]
                 + gates[:, 3 * h + 1:3 * h + 2] * o_sel[rows]
                 + gates[:, 3 * h + 2:3 * h + 3] * o_win[rows])
            if h % 2 != g:
                o = pltpu.roll(o, HEAD_DIM, 1)
            outs.append(o)

    for pair in range(NSA_HEADS // 2):
        o_ref[0, :, pair * LANES:(pair + 1) * LANES] = jnp.where(
            lane < HEAD_DIM, outs[2 * pair], outs[2 * pair + 1])


def _nsa_attn(qp, qr, gates, kc, vc, sel_ctx, win_ctx, ov, ex, tq, tk, pos0, wpos0):
    b, t_q, _ = qp.shape
    t_c = sel_ctx.shape[1]
    n_win_tiles_total = win_ctx.shape[1] // LANES
    tile0 = pos0 // LANES - NSA_WINDOW // LANES - wpos0 // LANES

    def win_spec(jt):
        def idx(bi, i):
            t = tile0 + (i * tq) // LANES + jt
            return (bi, jnp.clip(t, 0, n_win_tiles_total - 1), 0)
        return pl.BlockSpec((1, LANES, 2 * LANES), idx)

    qspec = pl.BlockSpec((1, tq, NSA_HEADS * LANES), lambda bi, i: (bi, i, 0))
    return pl.pallas_call(
        functools.partial(_nsa_kernel, tq=tq, tk=tk, pos0=pos0, wpos0=wpos0),
        grid=(b, t_q // tq),
        in_specs=[qspec, qspec,
                  pl.BlockSpec((1, tq, LANES), lambda bi, i: (bi, i, 0)),
                  pl.BlockSpec((1,) + kc.shape[1:], lambda bi, i: (bi, 0, 0)),
                  pl.BlockSpec((1,) + vc.shape[1:], lambda bi, i: (bi, 0, 0)),
                  pl.BlockSpec((1, t_c, 2 * LANES), lambda bi, i: (bi, 0, 0)),
                  pl.BlockSpec(ov.shape, lambda bi, i: (0, 0)),
                  pl.BlockSpec(ex.shape, lambda bi, i: (0, 0, 0))]
                 + [win_spec(jt) for jt in range(N_WIN_TILES)],
        out_specs=pl.BlockSpec((1, tq, NSA_HEADS * HEAD_DIM), lambda bi, i: (bi, i, 0)),
        out_shape=jax.ShapeDtypeStruct((b, t_q, NSA_HEADS * HEAD_DIM), F32),
        compiler_params=_cparams(("parallel", "arbitrary")),
        name="nsa_attn",
    )(qp, qr, gates, kc, vc, sel_ctx, ov, ex, *([win_ctx] * N_WIN_TILES))


def _sb_kernel(q_ref, k_ref, v_ref, tri_ref, o_ref, *, tq, tk, pos0):
    i = pl.program_id(2)
    q0 = pos0 + i * tq
    scale = HEAD_DIM ** -0.5
    qpos = q0 + lax.broadcasted_iota(jnp.int32, (tq, 1), 0)
    lane = lax.broadcasted_iota(jnp.int32, (1, LANES), 1)
    n_tiles = (q0 + tq - 2) // tk + 1
    tri = tri_ref[...]
    qs = [(q_ref[0, :, hh * LANES:(hh + 1) * LANES] * scale).astype(BF16) for hh in range(2)]

    def tile(q, k, v, valid, acc, run):
        z = _dot_nt(q, k)
        soft = jnp.log(1.0 + jnp.exp(-jnp.abs(z)))
        log_break = jnp.minimum(z, 0.0) - soft
        log_stay = jnp.where(valid, log_break - z, 0.0)
        s_hi, s_lo = _split(log_stay)
        after = _dot(s_hi, tri) + _dot(s_lo, tri)
        a = jnp.where(valid, jnp.exp(log_break + after + run), 0.0)
        return acc + _dot(a.astype(BF16), v), run + jnp.sum(log_stay, axis=1, keepdims=True)

    def cond(carry):
        return (carry[0] < n_tiles) & (carry[1] > 0)

    def body(carry):
        step, _, acc0, run0, acc1, run1 = carry
        k0 = pl.multiple_of((n_tiles - 1 - step) * tk, tk)
        k = k_ref[0, pl.ds(k0, tk), :].astype(BF16)
        v = v_ref[0, pl.ds(k0, tk), :].astype(BF16)
        valid = (k0 + lax.broadcasted_iota(jnp.int32, (1, tk), 1)) < qpos
        acc0, run0 = tile(qs[0], k, v, valid, acc0, run0)
        acc1, run1 = tile(qs[1], k, v, valid, acc1, run1)
        live = jnp.maximum(jnp.max(run0), jnp.max(run1)) > SB_DEAD
        return step + 1, live.astype(jnp.int32), acc0, run0, acc1, run1

    zero_acc = jnp.zeros((tq, LANES), F32)
    zero_run = jnp.zeros((tq, 1), F32)
    _, _, acc0, _, acc1, _ = lax.while_loop(
        cond, body, (jnp.int32(0), jnp.int32(1), zero_acc, zero_run, zero_acc, zero_run))
    o_ref[0] = jnp.where(lane < HEAD_DIM, acc0, acc1)


def _sb_attn(q, kv, tri, tq, tk, pos0):
    b, t_q, _ = q.shape
    t_c = kv.shape[1]
    n_pairs = SB_HEADS // 2
    return pl.pallas_call(
        functools.partial(_sb_kernel, tq=tq, tk=tk, pos0=pos0),
        grid=(b, n_pairs, t_q // tq),
        in_specs=[pl.BlockSpec((1, tq, 2 * LANES), lambda bi, pr, i: (bi, i, pr)),
                  pl.BlockSpec((1, t_c, LANES), lambda bi, pr, i: (bi, 0, pr)),
                  pl.BlockSpec((1, t_c, LANES), lambda bi, pr, i: (bi, 0, n_pairs + pr)),
                  pl.BlockSpec(tri.shape, lambda bi, pr, i: (0, 0))],
        out_specs=pl.BlockSpec((1, tq, LANES), lambda bi, pr, i: (bi, i, pr)),
        out_shape=jax.ShapeDtypeStruct((b, t_q, SB_HEADS * HEAD_DIM), F32),
        compiler_params=_cparams(("parallel", "parallel", "arbitrary")),
        name="sb_attn",
    )(q, kv, kv, tri)


def _moba_kernel(q_ref, k_ref, v_ref, o_ref, km_scr, *, tq, pos0, n_blocks, bps):
    i = pl.program_id(2)
    blk = MOBA_BLOCK
    q0 = pos0 + i * tq
    own = q0 // blk
    scale = HEAD_DIM ** -0.5
    qpos = q0 + lax.broadcasted_iota(jnp.int32, (tq, 1), 0)
    lane = lax.broadcasted_iota(jnp.int32, (1, LANES), 1)
    lane_f = lane.astype(F32)

    @pl.when(i == 0)
    def _():
        km_scr[...] = jnp.zeros_like(km_scr)
        for n in range(n_blocks):
            km_scr[n:n + 1, :] = jnp.sum(k_ref[0, n * blk:(n + 1) * blk, :], axis=0,
                                         keepdims=True) * (1.0 / blk)

    km_hi, km_lo = _split(km_scr[...])
    q_s, picked, states = [], [], []
    own0 = pl.multiple_of(own * blk, blk)
    k_own = k_ref[0, pl.ds(own0, blk), :].astype(BF16)
    v_own = v_ref[0, pl.ds(own0, blk), :].astype(BF16)
    causal = (own0 + lax.broadcasted_iota(jnp.int32, (1, blk), 1)) <= qpos
    for hh in range(2):
        q = q_ref[0, :, hh * LANES:(hh + 1) * LANES]
        q_hi, q_lo = _split(q)
        gate = _dot_nt(q_hi, km_hi) + _dot_nt(q_lo, km_hi) + _dot_nt(q_hi, km_lo)
        work = jnp.where(lane < own, gate, NEG)
        pick = jnp.zeros((tq, LANES), F32)
        for _ in range(MOBA_TOPK):
            top = jnp.max(work, axis=1, keepdims=True)
            first = jnp.min(jnp.where(work == top, lane_f, float(LANES)), axis=1,
                            keepdims=True)
            hit = (lane_f == first) & (top > 0.5 * NEG)
            pick = jnp.where(hit, 1.0, pick)
            work = jnp.where(hit, NEG, work)
        picked.append(pick)
        q_s.append((q * scale).astype(BF16))
        states.append(_flash_step(_flash_init(tq), _dot_nt(q_s[hh], k_own), causal, v_own))

    def body(t, carry):
        k0 = pl.multiple_of(t * (bps * blk), bps * blk)
        k = k_ref[0, pl.ds(k0, bps * blk), :].astype(BF16)
        v = v_ref[0, pl.ds(k0, bps * blk), :].astype(BF16)
        out = []
        for hh in range(2):
            m, l, acc = carry[3 * hh:3 * hh + 3]
            s = _dot_nt(q_s[hh], k)
            rows = [jnp.sum(jnp.where(lane == t * bps + j, picked[hh], 0.0), axis=1,
                            keepdims=True) > 0.5 for j in range(bps)]
            parts = [s[:, j * blk:(j + 1) * blk] for j in range(bps)]
            m_new = m
            for j in range(bps):
                m_new = jnp.maximum(m_new, jnp.where(
                    rows[j], jnp.max(parts[j], axis=1, keepdims=True), NEG))
            alpha = jnp.exp(m - m_new)
            p = [jnp.exp(parts[j] - jnp.where(rows[j], m_new, -NEG)) for j in range(bps)]
            p = p[0] if bps == 1 else jnp.concatenate(p, axis=1)
            l = alpha * l + jnp.sum(p, axis=1, keepdims=True)
            acc = alpha * acc + _dot(p.astype(BF16), v)
            out += [m_new, l, acc]
        return tuple(out)

    n_steps = (own + bps - 1) // bps
    final = lax.fori_loop(0, n_steps, body, tuple(states[0]) + tuple(states[1]))
    o_ref[0] = jnp.where(lane < HEAD_DIM, _flash_out(final[0:3]), _flash_out(final[3:6]))


def _moba_t_kernel(q_ref, k_ref, v_ref, o_ref, km_scr, vt_scr, pk_scr, *, tq, pos0, n_blocks,
                   bps):
    i = pl.program_id(2)
    blk = MOBA_BLOCK
    sub = blk // LANES
    q0 = pos0 + i * tq
    own = q0 // blk
    scale = HEAD_DIM ** -0.5
    row = lax.broadcasted_iota(jnp.int32, (LANES, 1), 0)
    row_f = row.astype(F32)
    qpos = q0 + lax.broadcasted_iota(jnp.int32, (1, tq), 1)

    @pl.when(i == 0)
    def _():
        km_scr[...] = jnp.zeros_like(km_scr)
        for n in range(n_blocks):
            km_scr[n:n + 1, :] = jnp.sum(k_ref[0, n * blk:(n + 1) * blk, :], axis=0,
                                         keepdims=True) * (1.0 / blk)

        def transpose_block(n, carry):
            n0 = pl.multiple_of(n * blk, blk)
            vt = jnp.concatenate(
                [v_ref[0, pl.ds(n0 + j * LANES, LANES), :].T for j in range(sub)], axis=1)
            vt_scr[0, n] = jnp.where(row < HEAD_DIM, vt, 1.0).astype(BF16)
            vt_scr[1, n] = jnp.where(row >= HEAD_DIM, vt, 1.0).astype(BF16)
            return carry

        lax.fori_loop(0, n_blocks, transpose_block, 0)

    def weighted_values(hh, t, p):
        return _dot(vt_scr[hh, t], p)

    km_hi, km_lo = _split(km_scr[...])
    own0 = pl.multiple_of(own * blk, blk)
    k_own = k_ref[0, pl.ds(own0, blk), :].astype(BF16)
    causal = (own0 + lax.broadcasted_iota(jnp.int32, (blk, 1), 0)) <= qpos
    q_s, states = [], []
    for hh in range(2):
        q = q_ref[0, :, hh * LANES:(hh + 1) * LANES]
        q_hi, q_lo = _split(q)
        gate = _dot_nt(km_hi, q_hi) + _dot_nt(km_hi, q_lo) + _dot_nt(km_lo, q_hi)
        work = jnp.where(row < own, gate, NEG)
        pick = jnp.zeros((LANES, tq), F32)
        for _ in range(MOBA_TOPK):
            top = jnp.max(work, axis=0, keepdims=True)
            first = jnp.min(jnp.where(work == top, row_f, float(LANES)), axis=0, keepdims=True)
            hit = (row_f == first) & (top > 0.5 * NEG)
            pick = jnp.where(hit, 1.0, pick)
            work = jnp.where(hit, NEG, work)
        pk_scr[hh] = pick
        q_s.append((q * scale).astype(BF16))
        s = jnp.where(causal, _dot_nt(k_own, q_s[hh]), NEG)
        m = jnp.max(s, axis=0, keepdims=True)
        p = jnp.where(causal, jnp.exp(s - m), 0.0).astype(BF16)
        states += [m, weighted_values(hh, own, p)]

    def body(t, carry):
        k0 = pl.multiple_of(t * (bps * blk), bps * blk)
        k = k_ref[0, pl.ds(k0, bps * blk), :].astype(BF16)
        out = []
        for hh in range(2):
            m, acc = carry[2 * hh:2 * hh + 2]
            s = _dot_nt(k, q_s[hh])
            parts = [s[j * blk:(j + 1) * blk] for j in range(bps)]
            chosen = [pk_scr[hh, pl.ds(t * bps + j, 1), :] > 0.5 for j in range(bps)]
            m_new = m
            for j in range(bps):
                m_new = jnp.maximum(m_new, jnp.where(
                    chosen[j], jnp.max(parts[j], axis=0, keepdims=True), NEG))
            acc = jnp.exp(m - m_new) * acc
            for j in range(bps):
                p = jnp.exp(parts[j] - jnp.where(chosen[j], m_new, -NEG)).astype(BF16)
                acc = acc + weighted_values(hh, t * bps + j, p)
            out += [m_new, acc]
        return tuple(out)

    _, acc0, _, acc1 = lax.fori_loop(0, (own + bps - 1) // bps, body, tuple(states))
    o_t = jnp.where(row < HEAD_DIM, acc0 / acc0[HEAD_DIM:HEAD_DIM + 1, :], acc1 / acc1[0:1, :])
    o_ref[0] = o_t.T


def _moba_attn_t(q, kv, tq, pos0, bps):
    b, t_q, _ = q.shape
    t_c = kv.shape[1]
    n_pairs = MOBA_HEADS // 2
    n_blocks = t_c // MOBA_BLOCK
    assert MOBA_BLOCK % tq == 0 and pos0 % tq == 0 and tq % LANES == 0
    assert n_blocks % bps == 0
    return pl.pallas_call(
        functools.partial(_moba_t_kernel, tq=tq, pos0=pos0, n_blocks=n_blocks, bps=bps),
        grid=(b, n_pairs, t_q // tq),
        in_specs=[pl.BlockSpec((1, tq, 2 * LANES), lambda bi, pr, i: (bi, i, pr)),
                  pl.BlockSpec((1, t_c, LANES), lambda bi, pr, i: (bi, 0, pr)),
                  pl.BlockSpec((1, t_c, LANES), lambda bi, pr, i: (bi, 0, n_pairs + pr))],
        out_specs=pl.BlockSpec((1, tq, LANES), lambda bi, pr, i: (bi, i, pr)),
        out_shape=jax.ShapeDtypeStruct((b, t_q, MOBA_HEADS * HEAD_DIM), F32),
        scratch_shapes=[pltpu.VMEM((LANES, LANES), F32),
                        pltpu.VMEM((2, n_blocks, LANES, MOBA_BLOCK), BF16),
                        pltpu.VMEM((2, LANES, tq), F32)],
        compiler_params=_cparams(("parallel", "parallel", "arbitrary")),
        name="moba_attn_t",
    )(q, kv, kv)


def _moba_attn(q, kv, tq, pos0, bps):
    b, t_q, _ = q.shape
    t_c = kv.shape[1]
    n_pairs = MOBA_HEADS // 2
    assert MOBA_BLOCK % tq == 0 and pos0 % tq == 0 and (pos0 // MOBA_BLOCK) % bps == 0
    return pl.pallas_call(
        functools.partial(_moba_kernel, tq=tq, pos0=pos0, n_blocks=t_c // MOBA_BLOCK, bps=bps),
        grid=(b, n_pairs, t_q // tq),
        in_specs=[pl.BlockSpec((1, tq, 2 * LANES), lambda bi, pr, i: (bi, i, pr)),
                  pl.BlockSpec((1, t_c, LANES), lambda bi, pr, i: (bi, 0, pr)),
                  pl.BlockSpec((1, t_c, LANES), lambda bi, pr, i: (bi, 0, n_pairs + pr))],
        out_specs=pl.BlockSpec((1, tq, LANES), lambda bi, pr, i: (bi, i, pr)),
        out_shape=jax.ShapeDtypeStruct((b, t_q, MOBA_HEADS * HEAD_DIM), F32),
        scratch_shapes=[pltpu.VMEM((LANES, LANES), F32)],
        compiler_params=_cparams(("parallel", "parallel", "arbitrary")),
        name="moba_attn",
    )(q, kv, kv)


def _pad_heads(w, halves):
    d = w.shape[0]
    n_h = len(halves)
    onehot = jax.nn.one_hot(jnp.asarray(halves), 2, dtype=w.dtype)
    return jnp.einsum("dhe,hs->dhse", w.reshape(d, n_h, HEAD_DIM), onehot).reshape(d, n_h * LANES)


def _rope_tables(pos):
    half = HEAD_DIM // 2
    inv = 1.0 / (ROPE_THETA ** (jnp.arange(half, dtype=F32) / half))
    ang = pos.astype(F32)[:, None] * inv[None, :]
    cos = jnp.tile(jnp.cos(ang), (1, LANES // half))
    sin = jnp.tile(jnp.sin(ang), (1, LANES // half))
    upper = (jnp.arange(LANES) % HEAD_DIM) >= half
    return cos, jnp.where(upper, sin, 0.0), jnp.where(upper, 0.0, -sin)


def _overlap_matrix(n_ch):
    c = jnp.arange(n_ch)[:, None]
    n = jnp.arange(LANES)[None, :]
    return ((c >= 4 * n - 1) & (c <= 4 * n + 3)).astype(BF16)


def _expand_matrix(t_c, tk):
    blk = (jnp.arange(t_c) // NSA_SEL_BLOCK).reshape(t_c // tk, 1, tk)
    return (jnp.arange(LANES)[None, :, None] == blk).astype(BF16)


def _compress_weights(w1, w2):
    half = NSA_STRIDE * HEAD_DIM
    out = []
    for kv in range(2):
        wa = w1[kv, :half].reshape(NSA_STRIDE, HEAD_DIM, NSA_HID)
        wb = w1[kv, half:].reshape(NSA_STRIDE, HEAD_DIM, NSA_HID)
        ab = jnp.concatenate([wa, wb], axis=2)
        z = jnp.zeros_like(ab)
        g0 = jnp.concatenate([ab, z], axis=1)
        g1 = jnp.concatenate([z, ab], axis=1)
        out.append(jnp.concatenate([g0, g1], axis=2).reshape(NSA_STRIDE * LANES, 4 * NSA_HID))
    z2 = jnp.zeros_like(w2)
    w2p = jnp.concatenate([jnp.concatenate([w2, z2], axis=2),
                           jnp.concatenate([z2, w2], axis=2)], axis=1)
    return out[0].astype(BF16), out[1].astype(BF16), w2p.astype(BF16)


def _prep_weights(w_in_even, w_out_even, nsa_cmp_pe, nsa_cmp_w1, nsa_cmp_w2, w_ffn_gate,
                  w_ffn_up, w_ffn_down, w_in_odd, w_out_odd, w_router, w_exp_gate, w_exp_up,
                  w_exp_down, w_ple_proj, w_ple_gate):
    bf = lambda a: a.astype(BF16)
    we = w_in_even[0]
    qw = NSA_HEADS * HEAD_DIM
    kvw = 2 * LANES
    c0 = qw + 3 * kvw
    n_gate = 3 * NSA_HEADS
    s0 = c0 + n_gate
    sbw = SB_HEADS * HEAD_DIM
    even = [
        bf(_pad_heads(we[:, :qw], [h // NSA_GROUP for h in range(NSA_HEADS)])),
        bf(we[:, qw:qw + kvw]),
        bf(we[:, qw + kvw:qw + 2 * kvw]),
        bf(we[:, qw + 2 * kvw:c0]),
        bf(jnp.pad(we[:, c0:s0], ((0, 0), (0, LANES - n_gate)))),
        bf(_pad_heads(we[:, s0:s0 + sbw], [h % 2 for h in range(SB_HEADS)])),
        bf(we[:, s0 + sbw:]),
    ]
    wo = w_in_odd[0]
    mw = MOBA_HEADS * HEAD_DIM
    odd = [bf(_pad_heads(wo[:, :mw], [h % 2 for h in range(MOBA_HEADS)])), bf(wo[:, mw:])]
    w1k, w1v, w2p = _compress_weights(nsa_cmp_w1[0], nsa_cmp_w2[0])
    pe = bf(jnp.broadcast_to(nsa_cmp_pe[0].reshape(2, 1, -1), (2, 8, 2 * NSA_STRIDE * HEAD_DIM)))
    wr = jnp.pad(w_router[0], ((0, 0), (0, LANES - N_EXPERTS)))
    wr_hi, wr_lo = _split(wr)
    return dict(
        even=even, odd=odd, w1k=w1k, w1v=w1v, w2p=w2p, pe=pe, w1=bf(nsa_cmp_w1[0]),
        wo_a=bf(w_out_even[0][:qw]), wo_b=bf(w_out_even[0][qw:]),
        ffn=(bf(w_ffn_gate[0]), bf(w_ffn_up[0]), bf(w_ffn_down[0])),
        wo_c=bf(w_out_odd[0]), wr_hi=wr_hi, wr_lo=wr_lo,
        exp=(bf(w_exp_gate[0]), bf(w_exp_up[0]), bf(w_exp_down[0])),
        ple_proj=bf(w_ple_proj), ple_gate=bf(w_ple_gate))


EVEN_SEGS = ("rope_dual", "plain", ("rope", "none"), ("rope", "none"), ("sigmoid",), "plain",
             "plain")
ODD_SEGS = (("rope",) * 16, ("rope",) * 8 + ("none",) * 8)


def _trunk(x, p, pos0, past, W, norms, sizes):
    b, t, d = x.shape
    n = b * t
    tm, tq, tk_sel, tk_sb, tm_moe, tq_moba, moba_bps = sizes
    norm_mix, norm_ffn, norm_ple, norm_final = norms
    pos = pos0 + jnp.arange(t, dtype=jnp.int32)
    tabs = _rope_tables(pos)
    if t < tm:
        tabs = [jnp.tile(a, (tm // t, 1)) for a in tabs]
    h = x.reshape(n, d)
    row = lambda a: a.reshape(1, d)
    b3 = lambda a: a.reshape(b, t, a.shape[-1])

    qp, qr, cmp_r, sel_r, win_r, gates, sbq, sbkv = _proj(
        h, row(norm_mix[0]), tabs, W["even"], EVEN_SEGS, tm)
    if past is None:
        n_pages = t // PAGE
        ident = jnp.arange(b * n_pages, dtype=jnp.int32).reshape(b, n_pages)
        kc, vc = _compress(cmp_r.reshape(b * n_pages, PAGE, 2 * LANES), ident,
                           W["w1k"], W["w1v"], W["w2p"], W["pe"], W["w1"], 8)
        sel_ctx, win_ctx, sb_ctx = b3(sel_r), b3(win_r), b3(sbkv)
        wpos0 = 0
        win_state = win_ctx[:, -NSA_WINDOW:]
    else:
        pt = past["page_table"]
        tail = lambda a: jnp.pad(b3(a), ((0, 0), (0, MOBA_BLOCK - t), (0, 0)))
        pool = lambda c: c[0].reshape(c.shape[1], PAGE, -1)
        kc, vc = _compress(pool(past["cache_nsa_cmp"]), pt,
                           W["w1k"], W["w1v"], W["w2p"], W["pe"], W["w1"], 8)
        sel_ctx = _gather_ctx(pool(past["cache_nsa_sel"]), pt, tail(sel_r), 8)
        sb_ctx = _gather_ctx(pool(past["cache_sb"]), pt, tail(sbkv), 8)
        state = past["state_nsa_win"][0].reshape(b, NSA_WINDOW, 2 * LANES)
        win_all = jnp.concatenate([state, b3(win_r)], axis=1)
        win_ctx = jnp.pad(win_all, ((0, 0), (0, LANES - t), (0, 0)))
        wpos0 = pos0 - NSA_WINDOW
        win_state = win_all[:, -NSA_WINDOW:]
    t_c = sel_ctx.shape[1]
    o_a = _nsa_attn(b3(qp), b3(qr), b3(gates), kc, vc, sel_ctx, win_ctx,
                    _overlap_matrix(kc.shape[1]), _expand_matrix(t_c, tk_sel),
                    tq, tk_sel, pos0, wpos0)
    o_b = _sb_attn(b3(sbq), sb_ctx, jnp.tril(jnp.ones((tk_sb, tk_sb), BF16), -1),
                   tq, tk_sb, pos0)
    h = _outproj(h, o_a.reshape(n, -1), o_b.reshape(n, -1), W["wo_a"], W["wo_b"], tm)
    h = _ffn(h, row(norm_ffn[0]), *W["ffn"], tm, W["ffn"][0].shape[1] // 2)
    h = _ple(h, p[0].reshape(n, -1), row(norm_ple[0]), row(norm_final),
             W["ple_gate"][0], W["ple_proj"][0], tm, False)

    mq, mkv = _proj(h, row(norm_mix[1]), tabs, W["odd"], ODD_SEGS, tm)
    if past is None:
        moba_ctx = b3(mkv)
    else:
        moba_ctx = _gather_ctx(pool(past["cache_moba"]), pt, tail(mkv), 4)
    if tq_moba % LANES == 0:
        o_c = _moba_attn_t(b3(mq), moba_ctx, tq_moba, pos0, moba_bps)
    else:
        o_c = _moba_attn(b3(mq), moba_ctx, tq_moba, pos0, moba_bps)
    half = o_c.shape[-1] // 2
    o_c = o_c.reshape(n, -1)
    h = _outproj(h, o_c[:, :half], o_c[:, half:], W["wo_c"][:half], W["wo_c"][half:], tm)
    h = _moe(h, row(norm_ffn[1]), W["wr_hi"], W["wr_lo"], *W["exp"], tm_moe, 512)
    y = _ple(h, p[1].reshape(n, -1), row(norm_ple[1]), row(norm_final),
             W["ple_gate"][1], W["ple_proj"][1], tm, True)

    kv5 = lambda a, heads: a.reshape(1, b, -1, 2, heads, HEAD_DIM)
    return (y.reshape(b, t, d), kv5(cmp_r, 2), kv5(sel_r, 2), kv5(win_state, 2),
            kv5(sbkv, SB_HEADS), kv5(mkv, MOBA_HEADS))


def kernel(x_prompt, x_sample, cache_nsa_cmp, cache_nsa_sel, state_nsa_win, cache_sb, cache_moba,
           page_table, p_prompt, p_sample, norm_mix, norm_ffn, norm_ple, norm_final, w_in_even,
           w_out_even, nsa_cmp_pe, nsa_cmp_w1, nsa_cmp_w2, w_ffn_gate, w_ffn_up, w_ffn_down,
           w_in_odd, w_out_odd, w_router, w_exp_gate, w_exp_up, w_exp_down, w_ple_proj,
           w_ple_gate):
    W = _prep_weights(w_in_even, w_out_even, nsa_cmp_pe, nsa_cmp_w1, nsa_cmp_w2, w_ffn_gate,
                      w_ffn_up, w_ffn_down, w_in_odd, w_out_odd, w_router, w_exp_gate, w_exp_up,
                      w_exp_down, w_ple_proj, w_ple_gate)
    norms = (norm_mix, norm_ffn, norm_ple, norm_final)
    past = dict(cache_nsa_cmp=cache_nsa_cmp, cache_nsa_sel=cache_nsa_sel,
                state_nsa_win=state_nsa_win, cache_sb=cache_sb, cache_moba=cache_moba,
                page_table=page_table)
    past_len = page_table.shape[1] * cache_sb.shape[2]
    t_dec = x_sample.shape[1]
    n_dec = x_sample.shape[0] * t_dec
    y_p, cmp_p, sel_p, win_p, sb_p, moba_p = _trunk(
        x_prompt, p_prompt, 0, None, W, norms, (256, 128, 256, 128, 1024, MOBA_BLOCK, 4))
    y_s, cmp_s, sel_s, win_s, sb_s, moba_s = _trunk(
        x_sample, p_sample, past_len, past, W, norms, (n_dec, t_dec, 256, 128, n_dec, t_dec, 8))
    return (y_p, y_s, cmp_p, cmp_s, sel_p, sel_s, win_p, win_s, sb_p, sb_s, moba_p, moba_s)
```

```python
import functools

import jax
import jax.numpy as jnp
from jax import lax
from jax.experimental import pallas as pl
from jax.experimental.pallas import tpu as pltpu

F32 = jnp.float32
BF16 = jnp.bfloat16

LANES = 128
HEAD_DIM = 64
PAGE = 128
RMS_EPS = 1e-6
ROPE_THETA = 10000.0
NEG = -1e30

NSA_HEADS = 8
NSA_GROUP = 4
NSA_STRIDE = 16
NSA_HID = 128
NSA_SEL_BLOCK = 64
NSA_PICKS = 13
NSA_WINDOW = 512
SB_HEADS = 8
MOBA_HEADS = 16
MOBA_BLOCK = 256
MOBA_TOPK = 3
N_EXPERTS = 8
SB_DEAD = -110.0

VMEM_LIMIT = 56 * 1024 * 1024


def _cparams(sem):
    return pltpu.CompilerParams(dimension_semantics=sem, vmem_limit_bytes=VMEM_LIMIT)


def _dot(a, b):
    return jnp.dot(a, b, preferred_element_type=F32)


def _dot_nt(a, b):
    return lax.dot_general(a, b, (((1,), (1,)), ((), ())), preferred_element_type=F32)


def _split(x):
    hi = x.astype(BF16)
    lo = (x - hi.astype(F32)).astype(BF16)
    return hi, lo


def _rmsnorm(x, g):
    return x * lax.rsqrt(jnp.mean(x * x, axis=-1, keepdims=True) + RMS_EPS) * g


def _sigmoid(x):
    return 1.0 / (1.0 + jnp.exp(-x))


def _rope_tile(y, cos, sa, sb):
    return y * cos + pltpu.roll(y, 32, 1) * sa + pltpu.roll(y, 96, 1) * sb


def _proj_kernel(x_ref, g_ref, cos_ref, sa_ref, sb_ref, *refs, segs):
    n_seg = len(segs)
    w_refs = refs[:n_seg]
    o_refs = list(refs[n_seg:])
    nb = _rmsnorm(x_ref[...], g_ref[...]).astype(BF16)
    cos, sa, sb = cos_ref[...], sa_ref[...], sb_ref[...]
    for w_ref, kinds in zip(w_refs, segs):
        y = _dot(nb, w_ref[...])
        if kinds == "plain":
            o_refs.pop(0)[...] = y
            continue
        if kinds == "rope_dual":
            o_refs.pop(0)[...] = y
            kinds = ("rope",) * (y.shape[1] // LANES)
        o_ref = o_refs.pop(0)
        for t, kind in enumerate(kinds):
            yt = y[:, t * LANES:(t + 1) * LANES]
            if kind == "rope":
                yt = _rope_tile(yt, cos, sa, sb)
            elif kind == "sigmoid":
                yt = _sigmoid(yt)
            o_ref[:, t * LANES:(t + 1) * LANES] = yt


def _proj(x, g, tabs, weights, segs, tm):
    n, d = x.shape
    nblk = tabs[0].shape[0] // tm
    out_shape, out_specs = [], []
    for w, kinds in zip(weights, segs):
        for _ in range(2 if kinds == "rope_dual" else 1):
            out_shape.append(jax.ShapeDtypeStruct((n, w.shape[1]), F32))
            out_specs.append(pl.BlockSpec((tm, w.shape[1]), lambda i: (i, 0)))
    tab_spec = pl.BlockSpec((tm, LANES), lambda i: (i % nblk, 0))
    return pl.pallas_call(
        functools.partial(_proj_kernel, segs=tuple(segs)),
        grid=(n // tm,),
        in_specs=[pl.BlockSpec((tm, d), lambda i: (i, 0)),
                  pl.BlockSpec((1, d), lambda i: (0, 0)),
                  tab_spec, tab_spec, tab_spec]
                 + [pl.BlockSpec(w.shape, lambda i: (0, 0)) for w in weights],
        out_specs=out_specs,
        out_shape=out_shape,
        compiler_params=_cparams(("parallel",)),
        name="norm_proj",
    )(x, g, *tabs, *weights)


def _outproj_kernel(h_ref, a_ref, b_ref, wa_ref, wb_ref, o_ref):
    o_ref[...] = (h_ref[...] + _dot(a_ref[...].astype(BF16), wa_ref[...])
                  + _dot(b_ref[...].astype(BF16), wb_ref[...]))


def _outproj(h, a, b, wa, wb, tm):
    n, d = h.shape
    row = lambda w: pl.BlockSpec((tm, w), lambda i: (i, 0))
    full = lambda w: pl.BlockSpec(w.shape, lambda i: (0, 0))
    return pl.pallas_call(
        _outproj_kernel,
        grid=(n // tm,),
        in_specs=[row(d), row(a.shape[1]), row(b.shape[1]), full(wa), full(wb)],
        out_specs=row(d),
        out_shape=jax.ShapeDtypeStruct((n, d), F32),
        compiler_params=_cparams(("parallel",)),
        name="out_proj",
    )(h, a, b, wa, wb)


def _ffn_kernel(h_ref, g_ref, wg_ref, wu_ref, wd_ref, o_ref, n_scr, acc_scr):
    f = pl.program_id(1)

    @pl.when(f == 0)
    def _():
        n_scr[...] = _rmsnorm(h_ref[...], g_ref[...]).astype(BF16)
        acc_scr[...] = jnp.zeros_like(acc_scr)

    nb = n_scr[...]
    gate = _dot(nb, wg_ref[...])
    up = _dot(nb, wu_ref[...])
    hid = (gate * _sigmoid(gate) * up).astype(BF16)
    acc_scr[...] += _dot(hid, wd_ref[...])

    @pl.when(f == pl.num_programs(1) - 1)
    def _():
        o_ref[...] = h_ref[...] + acc_scr[...]


def _ffn(h, g, wg, wu, wd, tm, tf):
    n, d = h.shape
    dff = wg.shape[1]
    return pl.pallas_call(
        _ffn_kernel,
        grid=(n // tm, dff // tf),
        in_specs=[pl.BlockSpec((tm, d), lambda i, f: (i, 0)),
                  pl.BlockSpec((1, d), lambda i, f: (0, 0)),
                  pl.BlockSpec((d, tf), lambda i, f: (0, f)),
                  pl.BlockSpec((d, tf), lambda i, f: (0, f)),
                  pl.BlockSpec((tf, d), lambda i, f: (f, 0))],
        out_specs=pl.BlockSpec((tm, d), lambda i, f: (i, 0)),
        out_shape=jax.ShapeDtypeStruct((n, d), F32),
        scratch_shapes=[pltpu.VMEM((tm, d), BF16), pltpu.VMEM((tm, d), F32)],
        compiler_params=_cparams(("parallel", "arbitrary")),
        name="swiglu_ffn",
    )(h, g, wg, wu, wd)


def _ple_kernel(h_ref, p_ref, g_ref, gf_ref, wg_ref, wp_ref, o_ref, *, final):
    h = h_ref[...]
    gate = _sigmoid(_dot(_rmsnorm(h, g_ref[...]).astype(BF16), wg_ref[...]))
    out = h + gate * _dot(p_ref[...].astype(BF16), wp_ref[...])
    if final:
        out = _rmsnorm(out, gf_ref[...])
    o_ref[...] = out


def _ple(h, p, g, gf, wg, wp, tm, final):
    n, d = h.shape
    row = lambda w: pl.BlockSpec((tm, w), lambda i: (i, 0))
    full = lambda a: pl.BlockSpec(a.shape, lambda i: (0, 0))
    return pl.pallas_call(
        functools.partial(_ple_kernel, final=final),
        grid=(n // tm,),
        in_specs=[row(d), row(p.shape[1]), full(g), full(gf), full(wg), full(wp)],
        out_specs=row(d),
        out_shape=jax.ShapeDtypeStruct((n, d), F32),
        compiler_params=_cparams(("parallel",)),
        name="ple",
    )(h, p, g, gf, wg, wp)


def _moe_kernel(h_ref, g_ref, wr_hi_ref, wr_lo_ref, wg_ref, wu_ref, wd_ref, o_ref,
                n_scr, gw_scr, acc_scr):
    e = pl.program_id(1)
    f = pl.program_id(2)
    lane = lax.broadcasted_iota(jnp.int32, (1, LANES), 1)

    @pl.when((e == 0) & (f == 0))
    def _():
        n = _rmsnorm(h_ref[...], g_ref[...])
        n_hi, n_lo = _split(n)
        n_scr[...] = n_hi
        logits = (_dot(n_hi, wr_hi_ref[...]) + _dot(n_lo, wr_hi_ref[...])
                  + _dot(n_hi, wr_lo_ref[...]))
        logits = jnp.where(lane < N_EXPERTS, logits, NEG)
        lane_f = lane.astype(F32)
        v1 = jnp.max(logits, axis=1, keepdims=True)
        i1 = jnp.min(jnp.where(logits == v1, lane_f, float(LANES)), axis=1, keepdims=True)
        rest = jnp.where(lane_f == i1, NEG, logits)
        v2 = jnp.max(rest, axis=1, keepdims=True)
        i2 = jnp.min(jnp.where(rest == v2, lane_f, float(LANES)), axis=1, keepdims=True)
        e2 = jnp.exp(v2 - v1)
        g1 = 1.0 / (1.0 + e2)
        gw_scr[...] = jnp.where(lane_f == i1, g1, jnp.where(lane_f == i2, e2 * g1, 0.0))
        acc_scr[...] = jnp.zeros_like(acc_scr)

    nb = n_scr[...]
    gate = _dot(nb, wg_ref[0])
    up = _dot(nb, wu_ref[0])
    hid = (gate * _sigmoid(gate) * up).astype(BF16)
    col = jnp.sum(jnp.where(lane == e, gw_scr[...], 0.0), axis=1, keepdims=True)
    acc_scr[...] += col * _dot(hid, wd_ref[0])

    @pl.when((e == pl.num_programs(1) - 1) & (f == pl.num_programs(2) - 1))
    def _():
        o_ref[...] = h_ref[...] + acc_scr[...]


def _moe(h, g, wr_hi, wr_lo, wg, wu, wd, tm, tf):
    n, d = h.shape
    n_e, _, dff = wg.shape
    return pl.pallas_call(
        _moe_kernel,
        grid=(n // tm, n_e, dff // tf),
        in_specs=[pl.BlockSpec((tm, d), lambda i, e, f: (i, 0)),
                  pl.BlockSpec((1, d), lambda i, e, f: (0, 0)),
                  pl.BlockSpec(wr_hi.shape, lambda i, e, f: (0, 0)),
                  pl.BlockSpec(wr_lo.shape, lambda i, e, f: (0, 0)),
                  pl.BlockSpec((1, d, tf), lambda i, e, f: (e, 0, f)),
                  pl.BlockSpec((1, d, tf), lambda i, e, f: (e, 0, f)),
                  pl.BlockSpec((1, tf, d), lambda i, e, f: (e, f, 0))],
        out_specs=pl.BlockSpec((tm, d), lambda i, e, f: (i, 0)),
        out_shape=jax.ShapeDtypeStruct((n, d), F32),
        scratch_shapes=[pltpu.VMEM((tm, d), BF16), pltpu.VMEM((tm, LANES), F32),
                        pltpu.VMEM((tm, d), F32)],
        compiler_params=_cparams(("parallel", "arbitrary", "arbitrary")),
        name="moe_ffn",
    )(h, g, wr_hi, wr_lo, wg, wu, wd)


def _gather_kernel(pt_ref, *refs, n_pg):
    del pt_ref
    page_refs, tail_ref, o_ref = refs[:n_pg], refs[n_pg], refs[n_pg + 1]
    s = pl.program_id(1)
    last = pl.num_programs(1) - 1

    n_heads = page_refs[0].shape[4]
    half = n_heads * HEAD_DIM

    @pl.when(s < last)
    def _():
        rows_per, heads_per = 32, min(n_heads, 8)
        for i, p_ref in enumerate(page_refs):
            for kv in range(2):
                for r0 in range(0, PAGE, rows_per):
                    for h0 in range(0, n_heads, heads_per):
                        by_head = pltpu.einshape(
                            "phd->hpd", p_ref[0, 0, r0:r0 + rows_per, kv, h0:h0 + heads_per])
                        for h in range(heads_per):
                            c0 = kv * half + (h0 + h) * HEAD_DIM
                            o_ref[0, i * PAGE + r0:i * PAGE + r0 + rows_per,
                                  c0:c0 + HEAD_DIM] = by_head[h]

    @pl.when(s == last)
    def _():
        o_ref[0, 0:tail_ref.shape[1], :] = tail_ref[0]


def _gather_ctx(cache, page_table, tail, n_pg):
    b, n_pages = page_table.shape
    n_heads = cache.shape[4]
    w = 2 * n_heads * HEAD_DIM
    n_tail = tail.shape[1]
    steps = n_pages // n_pg

    def page_spec(i):
        return pl.BlockSpec(
            (1, 1, PAGE, 2, n_heads, HEAD_DIM),
            lambda bi, s, pt: (0, pt[bi, jnp.minimum(s, steps - 1) * n_pg + i], 0, 0, 0, 0))

    return pl.pallas_call(
        functools.partial(_gather_kernel, n_pg=n_pg),
        grid_spec=pltpu.PrefetchScalarGridSpec(
            num_scalar_prefetch=1,
            grid=(b, steps + 1),
            in_specs=[page_spec(i) for i in range(n_pg)]
                     + [pl.BlockSpec((1, n_tail, w), lambda bi, s, pt: (bi, 0, 0))],
            out_specs=pl.BlockSpec((1, n_pg * PAGE, w), lambda bi, s, pt: (bi, s, 0)),
        ),
        out_shape=jax.ShapeDtypeStruct((b, n_pages * PAGE + n_tail, w), F32),
        compiler_params=_cparams(("parallel", "arbitrary")),
        name="paged_gather",
    )(page_table, *([cache] * n_pg), tail)


def _compress_kernel(pt_ref, *refs, n_pg):
    del pt_ref
    k_refs, v_refs = refs[:n_pg], refs[n_pg:2 * n_pg]
    (w1k_ref, w1v_ref, w2_ref, pe_ref, w1_ref, kc_ref, vc_ref,
     ak_scr, av_scr) = refs[2 * n_pg:]
    s = pl.program_id(1)
    n_ch = ak_scr.shape[0]
    per_page = PAGE // NSA_STRIDE

    for p_refs, a_scr in ((k_refs, ak_scr), (v_refs, av_scr)):
        for i, p_ref in enumerate(p_refs):
            base = pl.multiple_of((s * n_pg + i) * per_page, per_page)
            for j in range(NSA_STRIDE):
                a_scr[pl.ds(base, per_page), j * LANES:(j + 1) * LANES] = (
                    p_ref[0, pl.ds(j, per_page, stride=NSA_STRIDE), :])

    @pl.when(s == pl.num_programs(1) - 1)
    def _():
        for kv, (a_scr, w1p_ref, o_ref) in enumerate(((ak_scr, w1k_ref, kc_ref),
                                                      (av_scr, w1v_ref, vc_ref))):
            r = _dot(a_scr[...].astype(BF16), w1p_ref[...])
            bias = _dot(pe_ref[kv], w1_ref[kv])[0:1, :]
            hid = []
            for g in range(2):
                a = r[:, (2 * g) * NSA_HID:(2 * g + 1) * NSA_HID]
                b_next = pltpu.roll(r[:, (2 * g + 1) * NSA_HID:(2 * g + 2) * NSA_HID],
                                    n_ch - 1, 0)
                pre = a + b_next + bias
                hid.append(pre * _sigmoid(pre))
            hid = jnp.concatenate(hid, axis=1).astype(BF16)
            o_ref[0] = _dot(hid, w2_ref[kv])


def _compress(pool, page_table, w1k, w1v, w2, pe, w1, n_pg):
    b, n_pages = page_table.shape
    n_ch = n_pages * (PAGE // NSA_STRIDE)
    steps = n_pages // n_pg
    full = lambda a: pl.BlockSpec(a.shape, lambda bi, s, pt: (0,) * a.ndim)

    def page_spec(i, kv):
        return pl.BlockSpec((1, PAGE, LANES), lambda bi, s, pt: (pt[bi, s * n_pg + i], 0, kv))

    out_spec = pl.BlockSpec((1, n_ch, LANES), lambda bi, s, pt: (bi, 0, 0))
    return pl.pallas_call(
        functools.partial(_compress_kernel, n_pg=n_pg),
        grid_spec=pltpu.PrefetchScalarGridSpec(
            num_scalar_prefetch=1,
            grid=(b, steps),
            in_specs=[page_spec(i, kv) for kv in range(2) for i in range(n_pg)]
                     + [full(w1k), full(w1v), full(w2), full(pe), full(w1)],
            out_specs=[out_spec, out_spec],
            scratch_shapes=[pltpu.VMEM((n_ch, NSA_STRIDE * LANES), F32),
                            pltpu.VMEM((n_ch, NSA_STRIDE * LANES), F32)],
        ),
        out_shape=[jax.ShapeDtypeStruct((b, n_ch, LANES), F32)] * 2,
        compiler_params=_cparams(("parallel", "arbitrary")),
        name="nsa_compress",
    )(page_table, *([pool] * (2 * n_pg)), w1k, w1v, w2, pe, w1)


def _flash_step(state, s, valid, v):
    m, l, acc = state
    s = jnp.where(valid, s, NEG)
    m_new = jnp.maximum(m, jnp.max(s, axis=1, keepdims=True))
    alpha = jnp.exp(m - m_new)
    p = jnp.where(valid, jnp.exp(s - m_new), 0.0)
    l = alpha * l + jnp.sum(p, axis=1, keepdims=True)
    acc = alpha * acc + _dot(p.astype(BF16), v)
    return m_new, l, acc


def _flash_init(rows):
    return (jnp.full((rows, 1), NEG, F32), jnp.zeros((rows, 1), F32),
            jnp.zeros((rows, LANES), F32))


def _flash_out(state):
    _, l, acc = state
    return jnp.where(l > 0.0, acc / jnp.where(l > 0.0, l, 1.0), 0.0)


def _stack(x, times):
    return jnp.concatenate([x] * times, axis=0)


N_WIN_TILES = NSA_WINDOW // LANES + 1


def _nsa_kernel(qp_ref, qr_ref, gt_ref, kc_ref, vc_ref, sel_ref, ov_ref, ex_ref, *refs,
                tq, tk, pos0, wpos0):
    win_refs, o_ref = refs[:N_WIN_TILES], refs[N_WIN_TILES]
    i = pl.program_id(1)
    q0 = pos0 + i * tq
    scale = HEAD_DIM ** -0.5
    r4 = NSA_GROUP
    qpos = q0 + lax.broadcasted_iota(jnp.int32, (tq, 1), 0)
    qpos4 = _stack(qpos, r4)
    qblk = qpos // NSA_SEL_BLOCK
    lane = lax.broadcasted_iota(jnp.int32, (1, LANES), 1)
    lane_f = lane.astype(F32)
    n_ch = kc_ref.shape[1]
    c_end = lax.broadcasted_iota(jnp.int32, (1, n_ch), 1) * NSA_STRIDE + (2 * NSA_STRIDE - 1)
    kc = kc_ref[0].astype(BF16)
    vc = vc_ref[0].astype(BF16)
    gates = gt_ref[0]
    n_sel_tiles = (q0 + tq - 1) // tk + 1
    win_start = (q0 // LANES) * LANES - NSA_WINDOW

    q_rot, o_cmp, picked = [], [], []
    for g in range(2):
        heads = range(g * r4, (g + 1) * r4)
        q_c = jnp.concatenate([qp_ref[0, :, h * LANES:(h + 1) * LANES] for h in heads], axis=0)
        q_r = jnp.concatenate([qr_ref[0, :, h * LANES:(h + 1) * LANES] for h in heads], axis=0)
        q_c = (q_c * scale).astype(BF16)
        q_rot.append((q_r * scale).astype(BF16))

        s = _dot_nt(q_c, kc)
        valid = c_end <= qpos4
        s = jnp.where(valid, s, NEG)
        e = jnp.where(valid, jnp.exp(s - jnp.max(s, axis=1, keepdims=True)), 0.0)
        l = jnp.sum(e, axis=1, keepdims=True)
        p = jnp.where(l > 0.0, e / jnp.where(l > 0.0, l, 1.0), 0.0)
        o_cmp.append(_dot(p.astype(BF16), vc))

        p_sum = p[0:tq]
        for r in range(1, r4):
            p_sum = p_sum + p[r * tq:(r + 1) * tq]
        p_hi, p_lo = _split(p_sum)
        imp = _dot(p_hi, ov_ref[...]) + _dot(p_lo, ov_ref[...])

        cand = (lane <= qblk) & (lane != 0) & (lane != qblk) & (lane != qblk - 1)
        work = jnp.where(cand, imp, -1.0)
        pick = jnp.zeros((tq, LANES), F32)
        for _ in range(NSA_PICKS):
            top = jnp.max(work, axis=1, keepdims=True)
            first = jnp.min(jnp.where(work == top, lane_f, float(LANES)), axis=1,
                            keepdims=True)
            hit = (lane_f == first) & (top >= 0.0)
            pick = jnp.where(hit, 1.0, pick)
            work = jnp.where(hit, -1.0, work)
        picked.append(pick.astype(BF16))

    def sel_body(t, states):
        k0 = pl.multiple_of(t * tk, tk)
        k = sel_ref[0, pl.ds(k0, tk), 0:LANES].astype(BF16)
        v = sel_ref[0, pl.ds(k0, tk), LANES:2 * LANES].astype(BF16)
        kpos = k0 + lax.broadcasted_iota(jnp.int32, (1, tk), 1)
        kblk = kpos // NSA_SEL_BLOCK
        forced = (kblk == 0) | (kblk == qblk) | (kblk == qblk - 1)
        out = ()
        for g in range(2):
            chosen = _dot(picked[g], ex_ref[t]) > 0.5
            keep = jnp.where((chosen | forced) & (kpos <= qpos), 1.0, 0.0)
            out += _flash_step(states[3 * g:3 * g + 3], _dot_nt(q_rot[g], k),
                               _stack(keep, r4) > 0.5, v)
        return out

    sel_states = lax.fori_loop(0, n_sel_tiles, sel_body, _flash_init(r4 * tq) * 2)
    o_sel = [_flash_out(sel_states[0:3]), _flash_out(sel_states[3:6])]

    win_states = [_flash_init(r4 * tq), _flash_init(r4 * tq)]
    for jt, w_ref in enumerate(win_refs):
        k = w_ref[0, :, 0:LANES].astype(BF16)
        v = w_ref[0, :, LANES:2 * LANES].astype(BF16)
        wpos = win_start + jt * LANES + lane
        keep = jnp.where((wpos >= wpos0) & (wpos <= qpos) & (qpos - wpos < NSA_WINDOW),
                         1.0, 0.0)
        keep4 = _stack(keep, r4) > 0.5
        for g in range(2):
            win_states[g] = _flash_step(win_states[g], _dot_nt(q_rot[g], k), keep4, v)
    o_win = [_flash_out(win_states[0]), _flash_out(win_states[1])]

    outs = []
    for g in range(2):
        for r in range(r4):
            h = g * r4 + r
            rows = slice(r * tq, (r + 1) * tq)
            o = (gates[:, 3 * h:3 * h + 1] * o_cmp[g][rows]
                 + gates[:, 3 * h + 1:3 * h + 2] * o_sel[g][rows]
                 + gates[:, 3 * h + 2:3 * h + 3] * o_win[g][rows])
            if h % 2 != g:
                o = pltpu.roll(o, HEAD_DIM, 1)
            outs.append(o)

    for pair in range(NSA_HEADS // 2):
        o_ref[0, :, pair * LANES:(pair + 1) * LANES] = jnp.where(
            lane < HEAD_DIM, outs[2 * pair], outs[2 * pair + 1])


def _nsa_attn(qp, qr, gates, kc, vc, sel_ctx, win_ctx, ov, ex, tq, tk, pos0, wpos0):
    b, t_q, _ = qp.shape
    t_c = sel_ctx.shape[1]
    n_win_tiles_total = win_ctx.shape[1] // LANES
    tile0 = pos0 // LANES - NSA_WINDOW // LANES - wpos0 // LANES

    def win_spec(jt):
        def idx(bi, i):
            t = tile0 + (i * tq) // LANES + jt
            return (bi, jnp.clip(t, 0, n_win_tiles_total - 1), 0)
        return pl.BlockSpec((1, LANES, 2 * LANES), idx)

    qspec = pl.BlockSpec((1, tq, NSA_HEADS * LANES), lambda bi, i: (bi, i, 0))
    return pl.pallas_call(
        functools.partial(_nsa_kernel, tq=tq, tk=tk, pos0=pos0, wpos0=wpos0),
        grid=(b, t_q // tq),
        in_specs=[qspec, qspec,
                  pl.BlockSpec((1, tq, LANES), lambda bi, i: (bi, i, 0)),
                  pl.BlockSpec((1,) + kc.shape[1:], lambda bi, i: (bi, 0, 0)),
                  pl.BlockSpec((1,) + vc.shape[1:], lambda bi, i: (bi, 0, 0)),
                  pl.BlockSpec((1, t_c, 2 * LANES), lambda bi, i: (bi, 0, 0)),
                  pl.BlockSpec(ov.shape, lambda bi, i: (0, 0)),
                  pl.BlockSpec(ex.shape, lambda bi, i: (0, 0, 0))]
                 + [win_spec(jt) for jt in range(N_WIN_TILES)],
        out_specs=pl.BlockSpec((1, tq, NSA_HEADS * HEAD_DIM), lambda bi, i: (bi, i, 0)),
        out_shape=jax.ShapeDtypeStruct((b, t_q, NSA_HEADS * HEAD_DIM), F32),
        compiler_params=_cparams(("parallel", "arbitrary")),
        name="nsa_attn",
    )(qp, qr, gates, kc, vc, sel_ctx, ov, ex, *([win_ctx] * N_WIN_TILES))


def _sb_kernel(q_ref, k_ref, v_ref, tri_ref, o_ref, *, tq, tk, pos0):
    i = pl.program_id(2)
    q0 = pos0 + i * tq
    scale = HEAD_DIM ** -0.5
    qpos = q0 + lax.broadcasted_iota(jnp.int32, (tq, 1), 0)
    lane = lax.broadcasted_iota(jnp.int32, (1, LANES), 1)
    n_tiles = (q0 + tq - 2) // tk + 1
    tri = tri_ref[...]
    qs = [(q_ref[0, :, hh * LANES:(hh + 1) * LANES] * scale).astype(BF16) for hh in range(2)]

    def tile(q, k, v, valid, acc, run):
        z = _dot_nt(q, k)
        soft = jnp.log(1.0 + jnp.exp(-jnp.abs(z)))
        log_break = jnp.minimum(z, 0.0) - soft
        log_stay = jnp.where(valid, log_break - z, 0.0)
        s_hi, s_lo = _split(log_stay)
        after = _dot(s_hi, tri) + _dot(s_lo, tri)
        a = jnp.where(valid, jnp.exp(log_break + after + run), 0.0)
        return acc + _dot(a.astype(BF16), v), run + jnp.sum(log_stay, axis=1, keepdims=True)

    def cond(carry):
        return (carry[0] < n_tiles) & (carry[1] > 0)

    def body(carry):
        step, _, acc0, run0, acc1, run1 = carry
        k0 = pl.multiple_of((n_tiles - 1 - step) * tk, tk)
        k = k_ref[0, pl.ds(k0, tk), :].astype(BF16)
        v = v_ref[0, pl.ds(k0, tk), :].astype(BF16)
        valid = (k0 + lax.broadcasted_iota(jnp.int32, (1, tk), 1)) < qpos
        acc0, run0 = tile(qs[0], k, v, valid, acc0, run0)
        acc1, run1 = tile(qs[1], k, v, valid, acc1, run1)
        live = jnp.maximum(jnp.max(run0), jnp.max(run1)) > SB_DEAD
        return step + 1, live.astype(jnp.int32), acc0, run0, acc1, run1

    zero_acc = jnp.zeros((tq, LANES), F32)
    zero_run = jnp.zeros((tq, 1), F32)
    _, _, acc0, _, acc1, _ = lax.while_loop(
        cond, body, (jnp.int32(0), jnp.int32(1), zero_acc, zero_run, zero_acc, zero_run))
    o_ref[0] = jnp.where(lane < HEAD_DIM, acc0, acc1)


def _sb_attn(q, kv, tri, tq, tk, pos0):
    b, t_q, _ = q.shape
    t_c = kv.shape[1]
    n_pairs = SB_HEADS // 2
    return pl.pallas_call(
        functools.partial(_sb_kernel, tq=tq, tk=tk, pos0=pos0),
        grid=(b, n_pairs, t_q // tq),
        in_specs=[pl.BlockSpec((1, tq, 2 * LANES), lambda bi, pr, i: (bi, i, pr)),
                  pl.BlockSpec((1, t_c, LANES), lambda bi, pr, i: (bi, 0, pr)),
                  pl.BlockSpec((1, t_c, LANES), lambda bi, pr, i: (bi, 0, n_pairs + pr)),
                  pl.BlockSpec(tri.shape, lambda bi, pr, i: (0, 0))],
        out_specs=pl.BlockSpec((1, tq, LANES), lambda bi, pr, i: (bi, i, pr)),
        out_shape=jax.ShapeDtypeStruct((b, t_q, SB_HEADS * HEAD_DIM), F32),
        compiler_params=_cparams(("parallel", "parallel", "arbitrary")),
        name="sb_attn",
    )(q, kv, kv, tri)


def _moba_kernel(q_ref, k_ref, v_ref, o_ref, km_scr, *, tq, pos0, n_blocks, bps):
    i = pl.program_id(2)
    blk = MOBA_BLOCK
    q0 = pos0 + i * tq
    own = q0 // blk
    scale = HEAD_DIM ** -0.5
    qpos = q0 + lax.broadcasted_iota(jnp.int32, (tq, 1), 0)
    lane = lax.broadcasted_iota(jnp.int32, (1, LANES), 1)
    lane_f = lane.astype(F32)

    @pl.when(i == 0)
    def _():
        km_scr[...] = jnp.zeros_like(km_scr)
        for n in range(n_blocks):
            km_scr[n:n + 1, :] = jnp.sum(k_ref[0, n * blk:(n + 1) * blk, :], axis=0,
                                         keepdims=True) * (1.0 / blk)

    km_hi, km_lo = _split(km_scr[...])
    q_s, picked, states = [], [], []
    own0 = pl.multiple_of(own * blk, blk)
    k_own = k_ref[0, pl.ds(own0, blk), :].astype(BF16)
    v_own = v_ref[0, pl.ds(own0, blk), :].astype(BF16)
    causal = (own0 + lax.broadcasted_iota(jnp.int32, (1, blk), 1)) <= qpos
    for hh in range(2):
        q = q_ref[0, :, hh * LANES:(hh + 1) * LANES]
        q_hi, q_lo = _split(q)
        gate = _dot_nt(q_hi, km_hi) + _dot_nt(q_lo, km_hi) + _dot_nt(q_hi, km_lo)
        work = jnp.where(lane < own, gate, NEG)
        pick = jnp.zeros((tq, LANES), F32)
        for _ in range(MOBA_TOPK):
            top = jnp.max(work, axis=1, keepdims=True)
            first = jnp.min(jnp.where(work == top, lane_f, float(LANES)), axis=1,
                            keepdims=True)
            hit = (lane_f == first) & (top > 0.5 * NEG)
            pick = jnp.where(hit, 1.0, pick)
            work = jnp.where(hit, NEG, work)
        picked.append(pick)
        q_s.append((q * scale).astype(BF16))
        states.append(_flash_step(_flash_init(tq), _dot_nt(q_s[hh], k_own), causal, v_own))

    def body(t, carry):
        k0 = pl.multiple_of(t * (bps * blk), bps * blk)
        k = k_ref[0, pl.ds(k0, bps * blk), :].astype(BF16)
        v = v_ref[0, pl.ds(k0, bps * blk), :].astype(BF16)
        out = []
        for hh in range(2):
            m, l, acc = carry[3 * hh:3 * hh + 3]
            s = _dot_nt(q_s[hh], k)
            rows = [jnp.sum(jnp.where(lane == t * bps + j, picked[hh], 0.0), axis=1,
                            keepdims=True) > 0.5 for j in range(bps)]
            parts = [s[:, j * blk:(j + 1) * blk] for j in range(bps)]
            m_new = m
            for j in range(bps):
                m_new = jnp.maximum(m_new, jnp.where(
                    rows[j], jnp.max(parts[j], axis=1, keepdims=True), NEG))
            alpha = jnp.exp(m - m_new)
            p = [jnp.exp(parts[j] - jnp.where(rows[j], m_new, -NEG)) for j in range(bps)]
            p = p[0] if bps == 1 else jnp.concatenate(p, axis=1)
            l = alpha * l + jnp.sum(p, axis=1, keepdims=True)
            acc = alpha * acc + _dot(p.astype(BF16), v)
            out += [m_new, l, acc]
        return tuple(out)

    n_steps = (own + bps - 1) // bps
    final = lax.fori_loop(0, n_steps, body, tuple(states[0]) + tuple(states[1]))
    o_ref[0] = jnp.where(lane < HEAD_DIM, _flash_out(final[0:3]), _flash_out(final[3:6]))


def _moba_t_kernel(q_ref, k_ref, v_ref, o_ref, km_scr, vt_scr, pk_scr, *, tq, pos0, n_blocks,
                   bps):
    i = pl.program_id(2)
    blk = MOBA_BLOCK
    sub = blk // LANES
    q0 = pos0 + i * tq
    own = q0 // blk
    scale = HEAD_DIM ** -0.5
    row = lax.broadcasted_iota(jnp.int32, (LANES, 1), 0)
    row_f = row.astype(F32)
    qpos = q0 + lax.broadcasted_iota(jnp.int32, (1, tq), 1)

    @pl.when(i == 0)
    def _():
        km_scr[...] = jnp.zeros_like(km_scr)
        for n in range(n_blocks):
            km_scr[n:n + 1, :] = jnp.sum(k_ref[0, n * blk:(n + 1) * blk, :], axis=0,
                                         keepdims=True) * (1.0 / blk)

        def transpose_block(n, carry):
            n0 = pl.multiple_of(n * blk, blk)
            vt = jnp.concatenate(
                [v_ref[0, pl.ds(n0 + j * LANES, LANES), :].T for j in range(sub)], axis=1)
            vt_scr[0, n] = jnp.where(row < HEAD_DIM, vt, 1.0).astype(BF16)
            vt_scr[1, n] = jnp.where(row >= HEAD_DIM, vt, 1.0).astype(BF16)
            return carry

        lax.fori_loop(0, n_blocks, transpose_block, 0)

    def weighted_values(hh, t, p):
        return _dot(vt_scr[hh, t], p)

    km_hi, km_lo = _split(km_scr[...])
    own0 = pl.multiple_of(own * blk, blk)
    k_own = k_ref[0, pl.ds(own0, blk), :].astype(BF16)
    causal = (own0 + lax.broadcasted_iota(jnp.int32, (blk, 1), 0)) <= qpos
    q_s, states = [], []
    for hh in range(2):
        q = q_ref[0, :, hh * LANES:(hh + 1) * LANES]
        q_hi, q_lo = _split(q)
        gate = _dot_nt(km_hi, q_hi) + _dot_nt(km_hi, q_lo) + _dot_nt(km_lo, q_hi)
        work = jnp.where(row < own, gate, NEG)
        pick = jnp.zeros((LANES, tq), F32)
        for _ in range(MOBA_TOPK):
            top = jnp.max(work, axis=0, keepdims=True)
            first = jnp.min(jnp.where(work == top, row_f, float(LANES)), axis=0, keepdims=True)
            hit = (row_f == first) & (top > 0.5 * NEG)
            pick = jnp.where(hit, 1.0, pick)
            work = jnp.where(hit, NEG, work)
        pk_scr[hh] = pick
        q_s.append((q * scale).astype(BF16))
        s = jnp.where(causal, _dot_nt(k_own, q_s[hh]), NEG)
        m = jnp.max(s, axis=0, keepdims=True)
        p = jnp.where(causal, jnp.exp(s - m), 0.0).astype(BF16)
        states += [m, weighted_values(hh, own, p)]

    def body(t, carry):
        k0 = pl.multiple_of(t * (bps * blk), bps * blk)
        k = k_ref[0, pl.ds(k0, bps * blk), :].astype(BF16)
        out = []
        for hh in range(2):
            m, acc = carry[2 * hh:2 * hh + 2]
            s = _dot_nt(k, q_s[hh])
            parts = [s[j * blk:(j + 1) * blk] for j in range(bps)]
            chosen = [pk_scr[hh, pl.ds(t * bps + j, 1), :] > 0.5 for j in range(bps)]
            m_new = m
            for j in range(bps):
                m_new = jnp.maximum(m_new, jnp.where(
                    chosen[j], jnp.max(parts[j], axis=0, keepdims=True), NEG))
            acc = jnp.exp(m - m_new) * acc
            for j in range(bps):
                p = jnp.exp(parts[j] - jnp.where(chosen[j], m_new, -NEG)).astype(BF16)
                acc = acc + weighted_values(hh, t * bps + j, p)
            out += [m_new, acc]
        return tuple(out)

    _, acc0, _, acc1 = lax.fori_loop(0, (own + bps - 1) // bps, body, tuple(states))
    o_t = jnp.where(row < HEAD_DIM, acc0 / acc0[HEAD_DIM:HEAD_DIM + 1, :], acc1 / acc1[0:1, :])
    o_ref[0] = o_t.T


def _moba_attn_t(q, kv, tq, pos0, bps):
    b, t_q, _ = q.shape
    t_c = kv.shape[1]
    n_pairs = MOBA_HEADS // 2
    n_blocks = t_c // MOBA_BLOCK
    assert MOBA_BLOCK % tq == 0 and pos0 % tq == 0 and tq % LANES == 0
    assert n_blocks % bps == 0
    return pl.pallas_call(
        functools.partial(_moba_t_kernel, tq=tq, pos0=pos0, n_blocks=n_blocks, bps=bps),
        grid=(b, n_pairs, t_q // tq),
        in_specs=[pl.BlockSpec((1, tq, 2 * LANES), lambda bi, pr, i: (bi, i, pr)),
                  pl.BlockSpec((1, t_c, LANES), lambda bi, pr, i: (bi, 0, pr)),
                  pl.BlockSpec((1, t_c, LANES), lambda bi, pr, i: (bi, 0, n_pairs + pr))],
        out_specs=pl.BlockSpec((1, tq, LANES), lambda bi, pr, i: (bi, i, pr)),
        out_shape=jax.ShapeDtypeStruct((b, t_q, MOBA_HEADS * HEAD_DIM), F32),
        scratch_shapes=[pltpu.VMEM((LANES, LANES), F32),
                        pltpu.VMEM((2, n_blocks, LANES, MOBA_BLOCK), BF16),
                        pltpu.VMEM((2, LANES, tq), F32)],
        compiler_params=_cparams(("parallel", "parallel", "arbitrary")),
        name="moba_attn_t",
    )(q, kv, kv)


def _moba_attn(q, kv, tq, pos0, bps):
    b, t_q, _ = q.shape
    t_c = kv.shape[1]
    n_pairs = MOBA_HEADS // 2
    assert MOBA_BLOCK % tq == 0 and pos0 % tq == 0 and (pos0 // MOBA_BLOCK) % bps == 0
    return pl.pallas_call(
        functools.partial(_moba_kernel, tq=tq, pos0=pos0, n_blocks=t_c // MOBA_BLOCK, bps=bps),
        grid=(b, n_pairs, t_q // tq),
        in_specs=[pl.BlockSpec((1, tq, 2 * LANES), lambda bi, pr, i: (bi, i, pr)),
                  pl.BlockSpec((1, t_c, LANES), lambda bi, pr, i: (bi, 0, pr)),
                  pl.BlockSpec((1, t_c, LANES), lambda bi, pr, i: (bi, 0, n_pairs + pr))],
        out_specs=pl.BlockSpec((1, tq, LANES), lambda bi, pr, i: (bi, i, pr)),
        out_shape=jax.ShapeDtypeStruct((b, t_q, MOBA_HEADS * HEAD_DIM), F32),
        scratch_shapes=[pltpu.VMEM((LANES, LANES), F32)],
        compiler_params=_cparams(("parallel", "parallel", "arbitrary")),
        name="moba_attn",
    )(q, kv, kv)


def _pad_heads(w, halves):
    d = w.shape[0]
    n_h = len(halves)
    onehot = jax.nn.one_hot(jnp.asarray(halves), 2, dtype=w.dtype)
    return jnp.einsum("dhe,hs->dhse", w.reshape(d, n_h, HEAD_DIM), onehot).reshape(d, n_h * LANES)


def _rope_tables(pos):
    half = HEAD_DIM // 2
    inv = 1.0 / (ROPE_THETA ** (jnp.arange(half, dtype=F32) / half))
    ang = pos.astype(F32)[:, None] * inv[None, :]
    cos = jnp.tile(jnp.cos(ang), (1, LANES // half))
    sin = jnp.tile(jnp.sin(ang), (1, LANES // half))
    upper = (jnp.arange(LANES) % HEAD_DIM) >= half
    return cos, jnp.where(upper, sin, 0.0), jnp.where(upper, 0.0, -sin)


def _overlap_matrix(n_ch):
    c = jnp.arange(n_ch)[:, None]
    n = jnp.arange(LANES)[None, :]
    return ((c >= 4 * n - 1) & (c <= 4 * n + 3)).astype(BF16)


def _expand_matrix(t_c, tk):
    blk = (jnp.arange(t_c) // NSA_SEL_BLOCK).reshape(t_c // tk, 1, tk)
    return (jnp.arange(LANES)[None, :, None] == blk).astype(BF16)


def _compress_weights(w1, w2):
    half = NSA_STRIDE * HEAD_DIM
    out = []
    for kv in range(2):
        wa = w1[kv, :half].reshape(NSA_STRIDE, HEAD_DIM, NSA_HID)
        wb = w1[kv, half:].reshape(NSA_STRIDE, HEAD_DIM, NSA_HID)
        ab = jnp.concatenate([wa, wb], axis=2)
        z = jnp.zeros_like(ab)
        g0 = jnp.concatenate([ab, z], axis=1)
        g1 = jnp.concatenate([z, ab], axis=1)
        out.append(jnp.concatenate([g0, g1], axis=2).reshape(NSA_STRIDE * LANES, 4 * NSA_HID))
    z2 = jnp.zeros_like(w2)
    w2p = jnp.concatenate([jnp.concatenate([w2, z2], axis=2),
                           jnp.concatenate([z2, w2], axis=2)], axis=1)
    return out[0].astype(BF16), out[1].astype(BF16), w2p.astype(BF16)


def _prep_weights(w_in_even, w_out_even, nsa_cmp_pe, nsa_cmp_w1, nsa_cmp_w2, w_ffn_gate,
                  w_ffn_up, w_ffn_down, w_in_odd, w_out_odd, w_router, w_exp_gate, w_exp_up,
                  w_exp_down, w_ple_proj, w_ple_gate):
    bf = lambda a: a.astype(BF16)
    we = w_in_even[0]
    qw = NSA_HEADS * HEAD_DIM
    kvw = 2 * LANES
    c0 = qw + 3 * kvw
    n_gate = 3 * NSA_HEADS
    s0 = c0 + n_gate
    sbw = SB_HEADS * HEAD_DIM
    even = [
        bf(_pad_heads(we[:, :qw], [h // NSA_GROUP for h in range(NSA_HEADS)])),
        bf(we[:, qw:qw + kvw]),
        bf(we[:, qw + kvw:qw + 2 * kvw]),
        bf(we[:, qw + 2 * kvw:c0]),
        bf(jnp.pad(we[:, c0:s0], ((0, 0), (0, LANES - n_gate)))),
        bf(_pad_heads(we[:, s0:s0 + sbw], [h % 2 for h in range(SB_HEADS)])),
        bf(we[:, s0 + sbw:]),
    ]
    wo = w_in_odd[0]
    mw = MOBA_HEADS * HEAD_DIM
    odd = [bf(_pad_heads(wo[:, :mw], [h % 2 for h in range(MOBA_HEADS)])), bf(wo[:, mw:])]
    w1k, w1v, w2p = _compress_weights(nsa_cmp_w1[0], nsa_cmp_w2[0])
    pe = bf(jnp.broadcast_to(nsa_cmp_pe[0].reshape(2, 1, -1), (2, 8, 2 * NSA_STRIDE * HEAD_DIM)))
    wr = jnp.pad(w_router[0], ((0, 0), (0, LANES - N_EXPERTS)))
    wr_hi, wr_lo = _split(wr)
    return dict(
        even=even, odd=odd, w1k=w1k, w1v=w1v, w2p=w2p, pe=pe, w1=bf(nsa_cmp_w1[0]),
        wo_a=bf(w_out_even[0][:qw]), wo_b=bf(w_out_even[0][qw:]),
        ffn=(bf(w_ffn_gate[0]), bf(w_ffn_up[0]), bf(w_ffn_down[0])),
        wo_c=bf(w_out_odd[0]), wr_hi=wr_hi, wr_lo=wr_lo,
        exp=(bf(w_exp_gate[0]), bf(w_exp_up[0]), bf(w_exp_down[0])),
        ple_proj=bf(w_ple_proj), ple_gate=bf(w_ple_gate))


EVEN_SEGS = ("rope_dual", "plain", ("rope", "none"), ("rope", "none"), ("sigmoid",), "plain",
             "plain")
ODD_SEGS = (("rope",) * 16, ("rope",) * 8 + ("none",) * 8)


def _trunk(x, p, pos0, past, W, norms, sizes):
    b, t, d = x.shape
    n = b * t
    tm, tq, tk_sel, tk_sb, tm_moe, tq_moba, moba_bps = sizes
    norm_mix, norm_ffn, norm_ple, norm_final = norms
    pos = pos0 + jnp.arange(t, dtype=jnp.int32)
    tabs = _rope_tables(pos)
    if t < tm:
        tabs = [jnp.tile(a, (tm // t, 1)) for a in tabs]
    h = x.reshape(n, d)
    row = lambda a: a.reshape(1, d)
    b3 = lambda a: a.reshape(b, t, a.shape[-1])

    qp, qr, cmp_r, sel_r, win_r, gates, sbq, sbkv = _proj(
        h, row(norm_mix[0]), tabs, W["even"], EVEN_SEGS, tm)
    if past is None:
        n_pages = t // PAGE
        ident = jnp.arange(b * n_pages, dtype=jnp.int32).reshape(b, n_pages)
        kc, vc = _compress(cmp_r.reshape(b * n_pages, PAGE, 2 * LANES), ident,
                           W["w1k"], W["w1v"], W["w2p"], W["pe"], W["w1"], 8)
        sel_ctx, win_ctx, sb_ctx = b3(sel_r), b3(win_r), b3(sbkv)
        wpos0 = 0
        win_state = win_ctx[:, -NSA_WINDOW:]
    else:
        pt = past["page_table"]
        tail = lambda a: jnp.pad(b3(a), ((0, 0), (0, MOBA_BLOCK - t), (0, 0)))
        pool = lambda c: c[0].reshape(c.shape[1], PAGE, -1)
        kc, vc = _compress(pool(past["cache_nsa_cmp"]), pt,
                           W["w1k"], W["w1v"], W["w2p"], W["pe"], W["w1"], 8)
        sel_ctx = _gather_ctx(past["cache_nsa_sel"], pt, tail(sel_r), 8)
        sb_ctx = _gather_ctx(past["cache_sb"], pt, tail(sbkv), 8)
        state = past["state_nsa_win"][0].reshape(b, NSA_WINDOW, 2 * LANES)
        win_all = jnp.concatenate([state, b3(win_r)], axis=1)
        win_ctx = jnp.pad(win_all, ((0, 0), (0, LANES - t), (0, 0)))
        wpos0 = pos0 - NSA_WINDOW
        win_state = win_all[:, -NSA_WINDOW:]
    t_c = sel_ctx.shape[1]
    o_a = _nsa_attn(b3(qp), b3(qr), b3(gates), kc, vc, sel_ctx, win_ctx,
                    _overlap_matrix(kc.shape[1]), _expand_matrix(t_c, tk_sel),
                    tq, tk_sel, pos0, wpos0)
    o_b = _sb_attn(b3(sbq), sb_ctx, jnp.tril(jnp.ones((tk_sb, tk_sb), BF16), -1),
                   tq, tk_sb, pos0)
    h = _outproj(h, o_a.reshape(n, -1), o_b.reshape(n, -1), W["wo_a"], W["wo_b"], tm)
    h = _ffn(h, row(norm_ffn[0]), *W["ffn"], tm, W["ffn"][0].shape[1] // 2)
    h = _ple(h, p[0].reshape(n, -1), row(norm_ple[0]), row(norm_final),
             W["ple_gate"][0], W["ple_proj"][0], tm, False)

    mq, mkv = _proj(h, row(norm_mix[1]), tabs, W["odd"], ODD_SEGS, tm)
    if past is None:
        moba_ctx = b3(mkv)
    else:
        moba_ctx = _gather_ctx(past["cache_moba"], pt, tail(mkv), 4)
    if tq_moba % LANES == 0:
        o_c = _moba_attn_t(b3(mq), moba_ctx, tq_moba, pos0, moba_bps)
    else:
        o_c = _moba_attn(b3(mq), moba_ctx, tq_moba, pos0, moba_bps)
    half = o_c.shape[-1] // 2
    o_c = o_c.reshape(n, -1)
    h = _outproj(h, o_c[:, :half], o_c[:, half:], W["wo_c"][:half], W["wo_c"][half:], tm)
    h = _moe(h, row(norm_ffn[1]), W["wr_hi"], W["wr_lo"], *W["exp"], tm_moe, 512)
    y = _ple(h, p[1].reshape(n, -1), row(norm_ple[1]), row(norm_final),
             W["ple_gate"][1], W["ple_proj"][1], tm, True)

    kv5 = lambda a, heads: a.reshape(1, b, -1, 2, heads, HEAD_DIM)
    return (y.reshape(b, t, d), kv5(cmp_r, 2), kv5(sel_r, 2), kv5(win_state, 2),
            kv5(sbkv, SB_HEADS), kv5(mkv, MOBA_HEADS))


def kernel(x_prompt, x_sample, cache_nsa_cmp, cache_nsa_sel, state_nsa_win, cache_sb, cache_moba,
           page_table, p_prompt, p_sample, norm_mix, norm_ffn, norm_ple, norm_final, w_in_even,
           w_out_even, nsa_cmp_pe, nsa_cmp_w1, nsa_cmp_w2, w_ffn_gate, w_ffn_up, w_ffn_down,
           w_in_odd, w_out_odd, w_router, w_exp_gate, w_exp_up, w_exp_down, w_ple_proj,
           w_ple_gate):
    W = _prep_weights(w_in_even, w_out_even, nsa_cmp_pe, nsa_cmp_w1, nsa_cmp_w2, w_ffn_gate,
                      w_ffn_up, w_ffn_down, w_in_odd, w_out_odd, w_router, w_exp_gate, w_exp_up,
                      w_exp_down, w_ple_proj, w_ple_gate)
    norms = (norm_mix, norm_ffn, norm_ple, norm_final)
    past = dict(cache_nsa_cmp=cache_nsa_cmp, cache_nsa_sel=cache_nsa_sel,
                state_nsa_win=state_nsa_win, cache_sb=cache_sb, cache_moba=cache_moba,
                page_table=page_table)
    past_len = page_table.shape[1] * cache_sb.shape[2]
    t_dec = x_sample.shape[1]
    n_dec = x_sample.shape[0] * t_dec
    y_p, cmp_p, sel_p, win_p, sb_p, moba_p = _trunk(
        x_prompt, p_prompt, 0, None, W, norms, (256, 128, 256, 128, 1024, MOBA_BLOCK, 4))
    y_s, cmp_s, sel_s, win_s, sb_s, moba_s = _trunk(
        x_sample, p_sample, past_len, past, W, norms, (n_dec, t_dec, 256, 128, n_dec, t_dec, 8))
    return (y_p, y_s, cmp_p, cmp_s, sel_p, sel_s, win_p, win_s, sb_p, sb_s, moba_p, moba_s)
```

```python
import functools

import jax
import jax.numpy as jnp
from jax import lax
from jax.experimental import pallas as pl
from jax.experimental.pallas import tpu as pltpu

F32 = jnp.float32
BF16 = jnp.bfloat16

LANES = 128
HEAD_DIM = 64
PAGE = 128
RMS_EPS = 1e-6
ROPE_THETA = 10000.0
NEG = -1e30

NSA_HEADS = 8
NSA_GROUP = 4
NSA_STRIDE = 16
NSA_HID = 128
NSA_SEL_BLOCK = 64
NSA_PICKS = 13
NSA_WINDOW = 512
SB_HEADS = 8
MOBA_HEADS = 16
MOBA_BLOCK = 256
MOBA_TOPK = 3
N_EXPERTS = 8
SB_DEAD = -110.0

VMEM_LIMIT = 56 * 1024 * 1024


def _cparams(sem):
    return pltpu.CompilerParams(dimension_semantics=sem, vmem_limit_bytes=VMEM_LIMIT)


def _dot(a, b):
    return jnp.dot(a, b, preferred_element_type=F32)


def _dot_nt(a, b):
    return lax.dot_general(a, b, (((1,), (1,)), ((), ())), preferred_element_type=F32)


def _split(x):
    hi = x.astype(BF16)
    lo = (x - hi.astype(F32)).astype(BF16)
    return hi, lo


def _rmsnorm(x, g):
    return x * lax.rsqrt(jnp.mean(x * x, axis=-1, keepdims=True) + RMS_EPS) * g


def _sigmoid(x):
    return 1.0 / (1.0 + jnp.exp(-x))


def _rope_tile(y, cos, sa, sb):
    return y * cos + pltpu.roll(y, 32, 1) * sa + pltpu.roll(y, 96, 1) * sb


def _proj_kernel(x_ref, g_ref, cos_ref, sa_ref, sb_ref, *refs, segs):
    n_seg = len(segs)
    w_refs = refs[:n_seg]
    o_refs = list(refs[n_seg:])
    nb = _rmsnorm(x_ref[...], g_ref[...]).astype(BF16)
    cos, sa, sb = cos_ref[...], sa_ref[...], sb_ref[...]
    for w_ref, kinds in zip(w_refs, segs):
        y = _dot(nb, w_ref[...])
        if kinds == "plain":
            o_refs.pop(0)[...] = y
            continue
        if kinds == "rope_dual":
            o_refs.pop(0)[...] = y
            kinds = ("rope",) * (y.shape[1] // LANES)
        o_ref = o_refs.pop(0)
        for t, kind in enumerate(kinds):
            yt = y[:, t * LANES:(t + 1) * LANES]
            if kind == "rope":
                yt = _rope_tile(yt, cos, sa, sb)
            elif kind == "sigmoid":
                yt = _sigmoid(yt)
            o_ref[:, t * LANES:(t + 1) * LANES] = yt


def _proj(x, g, tabs, weights, segs, tm):
    n, d = x.shape
    nblk = tabs[0].shape[0] // tm
    out_shape, out_specs = [], []
    for w, kinds in zip(weights, segs):
        for _ in range(2 if kinds == "rope_dual" else 1):
            out_shape.append(jax.ShapeDtypeStruct((n, w.shape[1]), F32))
            out_specs.append(pl.BlockSpec((tm, w.shape[1]), lambda i: (i, 0)))
    tab_spec = pl.BlockSpec((tm, LANES), lambda i: (i % nblk, 0))
    return pl.pallas_call(
        functools.partial(_proj_kernel, segs=tuple(segs)),
        grid=(n // tm,),
        in_specs=[pl.BlockSpec((tm, d), lambda i: (i, 0)),
                  pl.BlockSpec((1, d), lambda i: (0, 0)),
                  tab_spec, tab_spec, tab_spec]
                 + [pl.BlockSpec(w.shape, lambda i: (0, 0)) for w in weights],
        out_specs=out_specs,
        out_shape=out_shape,
        compiler_params=_cparams(("parallel",)),
        name="norm_proj",
    )(x, g, *tabs, *weights)


def _outproj_kernel(h_ref, a_ref, b_ref, wa_ref, wb_ref, o_ref):
    o_ref[...] = (h_ref[...] + _dot(a_ref[...].astype(BF16), wa_ref[...])
                  + _dot(b_ref[...].astype(BF16), wb_ref[...]))


def _outproj(h, a, b, wa, wb, tm):
    n, d = h.shape
    row = lambda w: pl.BlockSpec((tm, w), lambda i: (i, 0))
    full = lambda w: pl.BlockSpec(w.shape, lambda i: (0, 0))
    return pl.pallas_call(
        _outproj_kernel,
        grid=(n // tm,),
        in_specs=[row(d), row(a.shape[1]), row(b.shape[1]), full(wa), full(wb)],
        out_specs=row(d),
        out_shape=jax.ShapeDtypeStruct((n, d), F32),
        compiler_params=_cparams(("parallel",)),
        name="out_proj",
    )(h, a, b, wa, wb)


def _ffn_kernel(h_ref, g_ref, wg_ref, wu_ref, wd_ref, o_ref, n_scr, acc_scr):
    f = pl.program_id(1)

    @pl.when(f == 0)
    def _():
        n_scr[...] = _rmsnorm(h_ref[...], g_ref[...]).astype(BF16)
        acc_scr[...] = jnp.zeros_like(acc_scr)

    nb = n_scr[...]
    gate = _dot(nb, wg_ref[...])
    up = _dot(nb, wu_ref[...])
    hid = (gate * _sigmoid(gate) * up).astype(BF16)
    acc_scr[...] += _dot(hid, wd_ref[...])

    @pl.when(f == pl.num_programs(1) - 1)
    def _():
        o_ref[...] = h_ref[...] + acc_scr[...]


def _ffn(h, g, wg, wu, wd, tm, tf):
    n, d = h.shape
    dff = wg.shape[1]
    return pl.pallas_call(
        _ffn_kernel,
        grid=(n // tm, dff // tf),
        in_specs=[pl.BlockSpec((tm, d), lambda i, f: (i, 0)),
                  pl.BlockSpec((1, d), lambda i, f: (0, 0)),
                  pl.BlockSpec((d, tf), lambda i, f: (0, f)),
                  pl.BlockSpec((d, tf), lambda i, f: (0, f)),
                  pl.BlockSpec((tf, d), lambda i, f: (f, 0))],
        out_specs=pl.BlockSpec((tm, d), lambda i, f: (i, 0)),
        out_shape=jax.ShapeDtypeStruct((n, d), F32),
        scratch_shapes=[pltpu.VMEM((tm, d), BF16), pltpu.VMEM((tm, d), F32)],
        compiler_params=_cparams(("parallel", "arbitrary")),
        name="swiglu_ffn",
    )(h, g, wg, wu, wd)


def _ple_kernel(h_ref, p_ref, g_ref, gf_ref, wg_ref, wp_ref, o_ref, *, final):
    h = h_ref[...]
    gate = _sigmoid(_dot(_rmsnorm(h, g_ref[...]).astype(BF16), wg_ref[...]))
    out = h + gate * _dot(p_ref[...].astype(BF16), wp_ref[...])
    if final:
        out = _rmsnorm(out, gf_ref[...])
    o_ref[...] = out


def _ple(h, p, g, gf, wg, wp, tm, final):
    n, d = h.shape
    row = lambda w: pl.BlockSpec((tm, w), lambda i: (i, 0))
    full = lambda a: pl.BlockSpec(a.shape, lambda i: (0, 0))
    return pl.pallas_call(
        functools.partial(_ple_kernel, final=final),
        grid=(n // tm,),
        in_specs=[row(d), row(p.shape[1]), full(g), full(gf), full(wg), full(wp)],
        out_specs=row(d),
        out_shape=jax.ShapeDtypeStruct((n, d), F32),
        compiler_params=_cparams(("parallel",)),
        name="ple",
    )(h, p, g, gf, wg, wp)


def _moe_kernel(h_ref, g_ref, wr_hi_ref, wr_lo_ref, wg_ref, wu_ref, wd_ref, o_ref,
                n_scr, gw_scr, acc_scr):
    e = pl.program_id(1)
    f = pl.program_id(2)
    lane = lax.broadcasted_iota(jnp.int32, (1, LANES), 1)

    @pl.when((e == 0) & (f == 0))
    def _():
        n = _rmsnorm(h_ref[...], g_ref[...])
        n_hi, n_lo = _split(n)
        n_scr[...] = n_hi
        logits = (_dot(n_hi, wr_hi_ref[...]) + _dot(n_lo, wr_hi_ref[...])
                  + _dot(n_hi, wr_lo_ref[...]))
        logits = jnp.where(lane < N_EXPERTS, logits, NEG)
        lane_f = lane.astype(F32)
        v1 = jnp.max(logits, axis=1, keepdims=True)
        i1 = jnp.min(jnp.where(logits == v1, lane_f, float(LANES)), axis=1, keepdims=True)
        rest = jnp.where(lane_f == i1, NEG, logits)
        v2 = jnp.max(rest, axis=1, keepdims=True)
        i2 = jnp.min(jnp.where(rest == v2, lane_f, float(LANES)), axis=1, keepdims=True)
        e2 = jnp.exp(v2 - v1)
        g1 = 1.0 / (1.0 + e2)
        gw_scr[...] = jnp.where(lane_f == i1, g1, jnp.where(lane_f == i2, e2 * g1, 0.0))
        acc_scr[...] = jnp.zeros_like(acc_scr)

    nb = n_scr[...]
    gate = _dot(nb, wg_ref[0])
    up = _dot(nb, wu_ref[0])
    hid = (gate * _sigmoid(gate) * up).astype(BF16)
    col = jnp.sum(jnp.where(lane == e, gw_scr[...], 0.0), axis=1, keepdims=True)
    acc_scr[...] += col * _dot(hid, wd_ref[0])

    @pl.when((e == pl.num_programs(1) - 1) & (f == pl.num_programs(2) - 1))
    def _():
        o_ref[...] = h_ref[...] + acc_scr[...]


def _moe(h, g, wr_hi, wr_lo, wg, wu, wd, tm, tf):
    n, d = h.shape
    n_e, _, dff = wg.shape
    return pl.pallas_call(
        _moe_kernel,
        grid=(n // tm, n_e, dff // tf),
        in_specs=[pl.BlockSpec((tm, d), lambda i, e, f: (i, 0)),
                  pl.BlockSpec((1, d), lambda i, e, f: (0, 0)),
                  pl.BlockSpec(wr_hi.shape, lambda i, e, f: (0, 0)),
                  pl.BlockSpec(wr_lo.shape, lambda i, e, f: (0, 0)),
                  pl.BlockSpec((1, d, tf), lambda i, e, f: (e, 0, f)),
                  pl.BlockSpec((1, d, tf), lambda i, e, f: (e, 0, f)),
                  pl.BlockSpec((1, tf, d), lambda i, e, f: (e, f, 0))],
        out_specs=pl.BlockSpec((tm, d), lambda i, e, f: (i, 0)),
        out_shape=jax.ShapeDtypeStruct((n, d), F32),
        scratch_shapes=[pltpu.VMEM((tm, d), BF16), pltpu.VMEM((tm, LANES), F32),
                        pltpu.VMEM((tm, d), F32)],
        compiler_params=_cparams(("parallel", "arbitrary", "arbitrary")),
        name="moe_ffn",
    )(h, g, wr_hi, wr_lo, wg, wu, wd)


def _gather_kernel(pt_ref, *refs, n_pg):
    del pt_ref
    page_refs, tail_ref, o_ref = refs[:n_pg], refs[n_pg], refs[n_pg + 1]
    s = pl.program_id(1)
    last = pl.num_programs(1) - 1

    n_heads = page_refs[0].shape[3]
    half = n_heads * HEAD_DIM

    @pl.when(s < last)
    def _():
        for i, p_ref in enumerate(page_refs):
            for kv in range(2):
                for pair in range(n_heads // 2):
                    tile = p_ref[0, 0, kv, 2 * pair:2 * pair + 2].reshape(LANES, PAGE)
                    c0 = kv * half + pair * LANES
                    o_ref[0, i * PAGE:(i + 1) * PAGE, c0:c0 + LANES] = tile.T

    @pl.when(s == last)
    def _():
        o_ref[0, 0:tail_ref.shape[1], :] = tail_ref[0]


def _gather_ctx(cache, page_table, tail, n_pg):
    b, n_pages = page_table.shape
    n_heads = cache.shape[4]
    w = 2 * n_heads * HEAD_DIM
    n_tail = tail.shape[1]
    steps = n_pages // n_pg
    cache = cache.transpose(0, 1, 3, 4, 5, 2)

    def page_spec(i):
        return pl.BlockSpec(
            (1, 1, 2, n_heads, HEAD_DIM, PAGE),
            lambda bi, s, pt: (0, pt[bi, jnp.minimum(s, steps - 1) * n_pg + i], 0, 0, 0, 0))

    return pl.pallas_call(
        functools.partial(_gather_kernel, n_pg=n_pg),
        grid_spec=pltpu.PrefetchScalarGridSpec(
            num_scalar_prefetch=1,
            grid=(b, steps + 1),
            in_specs=[page_spec(i) for i in range(n_pg)]
                     + [pl.BlockSpec((1, n_tail, w), lambda bi, s, pt: (bi, 0, 0))],
            out_specs=pl.BlockSpec((1, n_pg * PAGE, w), lambda bi, s, pt: (bi, s, 0)),
        ),
        out_shape=jax.ShapeDtypeStruct((b, n_pages * PAGE + n_tail, w), F32),
        compiler_params=_cparams(("parallel", "arbitrary")),
        name="paged_gather",
    )(page_table, *([cache] * n_pg), tail)


def _compress_kernel(pt_ref, *refs, n_pg):
    del pt_ref
    k_refs, v_refs = refs[:n_pg], refs[n_pg:2 * n_pg]
    (w1k_ref, w1v_ref, w2_ref, pe_ref, w1_ref, kc_ref, vc_ref,
     ak_scr, av_scr) = refs[2 * n_pg:]
    s = pl.program_id(1)
    n_ch = ak_scr.shape[0]
    per_page = PAGE // NSA_STRIDE

    for p_refs, a_scr in ((k_refs, ak_scr), (v_refs, av_scr)):
        for i, p_ref in enumerate(p_refs):
            base = pl.multiple_of((s * n_pg + i) * per_page, per_page)
            for j in range(NSA_STRIDE):
                a_scr[pl.ds(base, per_page), j * LANES:(j + 1) * LANES] = (
                    p_ref[0, pl.ds(j, per_page, stride=NSA_STRIDE), :])

    @pl.when(s == pl.num_programs(1) - 1)
    def _():
        for kv, (a_scr, w1p_ref, o_ref) in enumerate(((ak_scr, w1k_ref, kc_ref),
                                                      (av_scr, w1v_ref, vc_ref))):
            r = _dot(a_scr[...].astype(BF16), w1p_ref[...])
            bias = _dot(pe_ref[kv], w1_ref[kv])[0:1, :]
            hid = []
            for g in range(2):
                a = r[:, (2 * g) * NSA_HID:(2 * g + 1) * NSA_HID]
                b_next = pltpu.roll(r[:, (2 * g + 1) * NSA_HID:(2 * g + 2) * NSA_HID],
                                    n_ch - 1, 0)
                pre = a + b_next + bias
                hid.append(pre * _sigmoid(pre))
            hid = jnp.concatenate(hid, axis=1).astype(BF16)
            o_ref[0] = _dot(hid, w2_ref[kv])


def _compress(pool, page_table, w1k, w1v, w2, pe, w1, n_pg):
    b, n_pages = page_table.shape
    n_ch = n_pages * (PAGE // NSA_STRIDE)
    steps = n_pages // n_pg
    full = lambda a: pl.BlockSpec(a.shape, lambda bi, s, pt: (0,) * a.ndim)

    def page_spec(i, kv):
        return pl.BlockSpec((1, PAGE, LANES), lambda bi, s, pt: (pt[bi, s * n_pg + i], 0, kv))

    out_spec = pl.BlockSpec((1, n_ch, LANES), lambda bi, s, pt: (bi, 0, 0))
    return pl.pallas_call(
        functools.partial(_compress_kernel, n_pg=n_pg),
        grid_spec=pltpu.PrefetchScalarGridSpec(
            num_scalar_prefetch=1,
            grid=(b, steps),
            in_specs=[page_spec(i, kv) for kv in range(2) for i in range(n_pg)]
                     + [full(w1k), full(w1v), full(w2), full(pe), full(w1)],
            out_specs=[out_spec, out_spec],
            scratch_shapes=[pltpu.VMEM((n_ch, NSA_STRIDE * LANES), F32),
                            pltpu.VMEM((n_ch, NSA_STRIDE * LANES), F32)],
        ),
        out_shape=[jax.ShapeDtypeStruct((b, n_ch, LANES), F32)] * 2,
        compiler_params=_cparams(("parallel", "arbitrary")),
        name="nsa_compress",
    )(page_table, *([pool] * (2 * n_pg)), w1k, w1v, w2, pe, w1)


def _flash_step(state, s, valid, v):
    m, l, acc = state
    s = jnp.where(valid, s, NEG)
    m_new = jnp.maximum(m, jnp.max(s, axis=1, keepdims=True))
    alpha = jnp.exp(m - m_new)
    p = jnp.where(valid, jnp.exp(s - m_new), 0.0)
    l = alpha * l + jnp.sum(p, axis=1, keepdims=True)
    acc = alpha * acc + _dot(p.astype(BF16), v)
    return m_new, l, acc


def _flash_init(rows):
    return (jnp.full((rows, 1), NEG, F32), jnp.zeros((rows, 1), F32),
            jnp.zeros((rows, LANES), F32))


def _flash_out(state):
    _, l, acc = state
    return jnp.where(l > 0.0, acc / jnp.where(l > 0.0, l, 1.0), 0.0)


def _stack(x, times):
    return jnp.concatenate([x] * times, axis=0)


N_WIN_TILES = NSA_WINDOW // LANES + 1


def _nsa_kernel(qp_ref, qr_ref, gt_ref, kc_ref, vc_ref, sel_ref, ov_ref, ex_ref, *refs,
                tq, tk, pos0, wpos0):
    win_refs, o_ref = refs[:N_WIN_TILES], refs[N_WIN_TILES]
    i = pl.program_id(1)
    q0 = pos0 + i * tq
    scale = HEAD_DIM ** -0.5
    r4 = NSA_GROUP
    qpos = q0 + lax.broadcasted_iota(jnp.int32, (tq, 1), 0)
    qpos4 = _stack(qpos, r4)
    qblk = qpos // NSA_SEL_BLOCK
    lane = lax.broadcasted_iota(jnp.int32, (1, LANES), 1)
    lane_f = lane.astype(F32)
    n_ch = kc_ref.shape[1]
    c_end = lax.broadcasted_iota(jnp.int32, (1, n_ch), 1) * NSA_STRIDE + (2 * NSA_STRIDE - 1)
    kc = kc_ref[0].astype(BF16)
    vc = vc_ref[0].astype(BF16)
    gates = gt_ref[0]
    n_sel_tiles = (q0 + tq - 1) // tk + 1
    win_start = (q0 // LANES) * LANES - NSA_WINDOW

    q_rot, o_cmp, picked = [], [], []
    for g in range(2):
        heads = range(g * r4, (g + 1) * r4)
        q_c = jnp.concatenate([qp_ref[0, :, h * LANES:(h + 1) * LANES] for h in heads], axis=0)
        q_r = jnp.concatenate([qr_ref[0, :, h * LANES:(h + 1) * LANES] for h in heads], axis=0)
        q_c = (q_c * scale).astype(BF16)
        q_rot.append((q_r * scale).astype(BF16))

        s = _dot_nt(q_c, kc)
        valid = c_end <= qpos4
        s = jnp.where(valid, s, NEG)
        e = jnp.where(valid, jnp.exp(s - jnp.max(s, axis=1, keepdims=True)), 0.0)
        l = jnp.sum(e, axis=1, keepdims=True)
        p = jnp.where(l > 0.0, e / jnp.where(l > 0.0, l, 1.0), 0.0)
        o_cmp.append(_dot(p.astype(BF16), vc))

        p_sum = p[0:tq]
        for r in range(1, r4):
            p_sum = p_sum + p[r * tq:(r + 1) * tq]
        p_hi, p_lo = _split(p_sum)
        imp = _dot(p_hi, ov_ref[...]) + _dot(p_lo, ov_ref[...])

        cand = (lane <= qblk) & (lane != 0) & (lane != qblk) & (lane != qblk - 1)
        work = jnp.where(cand, imp, -1.0)
        pick = jnp.zeros((tq, LANES), F32)
        for _ in range(NSA_PICKS):
            top = jnp.max(work, axis=1, keepdims=True)
            first = jnp.min(jnp.where(work == top, lane_f, float(LANES)), axis=1,
                            keepdims=True)
            hit = (lane_f == first) & (top >= 0.0)
            pick = jnp.where(hit, 1.0, pick)
            work = jnp.where(hit, -1.0, work)
        picked.append(pick.astype(BF16))

    def sel_body(t, states):
        k0 = pl.multiple_of(t * tk, tk)
        k = sel_ref[0, pl.ds(k0, tk), 0:LANES].astype(BF16)
        v = sel_ref[0, pl.ds(k0, tk), LANES:2 * LANES].astype(BF16)
        kpos = k0 + lax.broadcasted_iota(jnp.int32, (1, tk), 1)
        kblk = kpos // NSA_SEL_BLOCK
        forced = (kblk == 0) | (kblk == qblk) | (kblk == qblk - 1)
        out = ()
        for g in range(2):
            chosen = _dot(picked[g], ex_ref[t]) > 0.5
            keep = jnp.where((chosen | forced) & (kpos <= qpos), 1.0, 0.0)
            out += _flash_step(states[3 * g:3 * g + 3], _dot_nt(q_rot[g], k),
                               _stack(keep, r4) > 0.5, v)
        return out

    sel_states = lax.fori_loop(0, n_sel_tiles, sel_body, _flash_init(r4 * tq) * 2)
    o_sel = [_flash_out(sel_states[0:3]), _flash_out(sel_states[3:6])]

    win_states = [_flash_init(r4 * tq), _flash_init(r4 * tq)]
    for jt, w_ref in enumerate(win_refs):
        k = w_ref[0, :, 0:LANES].astype(BF16)
        v = w_ref[0, :, LANES:2 * LANES].astype(BF16)
        wpos = win_start + jt * LANES + lane
        keep = jnp.where((wpos >= wpos0) & (wpos <= qpos) & (qpos - wpos < NSA_WINDOW),
                         1.0, 0.0)
        keep4 = _stack(keep, r4) > 0.5
        for g in range(2):
            win_states[g] = _flash_step(win_states[g], _dot_nt(q_rot[g], k), keep4, v)
    o_win = [_flash_out(win_states[0]), _flash_out(win_states[1])]

    outs = []
    for g in range(2):
        for r in range(r4):
            h = g * r4 + r
            rows = slice(r * tq, (r + 1) * tq)
            o = (gates[:, 3 * h:3 * h + 1] * o_cmp[g][rows]
                 + gates[:, 3 * h + 1:3 * h + 2] * o_sel[g][rows]
                 + gates[:, 3 * h + 2:3 * h + 3] * o_win[g][rows])
            if h % 2 != g:
                o = pltpu.roll(o, HEAD_DIM, 1)
            outs.append(o)

    for pair in range(NSA_HEADS // 2):
        o_ref[0, :, pair * LANES:(pair + 1) * LANES] = jnp.where(
            lane < HEAD_DIM, outs[2 * pair], outs[2 * pair + 1])


def _nsa_attn(qp, qr, gates, kc, vc, sel_ctx, win_ctx, ov, ex, tq, tk, pos0, wpos0):
    b, t_q, _ = qp.shape
    t_c = sel_ctx.shape[1]
    n_win_tiles_total = win_ctx.shape[1] // LANES
    tile0 = pos0 // LANES - NSA_WINDOW // LANES - wpos0 // LANES

    def win_spec(jt):
        def idx(bi, i):
            t = tile0 + (i * tq) // LANES + jt
            return (bi, jnp.clip(t, 0, n_win_tiles_total - 1), 0)
        return pl.BlockSpec((1, LANES, 2 * LANES), idx)

    qspec = pl.BlockSpec((1, tq, NSA_HEADS * LANES), lambda bi, i: (bi, i, 0))
    return pl.pallas_call(
        functools.partial(_nsa_kernel, tq=tq, tk=tk, pos0=pos0, wpos0=wpos0),
        grid=(b, t_q // tq),
        in_specs=[qspec, qspec,
                  pl.BlockSpec((1, tq, LANES), lambda bi, i: (bi, i, 0)),
                  pl.BlockSpec((1,) + kc.shape[1:], lambda bi, i: (bi, 0, 0)),
                  pl.BlockSpec((1,) + vc.shape[1:], lambda bi, i: (bi, 0, 0)),
                  pl.BlockSpec((1, t_c, 2 * LANES), lambda bi, i: (bi, 0, 0)),
                  pl.BlockSpec(ov.shape, lambda bi, i: (0, 0)),
                  pl.BlockSpec(ex.shape, lambda bi, i: (0, 0, 0))]
                 + [win_spec(jt) for jt in range(N_WIN_TILES)],
        out_specs=pl.BlockSpec((1, tq, NSA_HEADS * HEAD_DIM), lambda bi, i: (bi, i, 0)),
        out_shape=jax.ShapeDtypeStruct((b, t_q, NSA_HEADS * HEAD_DIM), F32),
        compiler_params=_cparams(("parallel", "arbitrary")),
        name="nsa_attn",
    )(qp, qr, gates, kc, vc, sel_ctx, ov, ex, *([win_ctx] * N_WIN_TILES))


def _sb_kernel(q_ref, k_ref, v_ref, tri_ref, o_ref, *, tq, tk, pos0):
    i = pl.program_id(2)
    q0 = pos0 + i * tq
    scale = HEAD_DIM ** -0.5
    qpos = q0 + lax.broadcasted_iota(jnp.int32, (tq, 1), 0)
    lane = lax.broadcasted_iota(jnp.int32, (1, LANES), 1)
    n_tiles = (q0 + tq - 2) // tk + 1
    tri = tri_ref[...]
    qs = [(q_ref[0, :, hh * LANES:(hh + 1) * LANES] * scale).astype(BF16) for hh in range(2)]

    def tile(q, k, v, valid, acc, run):
        z = _dot_nt(q, k)
        soft = jnp.log(1.0 + jnp.exp(-jnp.abs(z)))
        log_break = jnp.minimum(z, 0.0) - soft
        log_stay = jnp.where(valid, log_break - z, 0.0)
        s_hi, s_lo = _split(log_stay)
        after = _dot(s_hi, tri) + _dot(s_lo, tri)
        a = jnp.where(valid, jnp.exp(log_break + after + run), 0.0)
        return acc + _dot(a.astype(BF16), v), run + jnp.sum(log_stay, axis=1, keepdims=True)

    def cond(carry):
        return (carry[0] < n_tiles) & (carry[1] > 0)

    def body(carry):
        step, _, acc0, run0, acc1, run1 = carry
        k0 = pl.multiple_of((n_tiles - 1 - step) * tk, tk)
        k = k_ref[0, pl.ds(k0, tk), :].astype(BF16)
        v = v_ref[0, pl.ds(k0, tk), :].astype(BF16)
        valid = (k0 + lax.broadcasted_iota(jnp.int32, (1, tk), 1)) < qpos
        acc0, run0 = tile(qs[0], k, v, valid, acc0, run0)
        acc1, run1 = tile(qs[1], k, v, valid, acc1, run1)
        live = jnp.maximum(jnp.max(run0), jnp.max(run1)) > SB_DEAD
        return step + 1, live.astype(jnp.int32), acc0, run0, acc1, run1

    zero_acc = jnp.zeros((tq, LANES), F32)
    zero_run = jnp.zeros((tq, 1), F32)
    _, _, acc0, _, acc1, _ = lax.while_loop(
        cond, body, (jnp.int32(0), jnp.int32(1), zero_acc, zero_run, zero_acc, zero_run))
    o_ref[0] = jnp.where(lane < HEAD_DIM, acc0, acc1)


def _sb_attn(q, kv, tri, tq, tk, pos0):
    b, t_q, _ = q.shape
    t_c = kv.shape[1]
    n_pairs = SB_HEADS // 2
    return pl.pallas_call(
        functools.partial(_sb_kernel, tq=tq, tk=tk, pos0=pos0),
        grid=(b, n_pairs, t_q // tq),
        in_specs=[pl.BlockSpec((1, tq, 2 * LANES), lambda bi, pr, i: (bi, i, pr)),
                  pl.BlockSpec((1, t_c, LANES), lambda bi, pr, i: (bi, 0, pr)),
                  pl.BlockSpec((1, t_c, LANES), lambda bi, pr, i: (bi, 0, n_pairs + pr)),
                  pl.BlockSpec(tri.shape, lambda bi, pr, i: (0, 0))],
        out_specs=pl.BlockSpec((1, tq, LANES), lambda bi, pr, i: (bi, i, pr)),
        out_shape=jax.ShapeDtypeStruct((b, t_q, SB_HEADS * HEAD_DIM), F32),
        compiler_params=_cparams(("parallel", "parallel", "arbitrary")),
        name="sb_attn",
    )(q, kv, kv, tri)


def _moba_kernel(q_ref, k_ref, v_ref, o_ref, km_scr, *, tq, pos0, n_blocks, bps):
    i = pl.program_id(2)
    blk = MOBA_BLOCK
    q0 = pos0 + i * tq
    own = q0 // blk
    scale = HEAD_DIM ** -0.5
    qpos = q0 + lax.broadcasted_iota(jnp.int32, (tq, 1), 0)
    lane = lax.broadcasted_iota(jnp.int32, (1, LANES), 1)
    lane_f = lane.astype(F32)

    @pl.when(i == 0)
    def _():
        km_scr[...] = jnp.zeros_like(km_scr)
        for n in range(n_blocks):
            km_scr[n:n + 1, :] = jnp.sum(k_ref[0, n * blk:(n + 1) * blk, :], axis=0,
                                         keepdims=True) * (1.0 / blk)

    km_hi, km_lo = _split(km_scr[...])
    q_s, picked, states = [], [], []
    own0 = pl.multiple_of(own * blk, blk)
    k_own = k_ref[0, pl.ds(own0, blk), :].astype(BF16)
    v_own = v_ref[0, pl.ds(own0, blk), :].astype(BF16)
    causal = (own0 + lax.broadcasted_iota(jnp.int32, (1, blk), 1)) <= qpos
    for hh in range(2):
        q = q_ref[0, :, hh * LANES:(hh + 1) * LANES]
        q_hi, q_lo = _split(q)
        gate = _dot_nt(q_hi, km_hi) + _dot_nt(q_lo, km_hi) + _dot_nt(q_hi, km_lo)
        work = jnp.where(lane < own, gate, NEG)
        pick = jnp.zeros((tq, LANES), F32)
        for _ in range(MOBA_TOPK):
            top = jnp.max(work, axis=1, keepdims=True)
            first = jnp.min(jnp.where(work == top, lane_f, float(LANES)), axis=1,
                            keepdims=True)
            hit = (lane_f == first) & (top > 0.5 * NEG)
            pick = jnp.where(hit, 1.0, pick)
            work = jnp.where(hit, NEG, work)
        picked.append(pick)
        q_s.append((q * scale).astype(BF16))
        states.append(_flash_step(_flash_init(tq), _dot_nt(q_s[hh], k_own), causal, v_own))

    def body(t, carry):
        k0 = pl.multiple_of(t * (bps * blk), bps * blk)
        k = k_ref[0, pl.ds(k0, bps * blk), :].astype(BF16)
        v = v_ref[0, pl.ds(k0, bps * blk), :].astype(BF16)
        out = []
        for hh in range(2):
            m, l, acc = carry[3 * hh:3 * hh + 3]
            s = _dot_nt(q_s[hh], k)
            rows = [jnp.sum(jnp.where(lane == t * bps + j, picked[hh], 0.0), axis=1,
                            keepdims=True) > 0.5 for j in range(bps)]
            parts = [s[:, j * blk:(j + 1) * blk] for j in range(bps)]
            m_new = m
            for j in range(bps):
                m_new = jnp.maximum(m_new, jnp.where(
                    rows[j], jnp.max(parts[j], axis=1, keepdims=True), NEG))
            alpha = jnp.exp(m - m_new)
            p = [jnp.exp(parts[j] - jnp.where(rows[j], m_new, -NEG)) for j in range(bps)]
            p = p[0] if bps == 1 else jnp.concatenate(p, axis=1)
            l = alpha * l + jnp.sum(p, axis=1, keepdims=True)
            acc = alpha * acc + _dot(p.astype(BF16), v)
            out += [m_new, l, acc]
        return tuple(out)

    n_steps = (own + bps - 1) // bps
    final = lax.fori_loop(0, n_steps, body, tuple(states[0]) + tuple(states[1]))
    o_ref[0] = jnp.where(lane < HEAD_DIM, _flash_out(final[0:3]), _flash_out(final[3:6]))


def _moba_t_kernel(q_ref, k_ref, v_ref, o_ref, km_scr, vt_scr, pk_scr, *, tq, pos0, n_blocks,
                   bps):
    i = pl.program_id(2)
    blk = MOBA_BLOCK
    sub = blk // LANES
    q0 = pos0 + i * tq
    own = q0 // blk
    scale = HEAD_DIM ** -0.5
    row = lax.broadcasted_iota(jnp.int32, (LANES, 1), 0)
    row_f = row.astype(F32)
    qpos = q0 + lax.broadcasted_iota(jnp.int32, (1, tq), 1)

    @pl.when(i == 0)
    def _():
        km_scr[...] = jnp.zeros_like(km_scr)
        for n in range(n_blocks):
            km_scr[n:n + 1, :] = jnp.sum(k_ref[0, n * blk:(n + 1) * blk, :], axis=0,
                                         keepdims=True) * (1.0 / blk)

        def transpose_block(n, carry):
            n0 = pl.multiple_of(n * blk, blk)
            vt = jnp.concatenate(
                [v_ref[0, pl.ds(n0 + j * LANES, LANES), :].T for j in range(sub)], axis=1)
            vt_scr[0, n] = jnp.where(row < HEAD_DIM, vt, 1.0).astype(BF16)
            vt_scr[1, n] = jnp.where(row >= HEAD_DIM, vt, 1.0).astype(BF16)
            return carry

        lax.fori_loop(0, n_blocks, transpose_block, 0)

    def weighted_values(hh, t, p):
        return _dot(vt_scr[hh, t], p)

    km_hi, km_lo = _split(km_scr[...])
    own0 = pl.multiple_of(own * blk, blk)
    k_own = k_ref[0, pl.ds(own0, blk), :].astype(BF16)
    causal = (own0 + lax.broadcasted_iota(jnp.int32, (blk, 1), 0)) <= qpos
    q_s, states = [], []
    for hh in range(2):
        q = q_ref[0, :, hh * LANES:(hh + 1) * LANES]
        q_hi, q_lo = _split(q)
        gate = _dot_nt(km_hi, q_hi) + _dot_nt(km_hi, q_lo) + _dot_nt(km_lo, q_hi)
        work = jnp.where(row < own, gate, NEG)
        pick = jnp.zeros((LANES, tq), F32)
        for _ in range(MOBA_TOPK):
            top = jnp.max(work, axis=0, keepdims=True)
            first = jnp.min(jnp.where(work == top, row_f, float(LANES)), axis=0, keepdims=True)
            hit = (row_f == first) & (top > 0.5 * NEG)
            pick = jnp.where(hit, 1.0, pick)
            work = jnp.where(hit, NEG, work)
        pk_scr[hh] = pick
        q_s.append((q * scale).astype(BF16))
        s = jnp.where(causal, _dot_nt(k_own, q_s[hh]), NEG)
        m = jnp.max(s, axis=0, keepdims=True)
        p = jnp.where(causal, jnp.exp(s - m), 0.0).astype(BF16)
        states += [m, weighted_values(hh, own, p)]

    def body(t, carry):
        k0 = pl.multiple_of(t * (bps * blk), bps * blk)
        k = k_ref[0, pl.ds(k0, bps * blk), :].astype(BF16)
        out = []
        for hh in range(2):
            m, acc = carry[2 * hh:2 * hh + 2]
            s = _dot_nt(k, q_s[hh])
            parts = [s[j * blk:(j + 1) * blk] for j in range(bps)]
            chosen = [pk_scr[hh, pl.ds(t * bps + j, 1), :] > 0.5 for j in range(bps)]
            m_new = m
            for j in range(bps):
                m_new = jnp.maximum(m_new, jnp.where(
                    chosen[j], jnp.max(parts[j], axis=0, keepdims=True), NEG))
            acc = jnp.exp(m - m_new) * acc
            for j in range(bps):
                p = jnp.exp(parts[j] - jnp.where(chosen[j], m_new, -NEG)).astype(BF16)
                acc = acc + weighted_values(hh, t * bps + j, p)
            out += [m_new, acc]
        return tuple(out)

    _, acc0, _, acc1 = lax.fori_loop(0, (own + bps - 1) // bps, body, tuple(states))
    o_t = jnp.where(row < HEAD_DIM, acc0 / acc0[HEAD_DIM:HEAD_DIM + 1, :], acc1 / acc1[0:1, :])
    o_ref[0] = o_t.T


def _moba_attn_t(q, kv, tq, pos0, bps):
    b, t_q, _ = q.shape
    t_c = kv.shape[1]
    n_pairs = MOBA_HEADS // 2
    n_blocks = t_c // MOBA_BLOCK
    assert MOBA_BLOCK % tq == 0 and pos0 % tq == 0 and tq % LANES == 0
    assert n_blocks % bps == 0
    return pl.pallas_call(
        functools.partial(_moba_t_kernel, tq=tq, pos0=pos0, n_blocks=n_blocks, bps=bps),
        grid=(b, n_pairs, t_q // tq),
        in_specs=[pl.BlockSpec((1, tq, 2 * LANES), lambda bi, pr, i: (bi, i, pr)),
                  pl.BlockSpec((1, t_c, LANES), lambda bi, pr, i: (bi, 0, pr)),
                  pl.BlockSpec((1, t_c, LANES), lambda bi, pr, i: (bi, 0, n_pairs + pr))],
        out_specs=pl.BlockSpec((1, tq, LANES), lambda bi, pr, i: (bi, i, pr)),
        out_shape=jax.ShapeDtypeStruct((b, t_q, MOBA_HEADS * HEAD_DIM), F32),
        scratch_shapes=[pltpu.VMEM((LANES, LANES), F32),
                        pltpu.VMEM((2, n_blocks, LANES, MOBA_BLOCK), BF16),
                        pltpu.VMEM((2, LANES, tq), F32)],
        compiler_params=_cparams(("parallel", "parallel", "arbitrary")),
        name="moba_attn_t",
    )(q, kv, kv)


def _moba_attn(q, kv, tq, pos0, bps):
    b, t_q, _ = q.shape
    t_c = kv.shape[1]
    n_pairs = MOBA_HEADS // 2
    assert MOBA_BLOCK % tq == 0 and pos0 % tq == 0 and (pos0 // MOBA_BLOCK) % bps == 0
    return pl.pallas_call(
        functools.partial(_moba_kernel, tq=tq, pos0=pos0, n_blocks=t_c // MOBA_BLOCK, bps=bps),
        grid=(b, n_pairs, t_q // tq),
        in_specs=[pl.BlockSpec((1, tq, 2 * LANES), lambda bi, pr, i: (bi, i, pr)),
                  pl.BlockSpec((1, t_c, LANES), lambda bi, pr, i: (bi, 0, pr)),
                  pl.BlockSpec((1, t_c, LANES), lambda bi, pr, i: (bi, 0, n_pairs + pr))],
        out_specs=pl.BlockSpec((1, tq, LANES), lambda bi, pr, i: (bi, i, pr)),
        out_shape=jax.ShapeDtypeStruct((b, t_q, MOBA_HEADS * HEAD_DIM), F32),
        scratch_shapes=[pltpu.VMEM((LANES, LANES), F32)],
        compiler_params=_cparams(("parallel", "parallel", "arbitrary")),
        name="moba_attn",
    )(q, kv, kv)


def _pad_heads(w, halves):
    d = w.shape[0]
    n_h = len(halves)
    onehot = jax.nn.one_hot(jnp.asarray(halves), 2, dtype=w.dtype)
    return jnp.einsum("dhe,hs->dhse", w.reshape(d, n_h, HEAD_DIM), onehot).reshape(d, n_h * LANES)


def _rope_tables(pos):
    half = HEAD_DIM // 2
    inv = 1.0 / (ROPE_THETA ** (jnp.arange(half, dtype=F32) / half))
    ang = pos.astype(F32)[:, None] * inv[None, :]
    cos = jnp.tile(jnp.cos(ang), (1, LANES // half))
    sin = jnp.tile(jnp.sin(ang), (1, LANES // half))
    upper = (jnp.arange(LANES) % HEAD_DIM) >= half
    return cos, jnp.where(upper, sin, 0.0), jnp.where(upper, 0.0, -sin)


def _overlap_matrix(n_ch):
    c = jnp.arange(n_ch)[:, None]
    n = jnp.arange(LANES)[None, :]
    return ((c >= 4 * n - 1) & (c <= 4 * n + 3)).astype(BF16)


def _expand_matrix(t_c, tk):
    blk = (jnp.arange(t_c) // NSA_SEL_BLOCK).reshape(t_c // tk, 1, tk)
    return (jnp.arange(LANES)[None, :, None] == blk).astype(BF16)


def _compress_weights(w1, w2):
    half = NSA_STRIDE * HEAD_DIM
    out = []
    for kv in range(2):
        wa = w1[kv, :half].reshape(NSA_STRIDE, HEAD_DIM, NSA_HID)
        wb = w1[kv, half:].reshape(NSA_STRIDE, HEAD_DIM, NSA_HID)
        ab = jnp.concatenate([wa, wb], axis=2)
        z = jnp.zeros_like(ab)
        g0 = jnp.concatenate([ab, z], axis=1)
        g1 = jnp.concatenate([z, ab], axis=1)
        out.append(jnp.concatenate([g0, g1], axis=2).reshape(NSA_STRIDE * LANES, 4 * NSA_HID))
    z2 = jnp.zeros_like(w2)
    w2p = jnp.concatenate([jnp.concatenate([w2, z2], axis=2),
                           jnp.concatenate([z2, w2], axis=2)], axis=1)
    return out[0].astype(BF16), out[1].astype(BF16), w2p.astype(BF16)


def _prep_weights(w_in_even, w_out_even, nsa_cmp_pe, nsa_cmp_w1, nsa_cmp_w2, w_ffn_gate,
                  w_ffn_up, w_ffn_down, w_in_odd, w_out_odd, w_router, w_exp_gate, w_exp_up,
                  w_exp_down, w_ple_proj, w_ple_gate):
    bf = lambda a: a.astype(BF16)
    we = w_in_even[0]
    qw = NSA_HEADS * HEAD_DIM
    kvw = 2 * LANES
    c0 = qw + 3 * kvw
    n_gate = 3 * NSA_HEADS
    s0 = c0 + n_gate
    sbw = SB_HEADS * HEAD_DIM
    even = [
        bf(_pad_heads(we[:, :qw], [h // NSA_GROUP for h in range(NSA_HEADS)])),
        bf(we[:, qw:qw + kvw]),
        bf(we[:, qw + kvw:qw + 2 * kvw]),
        bf(we[:, qw + 2 * kvw:c0]),
        bf(jnp.pad(we[:, c0:s0], ((0, 0), (0, LANES - n_gate)))),
        bf(_pad_heads(we[:, s0:s0 + sbw], [h % 2 for h in range(SB_HEADS)])),
        bf(we[:, s0 + sbw:]),
    ]
    wo = w_in_odd[0]
    mw = MOBA_HEADS * HEAD_DIM
    odd = [bf(_pad_heads(wo[:, :mw], [h % 2 for h in range(MOBA_HEADS)])), bf(wo[:, mw:])]
    w1k, w1v, w2p = _compress_weights(nsa_cmp_w1[0], nsa_cmp_w2[0])
    pe = bf(jnp.broadcast_to(nsa_cmp_pe[0].reshape(2, 1, -1), (2, 8, 2 * NSA_STRIDE * HEAD_DIM)))
    wr = jnp.pad(w_router[0], ((0, 0), (0, LANES - N_EXPERTS)))
    wr_hi, wr_lo = _split(wr)
    return dict(
        even=even, odd=odd, w1k=w1k, w1v=w1v, w2p=w2p, pe=pe, w1=bf(nsa_cmp_w1[0]),
        wo_a=bf(w_out_even[0][:qw]), wo_b=bf(w_out_even[0][qw:]),
        ffn=(bf(w_ffn_gate[0]), bf(w_ffn_up[0]), bf(w_ffn_down[0])),
        wo_c=bf(w_out_odd[0]), wr_hi=wr_hi, wr_lo=wr_lo,
        exp=(bf(w_exp_gate[0]), bf(w_exp_up[0]), bf(w_exp_down[0])),
        ple_proj=bf(w_ple_proj), ple_gate=bf(w_ple_gate))


EVEN_SEGS = ("rope_dual", "plain", ("rope", "none"), ("rope", "none"), ("sigmoid",), "plain",
             "plain")
ODD_SEGS = (("rope",) * 16, ("rope",) * 8 + ("none",) * 8)


def _trunk(x, p, pos0, past, W, norms, sizes):
    b, t, d = x.shape
    n = b * t
    tm, tq, tk_sel, tk_sb, tm_moe, tq_moba, moba_bps = sizes
    norm_mix, norm_ffn, norm_ple, norm_final = norms
    pos = pos0 + jnp.arange(t, dtype=jnp.int32)
    tabs = _rope_tables(pos)
    if t < tm:
        tabs = [jnp.tile(a, (tm // t, 1)) for a in tabs]
    h = x.reshape(n, d)
    row = lambda a: a.reshape(1, d)
    b3 = lambda a: a.reshape(b, t, a.shape[-1])

    qp, qr, cmp_r, sel_r, win_r, gates, sbq, sbkv = _proj(
        h, row(norm_mix[0]), tabs, W["even"], EVEN_SEGS, tm)
    if past is None:
        n_pages = t // PAGE
        ident = jnp.arange(b * n_pages, dtype=jnp.int32).reshape(b, n_pages)
        kc, vc = _compress(cmp_r.reshape(b * n_pages, PAGE, 2 * LANES), ident,
                           W["w1k"], W["w1v"], W["w2p"], W["pe"], W["w1"], 8)
        sel_ctx, win_ctx, sb_ctx = b3(sel_r), b3(win_r), b3(sbkv)
        wpos0 = 0
        win_state = win_ctx[:, -NSA_WINDOW:]
    else:
        pt = past["page_table"]
        tail = lambda a: jnp.pad(b3(a), ((0, 0), (0, MOBA_BLOCK - t), (0, 0)))
        pool = lambda c: c[0].reshape(c.shape[1], PAGE, -1)
        kc, vc = _compress(pool(past["cache_nsa_cmp"]), pt,
                           W["w1k"], W["w1v"], W["w2p"], W["pe"], W["w1"], 8)
        sel_ctx = _gather_ctx(past["cache_nsa_sel"], pt, tail(sel_r), 8)
        sb_ctx = _gather_ctx(past["cache_sb"], pt, tail(sbkv), 8)
        state = past["state_nsa_win"][0].reshape(b, NSA_WINDOW, 2 * LANES)
        win_all = jnp.concatenate([state, b3(win_r)], axis=1)
        win_ctx = jnp.pad(win_all, ((0, 0), (0, LANES - t), (0, 0)))
        wpos0 = pos0 - NSA_WINDOW
        win_state = win_all[:, -NSA_WINDOW:]
    t_c = sel_ctx.shape[1]
    o_a = _nsa_attn(b3(qp), b3(qr), b3(gates), kc, vc, sel_ctx, win_ctx,
                    _overlap_matrix(kc.shape[1]), _expand_matrix(t_c, tk_sel),
                    tq, tk_sel, pos0, wpos0)
    o_b = _sb_attn(b3(sbq), sb_ctx, jnp.tril(jnp.ones((tk_sb, tk_sb), BF16), -1),
                   tq, tk_sb, pos0)
    h = _outproj(h, o_a.reshape(n, -1), o_b.reshape(n, -1), W["wo_a"], W["wo_b"], tm)
    h = _ffn(h, row(norm_ffn[0]), *W["ffn"], tm, W["ffn"][0].shape[1] // 2)
    h = _ple(h, p[0].reshape(n, -1), row(norm_ple[0]), row(norm_final),
             W["ple_gate"][0], W["ple_proj"][0], tm, False)

    mq, mkv = _proj(h, row(norm_mix[1]), tabs, W["odd"], ODD_SEGS, tm)
    if past is None:
        moba_ctx = b3(mkv)
    else:
        moba_ctx = _gather_ctx(past["cache_moba"], pt, tail(mkv), 4)
    if tq_moba % LANES == 0:
        o_c = _moba_attn_t(b3(mq), moba_ctx, tq_moba, pos0, moba_bps)
    else:
        o_c = _moba_attn(b3(mq), moba_ctx, tq_moba, pos0, moba_bps)
    half = o_c.shape[-1] // 2
    o_c = o_c.reshape(n, -1)
    h = _outproj(h, o_c[:, :half], o_c[:, half:], W["wo_c"][:half], W["wo_c"][half:], tm)
    h = _moe(h, row(norm_ffn[1]), W["wr_hi"], W["wr_lo"], *W["exp"], tm_moe, 512)
    y = _ple(h, p[1].reshape(n, -1), row(norm_ple[1]), row(norm_final),
             W["ple_gate"][1], W["ple_proj"][1], tm, True)

    kv5 = lambda a, heads: a.reshape(1, b, -1, 2, heads, HEAD_DIM)
    return (y.reshape(b, t, d), kv5(cmp_r, 2), kv5(sel_r, 2), kv5(win_state, 2),
            kv5(sbkv, SB_HEADS), kv5(mkv, MOBA_HEADS))


def kernel(x_prompt, x_sample, cache_nsa_cmp, cache_nsa_sel, state_nsa_win, cache_sb, cache_moba,
           page_table, p_prompt, p_sample, norm_mix, norm_ffn, norm_ple, norm_final, w_in_even,
           w_out_even, nsa_cmp_pe, nsa_cmp_w1, nsa_cmp_w2, w_ffn_gate, w_ffn_up, w_ffn_down,
           w_in_odd, w_out_odd, w_router, w_exp_gate, w_exp_up, w_exp_down, w_ple_proj,
           w_ple_gate):
    W = _prep_weights(w_in_even, w_out_even, nsa_cmp_pe, nsa_cmp_w1, nsa_cmp_w2, w_ffn_gate,
                      w_ffn_up, w_ffn_down, w_in_odd, w_out_odd, w_router, w_exp_gate, w_exp_up,
                      w_exp_down, w_ple_proj, w_ple_gate)
    norms = (norm_mix, norm_ffn, norm_ple, norm_final)
    past = dict(cache_nsa_cmp=cache_nsa_cmp, cache_nsa_sel=cache_nsa_sel,
                state_nsa_win=state_nsa_win, cache_sb=cache_sb, cache_moba=cache_moba,
                page_table=page_table)
    past_len = page_table.shape[1] * cache_sb.shape[2]
    t_dec = x_sample.shape[1]
    n_dec = x_sample.shape[0] * t_dec
    y_p, cmp_p, sel_p, win_p, sb_p, moba_p = _trunk(
        x_prompt, p_prompt, 0, None, W, norms, (256, 128, 256, 128, 1024, MOBA_BLOCK, 4))
    y_s, cmp_s, sel_s, win_s, sb_s, moba_s = _trunk(
        x_sample, p_sample, past_len, past, W, norms, (n_dec, t_dec, 256, 128, n_dec, t_dec, 8))
    return (y_p, y_s, cmp_p, cmp_s, sel_p, sel_s, win_p, win_s, sb_p, sb_s, moba_p, moba_s)
```

```python
import functools

import jax
import jax.numpy as jnp
from jax import lax
from jax.experimental import pallas as pl
from jax.experimental.pallas import tpu as pltpu

F32 = jnp.float32
BF16 = jnp.bfloat16

LANES = 128
HEAD_DIM = 64
PAGE = 128
RMS_EPS = 1e-6
ROPE_THETA = 10000.0
NEG = -1e30

NSA_HEADS = 8
NSA_GROUP = 4
NSA_STRIDE = 16
NSA_HID = 128
NSA_SEL_BLOCK = 64
NSA_PICKS = 13
NSA_WINDOW = 512
SB_HEADS = 8
MOBA_HEADS = 16
MOBA_BLOCK = 256
MOBA_TOPK = 3
N_EXPERTS = 8
SB_DEAD = -110.0

VMEM_LIMIT = 56 * 1024 * 1024


def _cparams(sem):
    return pltpu.CompilerParams(dimension_semantics=sem, vmem_limit_bytes=VMEM_LIMIT)


def _dot(a, b):
    return jnp.dot(a, b, preferred_element_type=F32)


def _dot_nt(a, b):
    return lax.dot_general(a, b, (((1,), (1,)), ((), ())), preferred_element_type=F32)


def _split(x):
    hi = x.astype(BF16)
    lo = (x - hi.astype(F32)).astype(BF16)
    return hi, lo


def _rmsnorm(x, g):
    return x * lax.rsqrt(jnp.mean(x * x, axis=-1, keepdims=True) + RMS_EPS) * g


def _sigmoid(x):
    return 1.0 / (1.0 + jnp.exp(-x))


def _rope_tile(y, cos, sa, sb):
    return y * cos + pltpu.roll(y, 32, 1) * sa + pltpu.roll(y, 96, 1) * sb


def _proj_kernel(x_ref, g_ref, cos_ref, sa_ref, sb_ref, *refs, segs):
    n_seg = len(segs)
    w_refs = refs[:n_seg]
    o_refs = list(refs[n_seg:])
    nb = _rmsnorm(x_ref[...], g_ref[...]).astype(BF16)
    cos, sa, sb = cos_ref[...], sa_ref[...], sb_ref[...]
    for w_ref, kinds in zip(w_refs, segs):
        y = _dot(nb, w_ref[...])
        if kinds == "plain":
            o_refs.pop(0)[...] = y
            continue
        if kinds == "rope_dual":
            o_refs.pop(0)[...] = y
            kinds = ("rope",) * (y.shape[1] // LANES)
        o_ref = o_refs.pop(0)
        for t, kind in enumerate(kinds):
            yt = y[:, t * LANES:(t + 1) * LANES]
            if kind == "rope":
                yt = _rope_tile(yt, cos, sa, sb)
            elif kind == "sigmoid":
                yt = _sigmoid(yt)
            o_ref[:, t * LANES:(t + 1) * LANES] = yt


def _proj(x, g, tabs, weights, segs, tm):
    n, d = x.shape
    nblk = tabs[0].shape[0] // tm
    out_shape, out_specs = [], []
    for w, kinds in zip(weights, segs):
        for _ in range(2 if kinds == "rope_dual" else 1):
            out_shape.append(jax.ShapeDtypeStruct((n, w.shape[1]), F32))
            out_specs.append(pl.BlockSpec((tm, w.shape[1]), lambda i: (i, 0)))
    tab_spec = pl.BlockSpec((tm, LANES), lambda i: (i % nblk, 0))
    return pl.pallas_call(
        functools.partial(_proj_kernel, segs=tuple(segs)),
        grid=(n // tm,),
        in_specs=[pl.BlockSpec((tm, d), lambda i: (i, 0)),
                  pl.BlockSpec((1, d), lambda i: (0, 0)),
                  tab_spec, tab_spec, tab_spec]
                 + [pl.BlockSpec(w.shape, lambda i: (0, 0)) for w in weights],
        out_specs=out_specs,
        out_shape=out_shape,
        compiler_params=_cparams(("parallel",)),
        name="norm_proj",
    )(x, g, *tabs, *weights)


def _outproj_kernel(h_ref, a_ref, b_ref, wa_ref, wb_ref, o_ref):
    o_ref[...] = (h_ref[...] + _dot(a_ref[...].astype(BF16), wa_ref[...])
                  + _dot(b_ref[...].astype(BF16), wb_ref[...]))


def _outproj(h, a, b, wa, wb, tm):
    n, d = h.shape
    row = lambda w: pl.BlockSpec((tm, w), lambda i: (i, 0))
    full = lambda w: pl.BlockSpec(w.shape, lambda i: (0, 0))
    return pl.pallas_call(
        _outproj_kernel,
        grid=(n // tm,),
        in_specs=[row(d), row(a.shape[1]), row(b.shape[1]), full(wa), full(wb)],
        out_specs=row(d),
        out_shape=jax.ShapeDtypeStruct((n, d), F32),
        compiler_params=_cparams(("parallel",)),
        name="out_proj",
    )(h, a, b, wa, wb)


def _ffn_kernel(h_ref, g_ref, wg_ref, wu_ref, wd_ref, o_ref, n_scr, acc_scr):
    f = pl.program_id(1)

    @pl.when(f == 0)
    def _():
        n_scr[...] = _rmsnorm(h_ref[...], g_ref[...]).astype(BF16)
        acc_scr[...] = jnp.zeros_like(acc_scr)

    nb = n_scr[...]
    gate = _dot(nb, wg_ref[...])
    up = _dot(nb, wu_ref[...])
    hid = (gate * _sigmoid(gate) * up).astype(BF16)
    acc_scr[...] += _dot(hid, wd_ref[...])

    @pl.when(f == pl.num_programs(1) - 1)
    def _():
        o_ref[...] = h_ref[...] + acc_scr[...]


def _ffn(h, g, wg, wu, wd, tm, tf):
    n, d = h.shape
    dff = wg.shape[1]
    return pl.pallas_call(
        _ffn_kernel,
        grid=(n // tm, dff // tf),
        in_specs=[pl.BlockSpec((tm, d), lambda i, f: (i, 0)),
                  pl.BlockSpec((1, d), lambda i, f: (0, 0)),
                  pl.BlockSpec((d, tf), lambda i, f: (0, f)),
                  pl.BlockSpec((d, tf), lambda i, f: (0, f)),
                  pl.BlockSpec((tf, d), lambda i, f: (f, 0))],
        out_specs=pl.BlockSpec((tm, d), lambda i, f: (i, 0)),
        out_shape=jax.ShapeDtypeStruct((n, d), F32),
        scratch_shapes=[pltpu.VMEM((tm, d), BF16), pltpu.VMEM((tm, d), F32)],
        compiler_params=_cparams(("parallel", "arbitrary")),
        name="swiglu_ffn",
    )(h, g, wg, wu, wd)


def _ple_kernel(h_ref, p_ref, g_ref, gf_ref, wg_ref, wp_ref, o_ref, *, final):
    h = h_ref[...]
    gate = _sigmoid(_dot(_rmsnorm(h, g_ref[...]).astype(BF16), wg_ref[...]))
    out = h + gate * _dot(p_ref[...].astype(BF16), wp_ref[...])
    if final:
        out = _rmsnorm(out, gf_ref[...])
    o_ref[...] = out


def _ple(h, p, g, gf, wg, wp, tm, final):
    n, d = h.shape
    row = lambda w: pl.BlockSpec((tm, w), lambda i: (i, 0))
    full = lambda a: pl.BlockSpec(a.shape, lambda i: (0, 0))
    return pl.pallas_call(
        functools.partial(_ple_kernel, final=final),
        grid=(n // tm,),
        in_specs=[row(d), row(p.shape[1]), full(g), full(gf), full(wg), full(wp)],
        out_specs=row(d),
        out_shape=jax.ShapeDtypeStruct((n, d), F32),
        compiler_params=_cparams(("parallel",)),
        name="ple",
    )(h, p, g, gf, wg, wp)


def _moe_kernel(h_ref, g_ref, wr_hi_ref, wr_lo_ref, wg_ref, wu_ref, wd_ref, o_ref,
                n_scr, gw_scr, acc_scr):
    e = pl.program_id(1)
    f = pl.program_id(2)
    lane = lax.broadcasted_iota(jnp.int32, (1, LANES), 1)

    @pl.when((e == 0) & (f == 0))
    def _():
        n = _rmsnorm(h_ref[...], g_ref[...])
        n_hi, n_lo = _split(n)
        n_scr[...] = n_hi
        logits = (_dot(n_hi, wr_hi_ref[...]) + _dot(n_lo, wr_hi_ref[...])
                  + _dot(n_hi, wr_lo_ref[...]))
        logits = jnp.where(lane < N_EXPERTS, logits, NEG)
        lane_f = lane.astype(F32)
        v1 = jnp.max(logits, axis=1, keepdims=True)
        i1 = jnp.min(jnp.where(logits == v1, lane_f, float(LANES)), axis=1, keepdims=True)
        rest = jnp.where(lane_f == i1, NEG, logits)
        v2 = jnp.max(rest, axis=1, keepdims=True)
        i2 = jnp.min(jnp.where(rest == v2, lane_f, float(LANES)), axis=1, keepdims=True)
        e2 = jnp.exp(v2 - v1)
        g1 = 1.0 / (1.0 + e2)
        gw_scr[...] = jnp.where(lane_f == i1, g1, jnp.where(lane_f == i2, e2 * g1, 0.0))
        acc_scr[...] = jnp.zeros_like(acc_scr)

    nb = n_scr[...]
    gate = _dot(nb, wg_ref[0])
    up = _dot(nb, wu_ref[0])
    hid = (gate * _sigmoid(gate) * up).astype(BF16)
    col = jnp.sum(jnp.where(lane == e, gw_scr[...], 0.0), axis=1, keepdims=True)
    acc_scr[...] += col * _dot(hid, wd_ref[0])

    @pl.when((e == pl.num_programs(1) - 1) & (f == pl.num_programs(2) - 1))
    def _():
        o_ref[...] = h_ref[...] + acc_scr[...]


def _moe(h, g, wr_hi, wr_lo, wg, wu, wd, tm, tf):
    n, d = h.shape
    n_e, _, dff = wg.shape
    return pl.pallas_call(
        _moe_kernel,
        grid=(n // tm, n_e, dff // tf),
        in_specs=[pl.BlockSpec((tm, d), lambda i, e, f: (i, 0)),
                  pl.BlockSpec((1, d), lambda i, e, f: (0, 0)),
                  pl.BlockSpec(wr_hi.shape, lambda i, e, f: (0, 0)),
                  pl.BlockSpec(wr_lo.shape, lambda i, e, f: (0, 0)),
                  pl.BlockSpec((1, d, tf), lambda i, e, f: (e, 0, f)),
                  pl.BlockSpec((1, d, tf), lambda i, e, f: (e, 0, f)),
                  pl.BlockSpec((1, tf, d), lambda i, e, f: (e, f, 0))],
        out_specs=pl.BlockSpec((tm, d), lambda i, e, f: (i, 0)),
        out_shape=jax.ShapeDtypeStruct((n, d), F32),
        scratch_shapes=[pltpu.VMEM((tm, d), BF16), pltpu.VMEM((tm, LANES), F32),
                        pltpu.VMEM((tm, d), F32)],
        compiler_params=_cparams(("parallel", "arbitrary", "arbitrary")),
        name="moe_ffn",
    )(h, g, wr_hi, wr_lo, wg, wu, wd)


def _gather_kernel(pt_ref, *refs, n_pg):
    del pt_ref
    page_refs, tail_ref, o_ref = refs[:n_pg], refs[n_pg], refs[n_pg + 1]
    s = pl.program_id(1)
    last = pl.num_programs(1) - 1

    n_heads = page_refs[0].shape[3]
    half = n_heads * HEAD_DIM

    @pl.when(s < last)
    def _():
        for i, p_ref in enumerate(page_refs):
            for kv in range(2):
                for pair in range(n_heads // 2):
                    tile = p_ref[0, 0, kv, 2 * pair:2 * pair + 2].reshape(LANES, PAGE)
                    c0 = kv * half + pair * LANES
                    o_ref[0, i * PAGE:(i + 1) * PAGE, c0:c0 + LANES] = tile.T

    @pl.when(s == last)
    def _():
        o_ref[0, 0:tail_ref.shape[1], :] = tail_ref[0]


def _gather_ctx(cache, page_table, tail, n_pg):
    b, n_pages = page_table.shape
    n_heads = cache.shape[4]
    w = 2 * n_heads * HEAD_DIM
    n_tail = tail.shape[1]
    steps = n_pages // n_pg
    cache = cache.transpose(0, 1, 3, 4, 5, 2)

    def page_spec(i):
        return pl.BlockSpec(
            (1, 1, 2, n_heads, HEAD_DIM, PAGE),
            lambda bi, s, pt: (0, pt[bi, jnp.minimum(s, steps - 1) * n_pg + i], 0, 0, 0, 0))

    return pl.pallas_call(
        functools.partial(_gather_kernel, n_pg=n_pg),
        grid_spec=pltpu.PrefetchScalarGridSpec(
            num_scalar_prefetch=1,
            grid=(b, steps + 1),
            in_specs=[page_spec(i) for i in range(n_pg)]
                     + [pl.BlockSpec((1, n_tail, w), lambda bi, s, pt: (bi, 0, 0))],
            out_specs=pl.BlockSpec((1, n_pg * PAGE, w), lambda bi, s, pt: (bi, s, 0)),
        ),
        out_shape=jax.ShapeDtypeStruct((b, n_pages * PAGE + n_tail, w), F32),
        compiler_params=_cparams(("parallel", "arbitrary")),
        name="paged_gather",
    )(page_table, *([cache] * n_pg), tail)


def _compress_kernel(pt_ref, *refs, n_pg):
    del pt_ref
    k_refs, v_refs = refs[:n_pg], refs[n_pg:2 * n_pg]
    (w1k_ref, w1v_ref, w2_ref, pe_ref, w1_ref, kc_ref, vc_ref,
     ak_scr, av_scr) = refs[2 * n_pg:]
    s = pl.program_id(1)
    n_ch = ak_scr.shape[0]
    per_page = PAGE // NSA_STRIDE

    for p_refs, a_scr in ((k_refs, ak_scr), (v_refs, av_scr)):
        for i, p_ref in enumerate(p_refs):
            base = pl.multiple_of((s * n_pg + i) * per_page, per_page)
            for j in range(NSA_STRIDE):
                a_scr[pl.ds(base, per_page), j * LANES:(j + 1) * LANES] = (
                    p_ref[0, pl.ds(j, per_page, stride=NSA_STRIDE), :])

    @pl.when(s == pl.num_programs(1) - 1)
    def _():
        for kv, (a_scr, w1p_ref, o_ref) in enumerate(((ak_scr, w1k_ref, kc_ref),
                                                      (av_scr, w1v_ref, vc_ref))):
            r = _dot(a_scr[...].astype(BF16), w1p_ref[...])
            bias = _dot(pe_ref[kv], w1_ref[kv])[0:1, :]
            hid = []
            for g in range(2):
                a = r[:, (2 * g) * NSA_HID:(2 * g + 1) * NSA_HID]
                b_next = pltpu.roll(r[:, (2 * g + 1) * NSA_HID:(2 * g + 2) * NSA_HID],
                                    n_ch - 1, 0)
                pre = a + b_next + bias
                hid.append(pre * _sigmoid(pre))
            hid = jnp.concatenate(hid, axis=1).astype(BF16)
            o_ref[0] = _dot(hid, w2_ref[kv])


def _compress(pool, page_table, w1k, w1v, w2, pe, w1, n_pg):
    b, n_pages = page_table.shape
    n_ch = n_pages * (PAGE // NSA_STRIDE)
    steps = n_pages // n_pg
    full = lambda a: pl.BlockSpec(a.shape, lambda bi, s, pt: (0,) * a.ndim)

    def page_spec(i, kv):
        return pl.BlockSpec((1, PAGE, LANES), lambda bi, s, pt: (pt[bi, s * n_pg + i], 0, kv))

    out_spec = pl.BlockSpec((1, n_ch, LANES), lambda bi, s, pt: (bi, 0, 0))
    return pl.pallas_call(
        functools.partial(_compress_kernel, n_pg=n_pg),
        grid_spec=pltpu.PrefetchScalarGridSpec(
            num_scalar_prefetch=1,
            grid=(b, steps),
            in_specs=[page_spec(i, kv) for kv in range(2) for i in range(n_pg)]
                     + [full(w1k), full(w1v), full(w2), full(pe), full(w1)],
            out_specs=[out_spec, out_spec],
            scratch_shapes=[pltpu.VMEM((n_ch, NSA_STRIDE * LANES), F32),
                            pltpu.VMEM((n_ch, NSA_STRIDE * LANES), F32)],
        ),
        out_shape=[jax.ShapeDtypeStruct((b, n_ch, LANES), F32)] * 2,
        compiler_params=_cparams(("parallel", "arbitrary")),
        name="nsa_compress",
    )(page_table, *([pool] * (2 * n_pg)), w1k, w1v, w2, pe, w1)


def _flash_step(state, s, valid, v):
    m, l, acc = state
    s = jnp.where(valid, s, NEG)
    m_new = jnp.maximum(m, jnp.max(s, axis=1, keepdims=True))
    alpha = jnp.exp(m - m_new)
    p = jnp.where(valid, jnp.exp(s - m_new), 0.0)
    l = alpha * l + jnp.sum(p, axis=1, keepdims=True)
    acc = alpha * acc + _dot(p.astype(BF16), v)
    return m_new, l, acc


def _flash_init(rows):
    return (jnp.full((rows, 1), NEG, F32), jnp.zeros((rows, 1), F32),
            jnp.zeros((rows, LANES), F32))


def _flash_out(state):
    _, l, acc = state
    return jnp.where(l > 0.0, acc / jnp.where(l > 0.0, l, 1.0), 0.0)


def _stack(x, times):
    return jnp.concatenate([x] * times, axis=0)


N_WIN_TILES = NSA_WINDOW // LANES + 1


def _nsa_kernel(qp_ref, qr_ref, gt_ref, kc_ref, vc_ref, sel_ref, ov_ref, ex_ref, *refs,
                tq, tk, pos0, wpos0):
    win_refs, o_ref = refs[:N_WIN_TILES], refs[N_WIN_TILES]
    i = pl.program_id(1)
    q0 = pos0 + i * tq
    scale = HEAD_DIM ** -0.5
    r4 = NSA_GROUP
    qpos = q0 + lax.broadcasted_iota(jnp.int32, (tq, 1), 0)
    qpos4 = _stack(qpos, r4)
    qblk = qpos // NSA_SEL_BLOCK
    lane = lax.broadcasted_iota(jnp.int32, (1, LANES), 1)
    lane_f = lane.astype(F32)
    n_ch = kc_ref.shape[1]
    c_end = lax.broadcasted_iota(jnp.int32, (1, n_ch), 1) * NSA_STRIDE + (2 * NSA_STRIDE - 1)
    kc = kc_ref[0].astype(BF16)
    vc = vc_ref[0].astype(BF16)
    gates = gt_ref[0]
    n_sel_tiles = (q0 + tq - 1) // tk + 1
    win_start = (q0 // LANES) * LANES - NSA_WINDOW

    q_rot, o_cmp, picked = [], [], []
    for g in range(2):
        heads = range(g * r4, (g + 1) * r4)
        q_c = jnp.concatenate([qp_ref[0, :, h * LANES:(h + 1) * LANES] for h in heads], axis=0)
        q_r = jnp.concatenate([qr_ref[0, :, h * LANES:(h + 1) * LANES] for h in heads], axis=0)
        q_c = (q_c * scale).astype(BF16)
        q_rot.append((q_r * scale).astype(BF16))

        s = _dot_nt(q_c, kc)
        valid = c_end <= qpos4
        s = jnp.where(valid, s, NEG)
        e = jnp.where(valid, jnp.exp(s - jnp.max(s, axis=1, keepdims=True)), 0.0)
        l = jnp.sum(e, axis=1, keepdims=True)
        p = jnp.where(l > 0.0, e / jnp.where(l > 0.0, l, 1.0), 0.0)
        o_cmp.append(_dot(p.astype(BF16), vc))

        p_sum = p[0:tq]
        for r in range(1, r4):
            p_sum = p_sum + p[r * tq:(r + 1) * tq]
        p_hi, p_lo = _split(p_sum)
        imp = _dot(p_hi, ov_ref[...]) + _dot(p_lo, ov_ref[...])

        cand = (lane <= qblk) & (lane != 0) & (lane != qblk) & (lane != qblk - 1)
        work = jnp.where(cand, imp, -1.0)
        pick = jnp.zeros((tq, LANES), F32)
        for _ in range(NSA_PICKS):
            top = jnp.max(work, axis=1, keepdims=True)
            first = jnp.min(jnp.where(work == top, lane_f, float(LANES)), axis=1,
                            keepdims=True)
            hit = (lane_f == first) & (top >= 0.0)
            pick = jnp.where(hit, 1.0, pick)
            work = jnp.where(hit, -1.0, work)
        picked.append(pick.astype(BF16))

    def sel_body(t, states):
        k0 = pl.multiple_of(t * tk, tk)
        k = sel_ref[0, pl.ds(k0, tk), 0:LANES].astype(BF16)
        v = sel_ref[0, pl.ds(k0, tk), LANES:2 * LANES].astype(BF16)
        kpos = k0 + lax.broadcasted_iota(jnp.int32, (1, tk), 1)
        kblk = kpos // NSA_SEL_BLOCK
        forced = (kblk == 0) | (kblk == qblk) | (kblk == qblk - 1)
        out = ()
        for g in range(2):
            chosen = _dot(picked[g], ex_ref[t]) > 0.5
            keep = jnp.where((chosen | forced) & (kpos <= qpos), 1.0, 0.0)
            out += _flash_step(states[3 * g:3 * g + 3], _dot_nt(q_rot[g], k),
                               _stack(keep, r4) > 0.5, v)
        return out

    sel_states = lax.fori_loop(0, n_sel_tiles, sel_body, _flash_init(r4 * tq) * 2)
    o_sel = [_flash_out(sel_states[0:3]), _flash_out(sel_states[3:6])]

    win_states = [_flash_init(r4 * tq), _flash_init(r4 * tq)]
    for jt, w_ref in enumerate(win_refs):
        k = w_ref[0, :, 0:LANES].astype(BF16)
        v = w_ref[0, :, LANES:2 * LANES].astype(BF16)
        wpos = win_start + jt * LANES + lane
        keep = jnp.where((wpos >= wpos0) & (wpos <= qpos) & (qpos - wpos < NSA_WINDOW),
                         1.0, 0.0)
        keep4 = _stack(keep, r4) > 0.5
        for g in range(2):
            win_states[g] = _flash_step(win_states[g], _dot_nt(q_rot[g], k), keep4, v)
    o_win = [_flash_out(win_states[0]), _flash_out(win_states[1])]

    outs = []
    for g in range(2):
        for r in range(r4):
            h = g * r4 + r
            rows = slice(r * tq, (r + 1) * tq)
            o = (gates[:, 3 * h:3 * h + 1] * o_cmp[g][rows]
                 + gates[:, 3 * h + 1:3 * h + 2] * o_sel[g][rows]
                 + gates[:, 3 * h + 2:3 * h + 3] * o_win[g][rows])
            if h % 2 != g:
                o = pltpu.roll(o, HEAD_DIM, 1)
            outs.append(o)

    for pair in range(NSA_HEADS // 2):
        o_ref[0, :, pair * LANES:(pair + 1) * LANES] = jnp.where(
            lane < HEAD_DIM, outs[2 * pair], outs[2 * pair + 1])


def _nsa_t_kernel(qp_ref, qr_ref, gt_ref, kc_ref, vc_ref, sel_ref, ovt_ref, ext_ref, *refs,
                  tq, tk, pos0, wpos0):
    win_refs, o_ref, svt_scr, vct_scr = (refs[:N_WIN_TILES], refs[N_WIN_TILES],
                                         refs[N_WIN_TILES + 1], refs[N_WIN_TILES + 2])
    i = pl.program_id(1)
    q0 = pos0 + i * tq
    scale = HEAD_DIM ** -0.5
    r4 = NSA_GROUP
    n_ch = kc_ref.shape[1]
    row = lax.broadcasted_iota(jnp.int32, (LANES, 1), 0)
    row_f = row.astype(F32)
    qpos = q0 + lax.broadcasted_iota(jnp.int32, (1, tq), 1)
    qblk = qpos // NSA_SEL_BLOCK
    n_sel_tiles = (q0 + tq - 1) // tk + 1
    win_start = (q0 // LANES) * LANES - NSA_WINDOW
    own_half = [row < HEAD_DIM, row >= HEAD_DIM]

    def lanes4(x):
        return jnp.concatenate([x] * r4, axis=1)

    def v_t(tiles):
        return jnp.concatenate([t.T for t in tiles], axis=1)

    @pl.when(i == 0)
    def _():
        def transpose_tile(t, carry):
            k0 = pl.multiple_of(t * tk, tk)
            vt = v_t([sel_ref[0, pl.ds(k0 + j * LANES, LANES), LANES:2 * LANES]
                      for j in range(tk // LANES)])
            for g in range(2):
                svt_scr[g, t] = jnp.where(own_half[g], vt, 1.0).astype(BF16)
            return carry

        lax.fori_loop(0, sel_ref.shape[1] // tk, transpose_tile, 0)
        vct = v_t([vc_ref[0, j * LANES:(j + 1) * LANES, :] for j in range(n_ch // LANES)])
        for g in range(2):
            vct_scr[g] = jnp.where(own_half[g], vct, 1.0).astype(BF16)

    kc = kc_ref[0].astype(BF16)
    c_end = (lax.broadcasted_iota(jnp.int32, (n_ch, 1), 0) * NSA_STRIDE
             + (2 * NSA_STRIDE - 1))
    cmp_bias = lanes4(jnp.where(c_end <= qpos, 0.0, NEG))

    q_rot, o_cmp, picked = [], [], []
    for g in range(2):
        heads = range(g * r4, (g + 1) * r4)
        q_c = jnp.concatenate([qp_ref[0, :, h * LANES:(h + 1) * LANES] for h in heads], axis=0)
        q_r = jnp.concatenate([qr_ref[0, :, h * LANES:(h + 1) * LANES] for h in heads], axis=0)
        q_c = (q_c * scale).astype(BF16)
        q_rot.append((q_r * scale).astype(BF16))

        s = _dot_nt(kc, q_c) + cmp_bias
        m = jnp.max(s, axis=0, keepdims=True)
        e = jnp.exp(s - m)
        p = jnp.where(m > 0.5 * NEG, e / jnp.sum(e, axis=0, keepdims=True), 0.0)
        o_cmp.append(_dot(vct_scr[g], p.astype(BF16)))

        p_sum = p[:, 0:tq]
        for r in range(1, r4):
            p_sum = p_sum + p[:, r * tq:(r + 1) * tq]
        p_hi, p_lo = _split(p_sum)
        imp = _dot(ovt_ref[...], p_hi) + _dot(ovt_ref[...], p_lo)

        cand = (row <= qblk) & (row != 0) & (row != qblk) & (row != qblk - 1)
        work = jnp.where(cand, imp, -1.0)
        pick = jnp.zeros((LANES, tq), F32)
        for _ in range(NSA_PICKS):
            top = jnp.max(work, axis=0, keepdims=True)
            first = jnp.min(jnp.where(work == top, row_f, float(LANES)), axis=0, keepdims=True)
            hit = (row_f == first) & (top >= 0.0)
            pick = jnp.where(hit, 1.0, pick)
            work = jnp.where(hit, -1.0, work)
        picked.append(pick.astype(BF16))

    def flash_t(state, s, vt):
        m, acc = state
        m_new = jnp.maximum(m, jnp.max(s, axis=0, keepdims=True))
        p = jnp.exp(s - m_new).astype(BF16)
        return m_new, jnp.exp(m - m_new) * acc + _dot(vt, p)

    def init_t():
        return (jnp.full((1, r4 * tq), NEG, F32), jnp.zeros((LANES, r4 * tq), F32))

    def sel_body(t, states):
        k0 = pl.multiple_of(t * tk, tk)
        k = sel_ref[0, pl.ds(k0, tk), 0:LANES].astype(BF16)
        kpos = k0 + lax.broadcasted_iota(jnp.int32, (tk, 1), 0)
        kblk = kpos // NSA_SEL_BLOCK
        forced = (kblk == 0) | (kblk == qblk) | (kblk == qblk - 1)
        causal = kpos <= qpos
        out = ()
        for g in range(2):
            chosen = _dot(ext_ref[t], picked[g]) > 0.5
            bias = lanes4(jnp.where((chosen | forced) & causal, 0.0, NEG))
            out += flash_t(states[2 * g:2 * g + 2], _dot_nt(k, q_rot[g]) + bias, svt_scr[g, t])
        return out

    sel_states = lax.fori_loop(0, n_sel_tiles, sel_body, init_t() * 2)

    win_states = [init_t(), init_t()]
    for jt, w_ref in enumerate(win_refs):
        k = w_ref[0, :, 0:LANES].astype(BF16)
        vt = w_ref[0, :, LANES:2 * LANES].T
        wpos = win_start + jt * LANES + row
        keep = (wpos >= wpos0) & (wpos <= qpos) & (qpos - wpos < NSA_WINDOW)
        bias = lanes4(jnp.where(keep, 0.0, NEG))
        for g in range(2):
            win_states[g] = flash_t(win_states[g], _dot_nt(k, q_rot[g]) + bias,
                                    jnp.where(own_half[g], vt, 1.0).astype(BF16))

    gates_t = gt_ref[0].T
    for g in range(2):
        ones_row = HEAD_DIM * (1 - g)
        o_sel = sel_states[2 * g + 1] / sel_states[2 * g + 1][ones_row:ones_row + 1, :]
        o_win = win_states[g][1] / win_states[g][1][ones_row:ones_row + 1, :]
        mixed = []
        for r in range(r4):
            h = g * r4 + r
            cols = slice(r * tq, (r + 1) * tq)
            o = (gates_t[3 * h:3 * h + 1, :] * o_cmp[g][:, cols]
                 + gates_t[3 * h + 1:3 * h + 2, :] * o_sel[:, cols]
                 + gates_t[3 * h + 2:3 * h + 3, :] * o_win[:, cols])
            mixed.append(o[g * HEAD_DIM:(g + 1) * HEAD_DIM, :])
        for pr in range(r4 // 2):
            pair = g * (r4 // 2) + pr
            tile = jnp.concatenate([mixed[2 * pr], mixed[2 * pr + 1]], axis=0)
            o_ref[0, :, pair * LANES:(pair + 1) * LANES] = tile.T


def _nsa_attn(qp, qr, gates, kc, vc, sel_ctx, win_ctx, ov, ex, tq, tk, pos0, wpos0):
    b, t_q, _ = qp.shape
    t_c = sel_ctx.shape[1]
    n_win_tiles_total = win_ctx.shape[1] // LANES
    tile0 = pos0 // LANES - NSA_WINDOW // LANES - wpos0 // LANES
    transposed = tq % LANES == 0
    if transposed:
        body, ov, ex = _nsa_t_kernel, ov.T, ex.transpose(0, 2, 1)
        scratch = [pltpu.VMEM((2, t_c // tk, LANES, tk), BF16),
                   pltpu.VMEM((2, LANES, kc.shape[1]), BF16)]
    else:
        body, scratch = _nsa_kernel, []

    def win_spec(jt):
        def idx(bi, i):
            t = tile0 + (i * tq) // LANES + jt
            return (bi, jnp.clip(t, 0, n_win_tiles_total - 1), 0)
        return pl.BlockSpec((1, LANES, 2 * LANES), idx)

    qspec = pl.BlockSpec((1, tq, NSA_HEADS * LANES), lambda bi, i: (bi, i, 0))
    return pl.pallas_call(
        functools.partial(body, tq=tq, tk=tk, pos0=pos0, wpos0=wpos0),
        grid=(b, t_q // tq),
        in_specs=[qspec, qspec,
                  pl.BlockSpec((1, tq, LANES), lambda bi, i: (bi, i, 0)),
                  pl.BlockSpec((1,) + kc.shape[1:], lambda bi, i: (bi, 0, 0)),
                  pl.BlockSpec((1,) + vc.shape[1:], lambda bi, i: (bi, 0, 0)),
                  pl.BlockSpec((1, t_c, 2 * LANES), lambda bi, i: (bi, 0, 0)),
                  pl.BlockSpec(ov.shape, lambda bi, i: (0, 0)),
                  pl.BlockSpec(ex.shape, lambda bi, i: (0, 0, 0))]
                 + [win_spec(jt) for jt in range(N_WIN_TILES)],
        out_specs=pl.BlockSpec((1, tq, NSA_HEADS * HEAD_DIM), lambda bi, i: (bi, i, 0)),
        out_shape=jax.ShapeDtypeStruct((b, t_q, NSA_HEADS * HEAD_DIM), F32),
        scratch_shapes=scratch,
        compiler_params=_cparams(("parallel", "arbitrary")),
        name="nsa_attn_t" if transposed else "nsa_attn",
    )(qp, qr, gates, kc, vc, sel_ctx, ov, ex, *([win_ctx] * N_WIN_TILES))


def _sb_kernel(q_ref, k_ref, v_ref, tri_ref, o_ref, *, tq, tk, pos0):
    i = pl.program_id(2)
    q0 = pos0 + i * tq
    scale = HEAD_DIM ** -0.5
    qpos = q0 + lax.broadcasted_iota(jnp.int32, (tq, 1), 0)
    lane = lax.broadcasted_iota(jnp.int32, (1, LANES), 1)
    n_tiles = (q0 + tq - 2) // tk + 1
    tri = tri_ref[...]
    qs = [(q_ref[0, :, hh * LANES:(hh + 1) * LANES] * scale).astype(BF16) for hh in range(2)]

    def tile(q, k, v, valid, acc, run):
        z = _dot_nt(q, k)
        soft = jnp.log(1.0 + jnp.exp(-jnp.abs(z)))
        log_break = jnp.minimum(z, 0.0) - soft
        log_stay = jnp.where(valid, log_break - z, 0.0)
        s_hi, s_lo = _split(log_stay)
        after = _dot(s_hi, tri) + _dot(s_lo, tri)
        a = jnp.where(valid, jnp.exp(log_break + after + run), 0.0)
        return acc + _dot(a.astype(BF16), v), run + jnp.sum(log_stay, axis=1, keepdims=True)

    def cond(carry):
        return (carry[0] < n_tiles) & (carry[1] > 0)

    def body(carry):
        step, _, acc0, run0, acc1, run1 = carry
        k0 = pl.multiple_of((n_tiles - 1 - step) * tk, tk)
        k = k_ref[0, pl.ds(k0, tk), :].astype(BF16)
        v = v_ref[0, pl.ds(k0, tk), :].astype(BF16)
        valid = (k0 + lax.broadcasted_iota(jnp.int32, (1, tk), 1)) < qpos
        acc0, run0 = tile(qs[0], k, v, valid, acc0, run0)
        acc1, run1 = tile(qs[1], k, v, valid, acc1, run1)
        live = jnp.maximum(jnp.max(run0), jnp.max(run1)) > SB_DEAD
        return step + 1, live.astype(jnp.int32), acc0, run0, acc1, run1

    zero_acc = jnp.zeros((tq, LANES), F32)
    zero_run = jnp.zeros((tq, 1), F32)
    _, _, acc0, _, acc1, _ = lax.while_loop(
        cond, body, (jnp.int32(0), jnp.int32(1), zero_acc, zero_run, zero_acc, zero_run))
    o_ref[0] = jnp.where(lane < HEAD_DIM, acc0, acc1)


def _sb_attn(q, kv, tri, tq, tk, pos0):
    b, t_q, _ = q.shape
    t_c = kv.shape[1]
    n_pairs = SB_HEADS // 2
    return pl.pallas_call(
        functools.partial(_sb_kernel, tq=tq, tk=tk, pos0=pos0),
        grid=(b, n_pairs, t_q // tq),
        in_specs=[pl.BlockSpec((1, tq, 2 * LANES), lambda bi, pr, i: (bi, i, pr)),
                  pl.BlockSpec((1, t_c, LANES), lambda bi, pr, i: (bi, 0, pr)),
                  pl.BlockSpec((1, t_c, LANES), lambda bi, pr, i: (bi, 0, n_pairs + pr)),
                  pl.BlockSpec(tri.shape, lambda bi, pr, i: (0, 0))],
        out_specs=pl.BlockSpec((1, tq, LANES), lambda bi, pr, i: (bi, i, pr)),
        out_shape=jax.ShapeDtypeStruct((b, t_q, SB_HEADS * HEAD_DIM), F32),
        compiler_params=_cparams(("parallel", "parallel", "arbitrary")),
        name="sb_attn",
    )(q, kv, kv, tri)


def _moba_kernel(q_ref, k_ref, v_ref, o_ref, km_scr, *, tq, pos0, n_blocks, bps):
    i = pl.program_id(2)
    blk = MOBA_BLOCK
    q0 = pos0 + i * tq
    own = q0 // blk
    scale = HEAD_DIM ** -0.5
    qpos = q0 + lax.broadcasted_iota(jnp.int32, (tq, 1), 0)
    lane = lax.broadcasted_iota(jnp.int32, (1, LANES), 1)
    lane_f = lane.astype(F32)

    @pl.when(i == 0)
    def _():
        km_scr[...] = jnp.zeros_like(km_scr)
        for n in range(n_blocks):
            km_scr[n:n + 1, :] = jnp.sum(k_ref[0, n * blk:(n + 1) * blk, :], axis=0,
                                         keepdims=True) * (1.0 / blk)

    km_hi, km_lo = _split(km_scr[...])
    q_s, picked, states = [], [], []
    own0 = pl.multiple_of(own * blk, blk)
    k_own = k_ref[0, pl.ds(own0, blk), :].astype(BF16)
    v_own = v_ref[0, pl.ds(own0, blk), :].astype(BF16)
    causal = (own0 + lax.broadcasted_iota(jnp.int32, (1, blk), 1)) <= qpos
    for hh in range(2):
        q = q_ref[0, :, hh * LANES:(hh + 1) * LANES]
        q_hi, q_lo = _split(q)
        gate = _dot_nt(q_hi, km_hi) + _dot_nt(q_lo, km_hi) + _dot_nt(q_hi, km_lo)
        work = jnp.where(lane < own, gate, NEG)
        pick = jnp.zeros((tq, LANES), F32)
        for _ in range(MOBA_TOPK):
            top = jnp.max(work, axis=1, keepdims=True)
            first = jnp.min(jnp.where(work == top, lane_f, float(LANES)), axis=1,
                            keepdims=True)
            hit = (lane_f == first) & (top > 0.5 * NEG)
            pick = jnp.where(hit, 1.0, pick)
            work = jnp.where(hit, NEG, work)
        picked.append(pick)
        q_s.append((q * scale).astype(BF16))
        states.append(_flash_step(_flash_init(tq), _dot_nt(q_s[hh], k_own), causal, v_own))

    def body(t, carry):
        k0 = pl.multiple_of(t * (bps * blk), bps * blk)
        k = k_ref[0, pl.ds(k0, bps * blk), :].astype(BF16)
        v = v_ref[0, pl.ds(k0, bps * blk), :].astype(BF16)
        out = []
        for hh in range(2):
            m, l, acc = carry[3 * hh:3 * hh + 3]
            s = _dot_nt(q_s[hh], k)
            rows = [jnp.sum(jnp.where(lane == t * bps + j, picked[hh], 0.0), axis=1,
                            keepdims=True) > 0.5 for j in range(bps)]
            parts = [s[:, j * blk:(j + 1) * blk] for j in range(bps)]
            m_new = m
            for j in range(bps):
                m_new = jnp.maximum(m_new, jnp.where(
                    rows[j], jnp.max(parts[j], axis=1, keepdims=True), NEG))
            alpha = jnp.exp(m - m_new)
            p = [jnp.exp(parts[j] - jnp.where(rows[j], m_new, -NEG)) for j in range(bps)]
            p = p[0] if bps == 1 else jnp.concatenate(p, axis=1)
            l = alpha * l + jnp.sum(p, axis=1, keepdims=True)
            acc = alpha * acc + _dot(p.astype(BF16), v)
            out += [m_new, l, acc]
        return tuple(out)

    n_steps = (own + bps - 1) // bps
    final = lax.fori_loop(0, n_steps, body, tuple(states[0]) + tuple(states[1]))
    o_ref[0] = jnp.where(lane < HEAD_DIM, _flash_out(final[0:3]), _flash_out(final[3:6]))


def _moba_t_kernel(q_ref, k_ref, v_ref, o_ref, km_scr, vt_scr, pk_scr, *, tq, pos0, n_blocks,
                   bps):
    i = pl.program_id(2)
    blk = MOBA_BLOCK
    sub = blk // LANES
    q0 = pos0 + i * tq
    own = q0 // blk
    scale = HEAD_DIM ** -0.5
    row = lax.broadcasted_iota(jnp.int32, (LANES, 1), 0)
    row_f = row.astype(F32)
    qpos = q0 + lax.broadcasted_iota(jnp.int32, (1, tq), 1)

    @pl.when(i == 0)
    def _():
        km_scr[...] = jnp.zeros_like(km_scr)
        for n in range(n_blocks):
            km_scr[n:n + 1, :] = jnp.sum(k_ref[0, n * blk:(n + 1) * blk, :], axis=0,
                                         keepdims=True) * (1.0 / blk)

        def transpose_block(n, carry):
            n0 = pl.multiple_of(n * blk, blk)
            vt = jnp.concatenate(
                [v_ref[0, pl.ds(n0 + j * LANES, LANES), :].T for j in range(sub)], axis=1)
            vt_scr[0, n] = jnp.where(row < HEAD_DIM, vt, 1.0).astype(BF16)
            vt_scr[1, n] = jnp.where(row >= HEAD_DIM, vt, 1.0).astype(BF16)
            return carry

        lax.fori_loop(0, n_blocks, transpose_block, 0)

    def weighted_values(hh, t, p):
        return _dot(vt_scr[hh, t], p)

    km_hi, km_lo = _split(km_scr[...])
    own0 = pl.multiple_of(own * blk, blk)
    k_own = k_ref[0, pl.ds(own0, blk), :].astype(BF16)
    causal = (own0 + lax.broadcasted_iota(jnp.int32, (blk, 1), 0)) <= qpos
    q_s, states = [], []
    for hh in range(2):
        q = q_ref[0, :, hh * LANES:(hh + 1) * LANES]
        q_hi, q_lo = _split(q)
        gate = _dot_nt(km_hi, q_hi) + _dot_nt(km_hi, q_lo) + _dot_nt(km_lo, q_hi)
        work = jnp.where(row < own, gate, NEG)
        pick = jnp.zeros((LANES, tq), F32)
        for _ in range(MOBA_TOPK):
            top = jnp.max(work, axis=0, keepdims=True)
            first = jnp.min(jnp.where(work == top, row_f, float(LANES)), axis=0, keepdims=True)
            hit = (row_f == first) & (top > 0.5 * NEG)
            pick = jnp.where(hit, 1.0, pick)
            work = jnp.where(hit, NEG, work)
        pk_scr[hh] = pick
        q_s.append((q * scale).astype(BF16))
        s = jnp.where(causal, _dot_nt(k_own, q_s[hh]), NEG)
        m = jnp.max(s, axis=0, keepdims=True)
        p = jnp.where(causal, jnp.exp(s - m), 0.0).astype(BF16)
        states += [m, weighted_values(hh, own, p)]

    def body(t, carry):
        k0 = pl.multiple_of(t * (bps * blk), bps * blk)
        k = k_ref[0, pl.ds(k0, bps * blk), :].astype(BF16)
        out = []
        for hh in range(2):
            m, acc = carry[2 * hh:2 * hh + 2]
            s = _dot_nt(k, q_s[hh])
            parts = [s[j * blk:(j + 1) * blk] for j in range(bps)]
            chosen = [pk_scr[hh, pl.ds(t * bps + j, 1), :] > 0.5 for j in range(bps)]
            m_new = m
            for j in range(bps):
                m_new = jnp.maximum(m_new, jnp.where(
                    chosen[j], jnp.max(parts[j], axis=0, keepdims=True), NEG))
            acc = jnp.exp(m - m_new) * acc
            for j in range(bps):
                p = jnp.exp(parts[j] - jnp.where(chosen[j], m_new, -NEG)).astype(BF16)
                acc = acc + weighted_values(hh, t * bps + j, p)
            out += [m_new, acc]
        return tuple(out)

    _, acc0, _, acc1 = lax.fori_loop(0, (own + bps - 1) // bps, body, tuple(states))
    o_t = jnp.where(row < HEAD_DIM, acc0 / acc0[HEAD_DIM:HEAD_DIM + 1, :], acc1 / acc1[0:1, :])
    o_ref[0] = o_t.T


def _moba_attn_t(q, kv, tq, pos0, bps):
    b, t_q, _ = q.shape
    t_c = kv.shape[1]
    n_pairs = MOBA_HEADS // 2
    n_blocks = t_c // MOBA_BLOCK
    assert MOBA_BLOCK % tq == 0 and pos0 % tq == 0 and tq % LANES == 0
    assert n_blocks % bps == 0
    return pl.pallas_call(
        functools.partial(_moba_t_kernel, tq=tq, pos0=pos0, n_blocks=n_blocks, bps=bps),
        grid=(b, n_pairs, t_q // tq),
        in_specs=[pl.BlockSpec((1, tq, 2 * LANES), lambda bi, pr, i: (bi, i, pr)),
                  pl.BlockSpec((1, t_c, LANES), lambda bi, pr, i: (bi, 0, pr)),
                  pl.BlockSpec((1, t_c, LANES), lambda bi, pr, i: (bi, 0, n_pairs + pr))],
        out_specs=pl.BlockSpec((1, tq, LANES), lambda bi, pr, i: (bi, i, pr)),
        out_shape=jax.ShapeDtypeStruct((b, t_q, MOBA_HEADS * HEAD_DIM), F32),
        scratch_shapes=[pltpu.VMEM((LANES, LANES), F32),
                        pltpu.VMEM((2, n_blocks, LANES, MOBA_BLOCK), BF16),
                        pltpu.VMEM((2, LANES, tq), F32)],
        compiler_params=_cparams(("parallel", "parallel", "arbitrary")),
        name="moba_attn_t",
    )(q, kv, kv)


def _moba_attn(q, kv, tq, pos0, bps):
    b, t_q, _ = q.shape
    t_c = kv.shape[1]
    n_pairs = MOBA_HEADS // 2
    assert MOBA_BLOCK % tq == 0 and pos0 % tq == 0 and (pos0 // MOBA_BLOCK) % bps == 0
    return pl.pallas_call(
        functools.partial(_moba_kernel, tq=tq, pos0=pos0, n_blocks=t_c // MOBA_BLOCK, bps=bps),
        grid=(b, n_pairs, t_q // tq),
        in_specs=[pl.BlockSpec((1, tq, 2 * LANES), lambda bi, pr, i: (bi, i, pr)),
                  pl.BlockSpec((1, t_c, LANES), lambda bi, pr, i: (bi, 0, pr)),
                  pl.BlockSpec((1, t_c, LANES), lambda bi, pr, i: (bi, 0, n_pairs + pr))],
        out_specs=pl.BlockSpec((1, tq, LANES), lambda bi, pr, i: (bi, i, pr)),
        out_shape=jax.ShapeDtypeStruct((b, t_q, MOBA_HEADS * HEAD_DIM), F32),
        scratch_shapes=[pltpu.VMEM((LANES, LANES), F32)],
        compiler_params=_cparams(("parallel", "parallel", "arbitrary")),
        name="moba_attn",
    )(q, kv, kv)


def _pad_heads(w, halves):
    d = w.shape[0]
    n_h = len(halves)
    onehot = jax.nn.one_hot(jnp.asarray(halves), 2, dtype=w.dtype)
    return jnp.einsum("dhe,hs->dhse", w.reshape(d, n_h, HEAD_DIM), onehot).reshape(d, n_h * LANES)


def _rope_tables(pos):
    half = HEAD_DIM // 2
    inv = 1.0 / (ROPE_THETA ** (jnp.arange(half, dtype=F32) / half))
    ang = pos.astype(F32)[:, None] * inv[None, :]
    cos = jnp.tile(jnp.cos(ang), (1, LANES // half))
    sin = jnp.tile(jnp.sin(ang), (1, LANES // half))
    upper = (jnp.arange(LANES) % HEAD_DIM) >= half
    return cos, jnp.where(upper, sin, 0.0), jnp.where(upper, 0.0, -sin)


def _overlap_matrix(n_ch):
    c = jnp.arange(n_ch)[:, None]
    n = jnp.arange(LANES)[None, :]
    return ((c >= 4 * n - 1) & (c <= 4 * n + 3)).astype(BF16)


def _expand_matrix(t_c, tk):
    blk = (jnp.arange(t_c) // NSA_SEL_BLOCK).reshape(t_c // tk, 1, tk)
    return (jnp.arange(LANES)[None, :, None] == blk).astype(BF16)


def _compress_weights(w1, w2):
    half = NSA_STRIDE * HEAD_DIM
    out = []
    for kv in range(2):
        wa = w1[kv, :half].reshape(NSA_STRIDE, HEAD_DIM, NSA_HID)
        wb = w1[kv, half:].reshape(NSA_STRIDE, HEAD_DIM, NSA_HID)
        ab = jnp.concatenate([wa, wb], axis=2)
        z = jnp.zeros_like(ab)
        g0 = jnp.concatenate([ab, z], axis=1)
        g1 = jnp.concatenate([z, ab], axis=1)
        out.append(jnp.concatenate([g0, g1], axis=2).reshape(NSA_STRIDE * LANES, 4 * NSA_HID))
    z2 = jnp.zeros_like(w2)
    w2p = jnp.concatenate([jnp.concatenate([w2, z2], axis=2),
                           jnp.concatenate([z2, w2], axis=2)], axis=1)
    return out[0].astype(BF16), out[1].astype(BF16), w2p.astype(BF16)


def _prep_weights(w_in_even, w_out_even, nsa_cmp_pe, nsa_cmp_w1, nsa_cmp_w2, w_ffn_gate,
                  w_ffn_up, w_ffn_down, w_in_odd, w_out_odd, w_router, w_exp_gate, w_exp_up,
                  w_exp_down, w_ple_proj, w_ple_gate):
    bf = lambda a: a.astype(BF16)
    we = w_in_even[0]
    qw = NSA_HEADS * HEAD_DIM
    kvw = 2 * LANES
    c0 = qw + 3 * kvw
    n_gate = 3 * NSA_HEADS
    s0 = c0 + n_gate
    sbw = SB_HEADS * HEAD_DIM
    even = [
        bf(_pad_heads(we[:, :qw], [h // NSA_GROUP for h in range(NSA_HEADS)])),
        bf(we[:, qw:qw + kvw]),
        bf(we[:, qw + kvw:qw + 2 * kvw]),
        bf(we[:, qw + 2 * kvw:c0]),
        bf(jnp.pad(we[:, c0:s0], ((0, 0), (0, LANES - n_gate)))),
        bf(_pad_heads(we[:, s0:s0 + sbw], [h % 2 for h in range(SB_HEADS)])),
        bf(we[:, s0 + sbw:]),
    ]
    wo = w_in_odd[0]
    mw = MOBA_HEADS * HEAD_DIM
    odd = [bf(_pad_heads(wo[:, :mw], [h % 2 for h in range(MOBA_HEADS)])), bf(wo[:, mw:])]
    w1k, w1v, w2p = _compress_weights(nsa_cmp_w1[0], nsa_cmp_w2[0])
    pe = bf(jnp.broadcast_to(nsa_cmp_pe[0].reshape(2, 1, -1), (2, 8, 2 * NSA_STRIDE * HEAD_DIM)))
    wr = jnp.pad(w_router[0], ((0, 0), (0, LANES - N_EXPERTS)))
    wr_hi, wr_lo = _split(wr)
    return dict(
        even=even, odd=odd, w1k=w1k, w1v=w1v, w2p=w2p, pe=pe, w1=bf(nsa_cmp_w1[0]),
        wo_a=bf(w_out_even[0][:qw]), wo_b=bf(w_out_even[0][qw:]),
        ffn=(bf(w_ffn_gate[0]), bf(w_ffn_up[0]), bf(w_ffn_down[0])),
        wo_c=bf(w_out_odd[0]), wr_hi=wr_hi, wr_lo=wr_lo,
        exp=(bf(w_exp_gate[0]), bf(w_exp_up[0]), bf(w_exp_down[0])),
        ple_proj=bf(w_ple_proj), ple_gate=bf(w_ple_gate))


EVEN_SEGS = ("rope_dual", "plain", ("rope", "none"), ("rope", "none"), ("sigmoid",), "plain",
             "plain")
ODD_SEGS = (("rope",) * 16, ("rope",) * 8 + ("none",) * 8)


def _trunk(x, p, pos0, past, W, norms, sizes):
    b, t, d = x.shape
    n = b * t
    tm, tq, tk_sel, tk_sb, tm_moe, tq_moba, moba_bps = sizes
    norm_mix, norm_ffn, norm_ple, norm_final = norms
    pos = pos0 + jnp.arange(t, dtype=jnp.int32)
    tabs = _rope_tables(pos)
    if t < tm:
        tabs = [jnp.tile(a, (tm // t, 1)) for a in tabs]
    h = x.reshape(n, d)
    row = lambda a: a.reshape(1, d)
    b3 = lambda a: a.reshape(b, t, a.shape[-1])

    qp, qr, cmp_r, sel_r, win_r, gates, sbq, sbkv = _proj(
        h, row(norm_mix[0]), tabs, W["even"], EVEN_SEGS, tm)
    if past is None:
        n_pages = t // PAGE
        ident = jnp.arange(b * n_pages, dtype=jnp.int32).reshape(b, n_pages)
        kc, vc = _compress(cmp_r.reshape(b * n_pages, PAGE, 2 * LANES), ident,
                           W["w1k"], W["w1v"], W["w2p"], W["pe"], W["w1"], 8)
        sel_ctx, win_ctx, sb_ctx = b3(sel_r), b3(win_r), b3(sbkv)
        wpos0 = 0
        win_state = win_ctx[:, -NSA_WINDOW:]
    else:
        pt = past["page_table"]
        tail = lambda a: jnp.pad(b3(a), ((0, 0), (0, MOBA_BLOCK - t), (0, 0)))
        pool = lambda c: c[0].reshape(c.shape[1], PAGE, -1)
        kc, vc = _compress(pool(past["cache_nsa_cmp"]), pt,
                           W["w1k"], W["w1v"], W["w2p"], W["pe"], W["w1"], 8)
        sel_ctx = _gather_ctx(past["cache_nsa_sel"], pt, tail(sel_r), 8)
        sb_ctx = _gather_ctx(past["cache_sb"], pt, tail(sbkv), 8)
        state = past["state_nsa_win"][0].reshape(b, NSA_WINDOW, 2 * LANES)
        win_all = jnp.concatenate([state, b3(win_r)], axis=1)
        win_ctx = jnp.pad(win_all, ((0, 0), (0, LANES - t), (0, 0)))
        wpos0 = pos0 - NSA_WINDOW
        win_state = win_all[:, -NSA_WINDOW:]
    t_c = sel_ctx.shape[1]
    o_a = _nsa_attn(b3(qp), b3(qr), b3(gates), kc, vc, sel_ctx, win_ctx,
                    _overlap_matrix(kc.shape[1]), _expand_matrix(t_c, tk_sel),
                    tq, tk_sel, pos0, wpos0)
    o_b = _sb_attn(b3(sbq), sb_ctx, jnp.tril(jnp.ones((tk_sb, tk_sb), BF16), -1),
                   tq, tk_sb, pos0)
    h = _outproj(h, o_a.reshape(n, -1), o_b.reshape(n, -1), W["wo_a"], W["wo_b"], tm)
    h = _ffn(h, row(norm_ffn[0]), *W["ffn"], tm, W["ffn"][0].shape[1] // 2)
    h = _ple(h, p[0].reshape(n, -1), row(norm_ple[0]), row(norm_final),
             W["ple_gate"][0], W["ple_proj"][0], tm, False)

    mq, mkv = _proj(h, row(norm_mix[1]), tabs, W["odd"], ODD_SEGS, tm)
    if past is None:
        moba_ctx = b3(mkv)
    else:
        moba_ctx = _gather_ctx(past["cache_moba"], pt, tail(mkv), 4)
    if tq_moba % LANES == 0:
        o_c = _moba_attn_t(b3(mq), moba_ctx, tq_moba, pos0, moba_bps)
    else:
        o_c = _moba_attn(b3(mq), moba_ctx, tq_moba, pos0, moba_bps)
    half = o_c.shape[-1] // 2
    o_c = o_c.reshape(n, -1)
    h = _outproj(h, o_c[:, :half], o_c[:, half:], W["wo_c"][:half], W["wo_c"][half:], tm)
    h = _moe(h, row(norm_ffn[1]), W["wr_hi"], W["wr_lo"], *W["exp"], tm_moe, 512)
    y = _ple(h, p[1].reshape(n, -1), row(norm_ple[1]), row(norm_final),
             W["ple_gate"][1], W["ple_proj"][1], tm, True)

    kv5 = lambda a, heads: a.reshape(1, b, -1, 2, heads, HEAD_DIM)
    return (y.reshape(b, t, d), kv5(cmp_r, 2), kv5(sel_r, 2), kv5(win_state, 2),
            kv5(sbkv, SB_HEADS), kv5(mkv, MOBA_HEADS))


def kernel(x_prompt, x_sample, cache_nsa_cmp, cache_nsa_sel, state_nsa_win, cache_sb, cache_moba,
           page_table, p_prompt, p_sample, norm_mix, norm_ffn, norm_ple, norm_final, w_in_even,
           w_out_even, nsa_cmp_pe, nsa_cmp_w1, nsa_cmp_w2, w_ffn_gate, w_ffn_up, w_ffn_down,
           w_in_odd, w_out_odd, w_router, w_exp_gate, w_exp_up, w_exp_down, w_ple_proj,
           w_ple_gate):
    W = _prep_weights(w_in_even, w_out_even, nsa_cmp_pe, nsa_cmp_w1, nsa_cmp_w2, w_ffn_gate,
                      w_ffn_up, w_ffn_down, w_in_odd, w_out_odd, w_router, w_exp_gate, w_exp_up,
                      w_exp_down, w_ple_proj, w_ple_gate)
    norms = (norm_mix, norm_ffn, norm_ple, norm_final)
    past = dict(cache_nsa_cmp=cache_nsa_cmp, cache_nsa_sel=cache_nsa_sel,
                state_nsa_win=state_nsa_win, cache_sb=cache_sb, cache_moba=cache_moba,
                page_table=page_table)
    past_len = page_table.shape[1] * cache_sb.shape[2]
    t_dec = x_sample.shape[1]
    n_dec = x_sample.shape[0] * t_dec
    y_p, cmp_p, sel_p, win_p, sb_p, moba_p = _trunk(
        x_prompt, p_prompt, 0, None, W, norms, (256, 128, 512, 128, 1024, MOBA_BLOCK, 4))
    y_s, cmp_s, sel_s, win_s, sb_s, moba_s = _trunk(
        x_sample, p_sample, past_len, past, W, norms, (n_dec, t_dec, 256, 128, n_dec, t_dec, 8))
    return (y_p, y_s, cmp_p, cmp_s, sel_p, sel_s, win_p, win_s, sb_p, sb_s, moba_p, moba_s)
```

```python
import functools

import jax
import jax.numpy as jnp
from jax import lax
from jax.experimental import pallas as pl
from jax.experimental.pallas import tpu as pltpu

F32 = jnp.float32
BF16 = jnp.bfloat16

LANES = 128
HEAD_DIM = 64
PAGE = 128
RMS_EPS = 1e-6
ROPE_THETA = 10000.0
NEG = -1e30

NSA_HEADS = 8
NSA_GROUP = 4
NSA_STRIDE = 16
NSA_HID = 128
NSA_SEL_BLOCK = 64
NSA_PICKS = 13
NSA_WINDOW = 512
SB_HEADS = 8
MOBA_HEADS = 16
MOBA_BLOCK = 256
MOBA_TOPK = 3
N_EXPERTS = 8
SB_DEAD = -110.0
SB_RECENT_PAGES = 8

VMEM_LIMIT = 56 * 1024 * 1024


def _cparams(sem):
    return pltpu.CompilerParams(dimension_semantics=sem, vmem_limit_bytes=VMEM_LIMIT)


def _dot(a, b):
    return jnp.dot(a, b, preferred_element_type=F32)


def _dot_nt(a, b):
    return lax.dot_general(a, b, (((1,), (1,)), ((), ())), preferred_element_type=F32)


def _split(x):
    hi = x.astype(BF16)
    lo = (x - hi.astype(F32)).astype(BF16)
    return hi, lo


def _rmsnorm(x, g):
    return x * lax.rsqrt(jnp.mean(x * x, axis=-1, keepdims=True) + RMS_EPS) * g


def _sigmoid(x):
    return 1.0 / (1.0 + jnp.exp(-x))


def _rope_tile(y, cos, sa, sb):
    return y * cos + pltpu.roll(y, 32, 1) * sa + pltpu.roll(y, 96, 1) * sb


def _proj_kernel(x_ref, g_ref, cos_ref, sa_ref, sb_ref, *refs, segs):
    n_seg = len(segs)
    w_refs = refs[:n_seg]
    o_refs = list(refs[n_seg:])
    nb = _rmsnorm(x_ref[...], g_ref[...]).astype(BF16)
    cos, sa, sb = cos_ref[...], sa_ref[...], sb_ref[...]
    for w_ref, kinds in zip(w_refs, segs):
        y = _dot(nb, w_ref[...])
        if kinds == "plain":
            o_refs.pop(0)[...] = y
            continue
        if kinds == "rope_dual":
            o_refs.pop(0)[...] = y
            kinds = ("rope",) * (y.shape[1] // LANES)
        o_ref = o_refs.pop(0)
        for t, kind in enumerate(kinds):
            yt = y[:, t * LANES:(t + 1) * LANES]
            if kind == "rope":
                yt = _rope_tile(yt, cos, sa, sb)
            elif kind == "sigmoid":
                yt = _sigmoid(yt)
            o_ref[:, t * LANES:(t + 1) * LANES] = yt


def _proj(x, g, tabs, weights, segs, tm):
    n, d = x.shape
    nblk = tabs[0].shape[0] // tm
    out_shape, out_specs = [], []
    for w, kinds in zip(weights, segs):
        for _ in range(2 if kinds == "rope_dual" else 1):
            out_shape.append(jax.ShapeDtypeStruct((n, w.shape[1]), F32))
            out_specs.append(pl.BlockSpec((tm, w.shape[1]), lambda i: (i, 0)))
    tab_spec = pl.BlockSpec((tm, LANES), lambda i: (i % nblk, 0))
    return pl.pallas_call(
        functools.partial(_proj_kernel, segs=tuple(segs)),
        grid=(n // tm,),
        in_specs=[pl.BlockSpec((tm, d), lambda i: (i, 0)),
                  pl.BlockSpec((1, d), lambda i: (0, 0)),
                  tab_spec, tab_spec, tab_spec]
                 + [pl.BlockSpec(w.shape, lambda i: (0, 0)) for w in weights],
        out_specs=out_specs,
        out_shape=out_shape,
        compiler_params=_cparams(("parallel",)),
        name="norm_proj",
    )(x, g, *tabs, *weights)


def _outproj_kernel(h_ref, a_ref, b_ref, wa_ref, wb_ref, o_ref):
    o_ref[...] = (h_ref[...] + _dot(a_ref[...].astype(BF16), wa_ref[...])
                  + _dot(b_ref[...].astype(BF16), wb_ref[...]))


def _outproj(h, a, b, wa, wb, tm):
    n, d = h.shape
    row = lambda w: pl.BlockSpec((tm, w), lambda i: (i, 0))
    full = lambda w: pl.BlockSpec(w.shape, lambda i: (0, 0))
    return pl.pallas_call(
        _outproj_kernel,
        grid=(n // tm,),
        in_specs=[row(d), row(a.shape[1]), row(b.shape[1]), full(wa), full(wb)],
        out_specs=row(d),
        out_shape=jax.ShapeDtypeStruct((n, d), F32),
        compiler_params=_cparams(("parallel",)),
        name="out_proj",
    )(h, a, b, wa, wb)


def _ffn_kernel(h_ref, g_ref, wg_ref, wu_ref, wd_ref, o_ref, n_scr, acc_scr):
    f = pl.program_id(1)

    @pl.when(f == 0)
    def _():
        n_scr[...] = _rmsnorm(h_ref[...], g_ref[...]).astype(BF16)
        acc_scr[...] = jnp.zeros_like(acc_scr)

    nb = n_scr[...]
    gate = _dot(nb, wg_ref[...])
    up = _dot(nb, wu_ref[...])
    hid = (gate * _sigmoid(gate) * up).astype(BF16)
    acc_scr[...] += _dot(hid, wd_ref[...])

    @pl.when(f == pl.num_programs(1) - 1)
    def _():
        o_ref[...] = h_ref[...] + acc_scr[...]


def _ffn(h, g, wg, wu, wd, tm, tf):
    n, d = h.shape
    dff = wg.shape[1]
    return pl.pallas_call(
        _ffn_kernel,
        grid=(n // tm, dff // tf),
        in_specs=[pl.BlockSpec((tm, d), lambda i, f: (i, 0)),
                  pl.BlockSpec((1, d), lambda i, f: (0, 0)),
                  pl.BlockSpec((d, tf), lambda i, f: (0, f)),
                  pl.BlockSpec((d, tf), lambda i, f: (0, f)),
                  pl.BlockSpec((tf, d), lambda i, f: (f, 0))],
        out_specs=pl.BlockSpec((tm, d), lambda i, f: (i, 0)),
        out_shape=jax.ShapeDtypeStruct((n, d), F32),
        scratch_shapes=[pltpu.VMEM((tm, d), BF16), pltpu.VMEM((tm, d), F32)],
        compiler_params=_cparams(("parallel", "arbitrary")),
        name="swiglu_ffn",
    )(h, g, wg, wu, wd)


def _ple_kernel(h_ref, p_ref, g_ref, gf_ref, wg_ref, wp_ref, o_ref, *, final):
    h = h_ref[...]
    gate = _sigmoid(_dot(_rmsnorm(h, g_ref[...]).astype(BF16), wg_ref[...]))
    out = h + gate * _dot(p_ref[...].astype(BF16), wp_ref[...])
    if final:
        out = _rmsnorm(out, gf_ref[...])
    o_ref[...] = out


def _ple(h, p, g, gf, wg, wp, tm, final):
    n, d = h.shape
    row = lambda w: pl.BlockSpec((tm, w), lambda i: (i, 0))
    full = lambda a: pl.BlockSpec(a.shape, lambda i: (0, 0))
    return pl.pallas_call(
        functools.partial(_ple_kernel, final=final),
        grid=(n // tm,),
        in_specs=[row(d), row(p.shape[1]), full(g), full(gf), full(wg), full(wp)],
        out_specs=row(d),
        out_shape=jax.ShapeDtypeStruct((n, d), F32),
        compiler_params=_cparams(("parallel",)),
        name="ple",
    )(h, p, g, gf, wg, wp)


def _moe_kernel(h_ref, g_ref, wr_hi_ref, wr_lo_ref, wg_ref, wu_ref, wd_ref, o_ref,
                n_scr, gw_scr, acc_scr):
    e = pl.program_id(1)
    f = pl.program_id(2)
    lane = lax.broadcasted_iota(jnp.int32, (1, LANES), 1)

    @pl.when((e == 0) & (f == 0))
    def _():
        n = _rmsnorm(h_ref[...], g_ref[...])
        n_hi, n_lo = _split(n)
        n_scr[...] = n_hi
        logits = (_dot(n_hi, wr_hi_ref[...]) + _dot(n_lo, wr_hi_ref[...])
                  + _dot(n_hi, wr_lo_ref[...]))
        logits = jnp.where(lane < N_EXPERTS, logits, NEG)
        lane_f = lane.astype(F32)
        v1 = jnp.max(logits, axis=1, keepdims=True)
        i1 = jnp.min(jnp.where(logits == v1, lane_f, float(LANES)), axis=1, keepdims=True)
        rest = jnp.where(lane_f == i1, NEG, logits)
        v2 = jnp.max(rest, axis=1, keepdims=True)
        i2 = jnp.min(jnp.where(rest == v2, lane_f, float(LANES)), axis=1, keepdims=True)
        e2 = jnp.exp(v2 - v1)
        g1 = 1.0 / (1.0 + e2)
        gw_scr[...] = jnp.where(lane_f == i1, g1, jnp.where(lane_f == i2, e2 * g1, 0.0))
        acc_scr[...] = jnp.zeros_like(acc_scr)

    nb = n_scr[...]
    gate = _dot(nb, wg_ref[0])
    up = _dot(nb, wu_ref[0])
    hid = (gate * _sigmoid(gate) * up).astype(BF16)
    col = jnp.sum(jnp.where(lane == e, gw_scr[...], 0.0), axis=1, keepdims=True)
    acc_scr[...] += col * _dot(hid, wd_ref[0])

    @pl.when((e == pl.num_programs(1) - 1) & (f == pl.num_programs(2) - 1))
    def _():
        o_ref[...] = h_ref[...] + acc_scr[...]


def _moe(h, g, wr_hi, wr_lo, wg, wu, wd, tm, tf):
    n, d = h.shape
    n_e, _, dff = wg.shape
    return pl.pallas_call(
        _moe_kernel,
        grid=(n // tm, n_e, dff // tf),
        in_specs=[pl.BlockSpec((tm, d), lambda i, e, f: (i, 0)),
                  pl.BlockSpec((1, d), lambda i, e, f: (0, 0)),
                  pl.BlockSpec(wr_hi.shape, lambda i, e, f: (0, 0)),
                  pl.BlockSpec(wr_lo.shape, lambda i, e, f: (0, 0)),
                  pl.BlockSpec((1, d, tf), lambda i, e, f: (e, 0, f)),
                  pl.BlockSpec((1, d, tf), lambda i, e, f: (e, 0, f)),
                  pl.BlockSpec((1, tf, d), lambda i, e, f: (e, f, 0))],
        out_specs=pl.BlockSpec((tm, d), lambda i, e, f: (i, 0)),
        out_shape=jax.ShapeDtypeStruct((n, d), F32),
        scratch_shapes=[pltpu.VMEM((tm, d), BF16), pltpu.VMEM((tm, LANES), F32),
                        pltpu.VMEM((tm, d), F32)],
        compiler_params=_cparams(("parallel", "arbitrary", "arbitrary")),
        name="moe_ffn",
    )(h, g, wr_hi, wr_lo, wg, wu, wd)


def _gather_kernel(pt_ref, *refs, n_pg):
    del pt_ref
    page_refs, tail_ref, o_ref = refs[:n_pg], refs[n_pg], refs[n_pg + 1]
    s = pl.program_id(1)
    last = pl.num_programs(1) - 1

    n_heads = page_refs[0].shape[3]
    half = n_heads * HEAD_DIM

    @pl.when(s < last)
    def _():
        for i, p_ref in enumerate(page_refs):
            for kv in range(2):
                for pair in range(n_heads // 2):
                    tile = p_ref[0, 0, kv, 2 * pair:2 * pair + 2].reshape(LANES, PAGE)
                    c0 = kv * half + pair * LANES
                    o_ref[0, i * PAGE:(i + 1) * PAGE, c0:c0 + LANES] = tile.T

    @pl.when(s == last)
    def _():
        o_ref[0, 0:tail_ref.shape[1], :] = tail_ref[0]


def _gather_ctx(cache, page_table, tail, n_pg):
    b, n_pages = page_table.shape
    n_heads = cache.shape[4]
    w = 2 * n_heads * HEAD_DIM
    n_tail = tail.shape[1]
    steps = n_pages // n_pg
    cache = cache.transpose(0, 1, 3, 4, 5, 2)

    def page_spec(i):
        return pl.BlockSpec(
            (1, 1, 2, n_heads, HEAD_DIM, PAGE),
            lambda bi, s, pt: (0, pt[bi, jnp.minimum(s, steps - 1) * n_pg + i], 0, 0, 0, 0))

    return pl.pallas_call(
        functools.partial(_gather_kernel, n_pg=n_pg),
        grid_spec=pltpu.PrefetchScalarGridSpec(
            num_scalar_prefetch=1,
            grid=(b, steps + 1),
            in_specs=[page_spec(i) for i in range(n_pg)]
                     + [pl.BlockSpec((1, n_tail, w), lambda bi, s, pt: (bi, 0, 0))],
            out_specs=pl.BlockSpec((1, n_pg * PAGE, w), lambda bi, s, pt: (bi, s, 0)),
        ),
        out_shape=jax.ShapeDtypeStruct((b, n_pages * PAGE + n_tail, w), F32),
        compiler_params=_cparams(("parallel", "arbitrary")),
        name="paged_gather",
    )(page_table, *([cache] * n_pg), tail)


def _compress_kernel(pt_ref, *refs, n_pg):
    del pt_ref
    k_refs, v_refs = refs[:n_pg], refs[n_pg:2 * n_pg]
    (w1k_ref, w1v_ref, w2_ref, pe_ref, w1_ref, kc_ref, vc_ref,
     ak_scr, av_scr) = refs[2 * n_pg:]
    s = pl.program_id(1)
    n_ch = ak_scr.shape[0]
    per_page = PAGE // NSA_STRIDE

    for p_refs, a_scr in ((k_refs, ak_scr), (v_refs, av_scr)):
        for i, p_ref in enumerate(p_refs):
            base = pl.multiple_of((s * n_pg + i) * per_page, per_page)
            for j in range(NSA_STRIDE):
                a_scr[pl.ds(base, per_page), j * LANES:(j + 1) * LANES] = (
                    p_ref[0, pl.ds(j, per_page, stride=NSA_STRIDE), :])

    @pl.when(s == pl.num_programs(1) - 1)
    def _():
        for kv, (a_scr, w1p_ref, o_ref) in enumerate(((ak_scr, w1k_ref, kc_ref),
                                                      (av_scr, w1v_ref, vc_ref))):
            r = _dot(a_scr[...].astype(BF16), w1p_ref[...])
            bias = _dot(pe_ref[kv], w1_ref[kv])[0:1, :]
            hid = []
            for g in range(2):
                a = r[:, (2 * g) * NSA_HID:(2 * g + 1) * NSA_HID]
                b_next = pltpu.roll(r[:, (2 * g + 1) * NSA_HID:(2 * g + 2) * NSA_HID],
                                    n_ch - 1, 0)
                pre = a + b_next + bias
                hid.append(pre * _sigmoid(pre))
            hid = jnp.concatenate(hid, axis=1).astype(BF16)
            o_ref[0] = _dot(hid, w2_ref[kv])


def _compress(pool, page_table, w1k, w1v, w2, pe, w1, n_pg):
    b, n_pages = page_table.shape
    n_ch = n_pages * (PAGE // NSA_STRIDE)
    steps = n_pages // n_pg
    full = lambda a: pl.BlockSpec(a.shape, lambda bi, s, pt: (0,) * a.ndim)

    def page_spec(i, kv):
        return pl.BlockSpec((1, PAGE, LANES), lambda bi, s, pt: (pt[bi, s * n_pg + i], 0, kv))

    out_spec = pl.BlockSpec((1, n_ch, LANES), lambda bi, s, pt: (bi, 0, 0))
    return pl.pallas_call(
        functools.partial(_compress_kernel, n_pg=n_pg),
        grid_spec=pltpu.PrefetchScalarGridSpec(
            num_scalar_prefetch=1,
            grid=(b, steps),
            in_specs=[page_spec(i, kv) for kv in range(2) for i in range(n_pg)]
                     + [full(w1k), full(w1v), full(w2), full(pe), full(w1)],
            out_specs=[out_spec, out_spec],
            scratch_shapes=[pltpu.VMEM((n_ch, NSA_STRIDE * LANES), F32),
                            pltpu.VMEM((n_ch, NSA_STRIDE * LANES), F32)],
        ),
        out_shape=[jax.ShapeDtypeStruct((b, n_ch, LANES), F32)] * 2,
        compiler_params=_cparams(("parallel", "arbitrary")),
        name="nsa_compress",
    )(page_table, *([pool] * (2 * n_pg)), w1k, w1v, w2, pe, w1)


def _flash_step(state, s, valid, v):
    m, l, acc = state
    s = jnp.where(valid, s, NEG)
    m_new = jnp.maximum(m, jnp.max(s, axis=1, keepdims=True))
    alpha = jnp.exp(m - m_new)
    p = jnp.where(valid, jnp.exp(s - m_new), 0.0)
    l = alpha * l + jnp.sum(p, axis=1, keepdims=True)
    acc = alpha * acc + _dot(p.astype(BF16), v)
    return m_new, l, acc


def _flash_init(rows):
    return (jnp.full((rows, 1), NEG, F32), jnp.zeros((rows, 1), F32),
            jnp.zeros((rows, LANES), F32))


def _flash_out(state):
    _, l, acc = state
    return jnp.where(l > 0.0, acc / jnp.where(l > 0.0, l, 1.0), 0.0)


def _stack(x, times):
    return jnp.concatenate([x] * times, axis=0)


N_WIN_TILES = NSA_WINDOW // LANES + 1


def _nsa_kernel(qp_ref, qr_ref, gt_ref, kc_ref, vc_ref, sel_ref, ov_ref, ex_ref, *refs,
                tq, tk, pos0, wpos0):
    win_refs, o_ref = refs[:N_WIN_TILES], refs[N_WIN_TILES]
    i = pl.program_id(1)
    q0 = pos0 + i * tq
    scale = HEAD_DIM ** -0.5
    r4 = NSA_GROUP
    qpos = q0 + lax.broadcasted_iota(jnp.int32, (tq, 1), 0)
    qpos4 = _stack(qpos, r4)
    qblk = qpos // NSA_SEL_BLOCK
    lane = lax.broadcasted_iota(jnp.int32, (1, LANES), 1)
    lane_f = lane.astype(F32)
    n_ch = kc_ref.shape[1]
    c_end = lax.broadcasted_iota(jnp.int32, (1, n_ch), 1) * NSA_STRIDE + (2 * NSA_STRIDE - 1)
    kc = kc_ref[0].astype(BF16)
    vc = vc_ref[0].astype(BF16)
    gates = gt_ref[0]
    n_sel_tiles = (q0 + tq - 1) // tk + 1
    win_start = (q0 // LANES) * LANES - NSA_WINDOW

    q_rot, o_cmp, picked = [], [], []
    for g in range(2):
        heads = range(g * r4, (g + 1) * r4)
        q_c = jnp.concatenate([qp_ref[0, :, h * LANES:(h + 1) * LANES] for h in heads], axis=0)
        q_r = jnp.concatenate([qr_ref[0, :, h * LANES:(h + 1) * LANES] for h in heads], axis=0)
        q_c = (q_c * scale).astype(BF16)
        q_rot.append((q_r * scale).astype(BF16))

        s = _dot_nt(q_c, kc)
        valid = c_end <= qpos4
        s = jnp.where(valid, s, NEG)
        e = jnp.where(valid, jnp.exp(s - jnp.max(s, axis=1, keepdims=True)), 0.0)
        l = jnp.sum(e, axis=1, keepdims=True)
        p = jnp.where(l > 0.0, e / jnp.where(l > 0.0, l, 1.0), 0.0)
        o_cmp.append(_dot(p.astype(BF16), vc))

        p_sum = p[0:tq]
        for r in range(1, r4):
            p_sum = p_sum + p[r * tq:(r + 1) * tq]
        p_hi, p_lo = _split(p_sum)
        imp = _dot(p_hi, ov_ref[...]) + _dot(p_lo, ov_ref[...])

        cand = (lane <= qblk) & (lane != 0) & (lane != qblk) & (lane != qblk - 1)
        work = jnp.where(cand, imp, -1.0)
        pick = jnp.zeros((tq, LANES), F32)
        for _ in range(NSA_PICKS):
            top = jnp.max(work, axis=1, keepdims=True)
            first = jnp.min(jnp.where(work == top, lane_f, float(LANES)), axis=1,
                            keepdims=True)
            hit = (lane_f == first) & (top >= 0.0)
            pick = jnp.where(hit, 1.0, pick)
            work = jnp.where(hit, -1.0, work)
        picked.append(pick.astype(BF16))

    def sel_body(t, states):
        k0 = pl.multiple_of(t * tk, tk)
        k = sel_ref[0, pl.ds(k0, tk), 0:LANES].astype(BF16)
        v = sel_ref[0, pl.ds(k0, tk), LANES:2 * LANES].astype(BF16)
        kpos = k0 + lax.broadcasted_iota(jnp.int32, (1, tk), 1)
        kblk = kpos // NSA_SEL_BLOCK
        forced = (kblk == 0) | (kblk == qblk) | (kblk == qblk - 1)
        out = ()
        for g in range(2):
            chosen = _dot(picked[g], ex_ref[t]) > 0.5
            keep = jnp.where((chosen | forced) & (kpos <= qpos), 1.0, 0.0)
            out += _flash_step(states[3 * g:3 * g + 3], _dot_nt(q_rot[g], k),
                               _stack(keep, r4) > 0.5, v)
        return out

    sel_states = lax.fori_loop(0, n_sel_tiles, sel_body, _flash_init(r4 * tq) * 2)
    o_sel = [_flash_out(sel_states[0:3]), _flash_out(sel_states[3:6])]

    win_states = [_flash_init(r4 * tq), _flash_init(r4 * tq)]
    for jt, w_ref in enumerate(win_refs):
        k = w_ref[0, :, 0:LANES].astype(BF16)
        v = w_ref[0, :, LANES:2 * LANES].astype(BF16)
        wpos = win_start + jt * LANES + lane
        keep = jnp.where((wpos >= wpos0) & (wpos <= qpos) & (qpos - wpos < NSA_WINDOW),
                         1.0, 0.0)
        keep4 = _stack(keep, r4) > 0.5
        for g in range(2):
            win_states[g] = _flash_step(win_states[g], _dot_nt(q_rot[g], k), keep4, v)
    o_win = [_flash_out(win_states[0]), _flash_out(win_states[1])]

    outs = []
    for g in range(2):
        for r in range(r4):
            h = g * r4 + r
            rows = slice(r * tq, (r + 1) * tq)
            o = (gates[:, 3 * h:3 * h + 1] * o_cmp[g][rows]
                 + gates[:, 3 * h + 1:3 * h + 2] * o_sel[g][rows]
                 + gates[:, 3 * h + 2:3 * h + 3] * o_win[g][rows])
            if h % 2 != g:
                o = pltpu.roll(o, HEAD_DIM, 1)
            outs.append(o)

    for pair in range(NSA_HEADS // 2):
        o_ref[0, :, pair * LANES:(pair + 1) * LANES] = jnp.where(
            lane < HEAD_DIM, outs[2 * pair], outs[2 * pair + 1])


def _nsa_t_kernel(qp_ref, qr_ref, gt_ref, kc_ref, vc_ref, sel_ref, ovt_ref, ext_ref, *refs,
                  tq, tk, pos0, wpos0):
    win_refs, o_ref, svt_scr, vct_scr = (refs[:N_WIN_TILES], refs[N_WIN_TILES],
                                         refs[N_WIN_TILES + 1], refs[N_WIN_TILES + 2])
    i = pl.program_id(1)
    q0 = pos0 + i * tq
    scale = HEAD_DIM ** -0.5
    r4 = NSA_GROUP
    n_ch = kc_ref.shape[1]
    row = lax.broadcasted_iota(jnp.int32, (LANES, 1), 0)
    row_f = row.astype(F32)
    qpos = q0 + lax.broadcasted_iota(jnp.int32, (1, tq), 1)
    qblk = qpos // NSA_SEL_BLOCK
    n_sel_tiles = (q0 + tq - 1) // tk + 1
    win_start = (q0 // LANES) * LANES - NSA_WINDOW
    own_half = [row < HEAD_DIM, row >= HEAD_DIM]

    def lanes4(x):
        return jnp.concatenate([x] * r4, axis=1)

    def v_t(tiles):
        return jnp.concatenate([t.T for t in tiles], axis=1)

    @pl.when(i == 0)
    def _():
        def transpose_tile(t, carry):
            k0 = pl.multiple_of(t * tk, tk)
            vt = v_t([sel_ref[0, pl.ds(k0 + j * LANES, LANES), LANES:2 * LANES]
                      for j in range(tk // LANES)])
            for g in range(2):
                svt_scr[g, t] = jnp.where(own_half[g], vt, 1.0).astype(BF16)
            return carry

        lax.fori_loop(0, sel_ref.shape[1] // tk, transpose_tile, 0)
        vct = v_t([vc_ref[0, j * LANES:(j + 1) * LANES, :] for j in range(n_ch // LANES)])
        for g in range(2):
            vct_scr[g] = jnp.where(own_half[g], vct, 1.0).astype(BF16)

    kc = kc_ref[0].astype(BF16)
    c_end = (lax.broadcasted_iota(jnp.int32, (n_ch, 1), 0) * NSA_STRIDE
             + (2 * NSA_STRIDE - 1))
    cmp_bias = lanes4(jnp.where(c_end <= qpos, 0.0, NEG))

    q_rot, o_cmp, picked = [], [], []
    for g in range(2):
        heads = range(g * r4, (g + 1) * r4)
        q_c = jnp.concatenate([qp_ref[0, :, h * LANES:(h + 1) * LANES] for h in heads], axis=0)
        q_r = jnp.concatenate([qr_ref[0, :, h * LANES:(h + 1) * LANES] for h in heads], axis=0)
        q_c = (q_c * scale).astype(BF16)
        q_rot.append((q_r * scale).astype(BF16))

        s = _dot_nt(kc, q_c) + cmp_bias
        m = jnp.max(s, axis=0, keepdims=True)
        e = jnp.exp(s - m)
        p = jnp.where(m > 0.5 * NEG, e / jnp.sum(e, axis=0, keepdims=True), 0.0)
        o_cmp.append(_dot(vct_scr[g], p.astype(BF16)))

        p_sum = p[:, 0:tq]
        for r in range(1, r4):
            p_sum = p_sum + p[:, r * tq:(r + 1) * tq]
        p_hi, p_lo = _split(p_sum)
        imp = _dot(ovt_ref[...], p_hi) + _dot(ovt_ref[...], p_lo)

        cand = (row <= qblk) & (row != 0) & (row != qblk) & (row != qblk - 1)
        work = jnp.where(cand, imp, -1.0)
        pick = jnp.zeros((LANES, tq), F32)
        for _ in range(NSA_PICKS):
            top = jnp.max(work, axis=0, keepdims=True)
            first = jnp.min(jnp.where(work == top, row_f, float(LANES)), axis=0, keepdims=True)
            hit = (row_f == first) & (top >= 0.0)
            pick = jnp.where(hit, 1.0, pick)
            work = jnp.where(hit, -1.0, work)
        picked.append(pick.astype(BF16))

    def flash_t(state, s, vt):
        m, acc = state
        m_new = jnp.maximum(m, jnp.max(s, axis=0, keepdims=True))
        p = jnp.exp(s - m_new).astype(BF16)
        return m_new, jnp.exp(m - m_new) * acc + _dot(vt, p)

    def init_t():
        return (jnp.full((1, r4 * tq), NEG, F32), jnp.zeros((LANES, r4 * tq), F32))

    def sel_body(t, states):
        k0 = pl.multiple_of(t * tk, tk)
        k = sel_ref[0, pl.ds(k0, tk), 0:LANES].astype(BF16)
        kpos = k0 + lax.broadcasted_iota(jnp.int32, (tk, 1), 0)
        kblk = kpos // NSA_SEL_BLOCK
        forced = (kblk == 0) | (kblk == qblk) | (kblk == qblk - 1)
        causal = kpos <= qpos
        out = ()
        for g in range(2):
            chosen = _dot(ext_ref[t], picked[g]) > 0.5
            bias = lanes4(jnp.where((chosen | forced) & causal, 0.0, NEG))
            out += flash_t(states[2 * g:2 * g + 2], _dot_nt(k, q_rot[g]) + bias, svt_scr[g, t])
        return out

    sel_states = lax.fori_loop(0, n_sel_tiles, sel_body, init_t() * 2)

    win_states = [init_t(), init_t()]
    for jt, w_ref in enumerate(win_refs):
        k = w_ref[0, :, 0:LANES].astype(BF16)
        vt = w_ref[0, :, LANES:2 * LANES].T
        wpos = win_start + jt * LANES + row
        keep = (wpos >= wpos0) & (wpos <= qpos) & (qpos - wpos < NSA_WINDOW)
        bias = lanes4(jnp.where(keep, 0.0, NEG))
        for g in range(2):
            win_states[g] = flash_t(win_states[g], _dot_nt(k, q_rot[g]) + bias,
                                    jnp.where(own_half[g], vt, 1.0).astype(BF16))

    gates_t = gt_ref[0].T
    for g in range(2):
        ones_row = HEAD_DIM * (1 - g)
        o_sel = sel_states[2 * g + 1] / sel_states[2 * g + 1][ones_row:ones_row + 1, :]
        o_win = win_states[g][1] / win_states[g][1][ones_row:ones_row + 1, :]
        mixed = []
        for r in range(r4):
            h = g * r4 + r
            cols = slice(r * tq, (r + 1) * tq)
            o = (gates_t[3 * h:3 * h + 1, :] * o_cmp[g][:, cols]
                 + gates_t[3 * h + 1:3 * h + 2, :] * o_sel[:, cols]
                 + gates_t[3 * h + 2:3 * h + 3, :] * o_win[:, cols])
            mixed.append(o[g * HEAD_DIM:(g + 1) * HEAD_DIM, :])
        for pr in range(r4 // 2):
            pair = g * (r4 // 2) + pr
            tile = jnp.concatenate([mixed[2 * pr], mixed[2 * pr + 1]], axis=0)
            o_ref[0, :, pair * LANES:(pair + 1) * LANES] = tile.T


def _nsa_attn(qp, qr, gates, kc, vc, sel_ctx, win_ctx, ov, ex, tq, tk, pos0, wpos0):
    b, t_q, _ = qp.shape
    t_c = sel_ctx.shape[1]
    n_win_tiles_total = win_ctx.shape[1] // LANES
    tile0 = pos0 // LANES - NSA_WINDOW // LANES - wpos0 // LANES
    transposed = tq % LANES == 0
    if transposed:
        body, ov, ex = _nsa_t_kernel, ov.T, ex.transpose(0, 2, 1)
        scratch = [pltpu.VMEM((2, t_c // tk, LANES, tk), BF16),
                   pltpu.VMEM((2, LANES, kc.shape[1]), BF16)]
    else:
        body, scratch = _nsa_kernel, []

    def win_spec(jt):
        def idx(bi, i):
            t = tile0 + (i * tq) // LANES + jt
            return (bi, jnp.clip(t, 0, n_win_tiles_total - 1), 0)
        return pl.BlockSpec((1, LANES, 2 * LANES), idx)

    qspec = pl.BlockSpec((1, tq, NSA_HEADS * LANES), lambda bi, i: (bi, i, 0))
    return pl.pallas_call(
        functools.partial(body, tq=tq, tk=tk, pos0=pos0, wpos0=wpos0),
        grid=(b, t_q // tq),
        in_specs=[qspec, qspec,
                  pl.BlockSpec((1, tq, LANES), lambda bi, i: (bi, i, 0)),
                  pl.BlockSpec((1,) + kc.shape[1:], lambda bi, i: (bi, 0, 0)),
                  pl.BlockSpec((1,) + vc.shape[1:], lambda bi, i: (bi, 0, 0)),
                  pl.BlockSpec((1, t_c, 2 * LANES), lambda bi, i: (bi, 0, 0)),
                  pl.BlockSpec(ov.shape, lambda bi, i: (0, 0)),
                  pl.BlockSpec(ex.shape, lambda bi, i: (0, 0, 0))]
                 + [win_spec(jt) for jt in range(N_WIN_TILES)],
        out_specs=pl.BlockSpec((1, tq, NSA_HEADS * HEAD_DIM), lambda bi, i: (bi, i, 0)),
        out_shape=jax.ShapeDtypeStruct((b, t_q, NSA_HEADS * HEAD_DIM), F32),
        scratch_shapes=scratch,
        compiler_params=_cparams(("parallel", "arbitrary")),
        name="nsa_attn_t" if transposed else "nsa_attn",
    )(qp, qr, gates, kc, vc, sel_ctx, ov, ex, *([win_ctx] * N_WIN_TILES))


def _sb_kernel(q_ref, k_ref, v_ref, tri_ref, o_ref, alive_ref, *, tq, tk, pos0, kpos0):
    i = pl.program_id(2)
    q0 = pos0 + i * tq
    scale = HEAD_DIM ** -0.5
    qpos = q0 + lax.broadcasted_iota(jnp.int32, (tq, 1), 0)
    lane = lax.broadcasted_iota(jnp.int32, (1, LANES), 1)
    n_tiles = (q0 + tq - 2 - kpos0) // tk + 1
    tri = tri_ref[...]
    qs = [(q_ref[0, :, hh * LANES:(hh + 1) * LANES] * scale).astype(BF16) for hh in range(2)]

    def tile(q, k, v, valid, acc, run):
        z = _dot_nt(q, k)
        soft = jnp.log(1.0 + jnp.exp(-jnp.abs(z)))
        log_break = jnp.minimum(z, 0.0) - soft
        log_stay = jnp.where(valid, log_break - z, 0.0)
        s_hi, s_lo = _split(log_stay)
        after = _dot(s_hi, tri) + _dot(s_lo, tri)
        a = jnp.where(valid, jnp.exp(log_break + after + run), 0.0)
        return acc + _dot(a.astype(BF16), v), run + jnp.sum(log_stay, axis=1, keepdims=True)

    def cond(carry):
        return (carry[0] < n_tiles) & (carry[1] > 0)

    def body(carry):
        step, _, acc0, run0, acc1, run1 = carry
        k0 = pl.multiple_of((n_tiles - 1 - step) * tk, tk)
        k = k_ref[0, pl.ds(k0, tk), :].astype(BF16)
        v = v_ref[0, pl.ds(k0, tk), :].astype(BF16)
        valid = (kpos0 + k0 + lax.broadcasted_iota(jnp.int32, (1, tk), 1)) < qpos
        acc0, run0 = tile(qs[0], k, v, valid, acc0, run0)
        acc1, run1 = tile(qs[1], k, v, valid, acc1, run1)
        live = jnp.maximum(jnp.max(run0), jnp.max(run1)) > SB_DEAD
        return step + 1, live.astype(jnp.int32), acc0, run0, acc1, run1

    zero_acc = jnp.zeros((tq, LANES), F32)
    zero_run = jnp.zeros((tq, 1), F32)
    _, live, acc0, _, acc1, _ = lax.while_loop(
        cond, body, (jnp.int32(0), jnp.int32(1), zero_acc, zero_run, zero_acc, zero_run))
    o_ref[0] = jnp.where(lane < HEAD_DIM, acc0, acc1)
    alive_ref[...] = jnp.zeros(alive_ref.shape, F32) + live.astype(F32)


def _sb_attn(q, kv, tri, tq, tk, pos0, kpos0=0):
    b, t_q, _ = q.shape
    t_c = kv.shape[1]
    n_pairs = SB_HEADS // 2
    n_q = t_q // tq
    out, alive = pl.pallas_call(
        functools.partial(_sb_kernel, tq=tq, tk=tk, pos0=pos0, kpos0=kpos0),
        grid=(b, n_pairs, n_q),
        in_specs=[pl.BlockSpec((1, tq, 2 * LANES), lambda bi, pr, i: (bi, i, pr)),
                  pl.BlockSpec((1, t_c, LANES), lambda bi, pr, i: (bi, 0, pr)),
                  pl.BlockSpec((1, t_c, LANES), lambda bi, pr, i: (bi, 0, n_pairs + pr)),
                  pl.BlockSpec(tri.shape, lambda bi, pr, i: (0, 0))],
        out_specs=[pl.BlockSpec((1, tq, LANES), lambda bi, pr, i: (bi, i, pr)),
                   pl.BlockSpec((1, 1, 1, 8, LANES), lambda bi, pr, i: (bi, pr, i, 0, 0))],
        out_shape=[jax.ShapeDtypeStruct((b, t_q, SB_HEADS * HEAD_DIM), F32),
                   jax.ShapeDtypeStruct((b, n_pairs, n_q, 8, LANES), F32)],
        compiler_params=_cparams(("parallel", "parallel", "arbitrary")),
        name="sb_attn",
    )(q, kv, kv, tri)
    return out, jnp.max(alive)


def _moba_kernel(q_ref, k_ref, v_ref, o_ref, km_scr, *, tq, pos0, n_blocks, bps):
    i = pl.program_id(2)
    blk = MOBA_BLOCK
    q0 = pos0 + i * tq
    own = q0 // blk
    scale = HEAD_DIM ** -0.5
    qpos = q0 + lax.broadcasted_iota(jnp.int32, (tq, 1), 0)
    lane = lax.broadcasted_iota(jnp.int32, (1, LANES), 1)
    lane_f = lane.astype(F32)

    @pl.when(i == 0)
    def _():
        km_scr[...] = jnp.zeros_like(km_scr)
        for n in range(n_blocks):
            km_scr[n:n + 1, :] = jnp.sum(k_ref[0, n * blk:(n + 1) * blk, :], axis=0,
                                         keepdims=True) * (1.0 / blk)

    km_hi, km_lo = _split(km_scr[...])
    q_s, picked, states = [], [], []
    own0 = pl.multiple_of(own * blk, blk)
    k_own = k_ref[0, pl.ds(own0, blk), :].astype(BF16)
    v_own = v_ref[0, pl.ds(own0, blk), :].astype(BF16)
    causal = (own0 + lax.broadcasted_iota(jnp.int32, (1, blk), 1)) <= qpos
    for hh in range(2):
        q = q_ref[0, :, hh * LANES:(hh + 1) * LANES]
        q_hi, q_lo = _split(q)
        gate = _dot_nt(q_hi, km_hi) + _dot_nt(q_lo, km_hi) + _dot_nt(q_hi, km_lo)
        work = jnp.where(lane < own, gate, NEG)
        pick = jnp.zeros((tq, LANES), F32)
        for _ in range(MOBA_TOPK):
            top = jnp.max(work, axis=1, keepdims=True)
            first = jnp.min(jnp.where(work == top, lane_f, float(LANES)), axis=1,
                            keepdims=True)
            hit = (lane_f == first) & (top > 0.5 * NEG)
            pick = jnp.where(hit, 1.0, pick)
            work = jnp.where(hit, NEG, work)
        picked.append(pick)
        q_s.append((q * scale).astype(BF16))
        states.append(_flash_step(_flash_init(tq), _dot_nt(q_s[hh], k_own), causal, v_own))

    def body(t, carry):
        k0 = pl.multiple_of(t * (bps * blk), bps * blk)
        k = k_ref[0, pl.ds(k0, bps * blk), :].astype(BF16)
        v = v_ref[0, pl.ds(k0, bps * blk), :].astype(BF16)
        out = []
        for hh in range(2):
            m, l, acc = carry[3 * hh:3 * hh + 3]
            s = _dot_nt(q_s[hh], k)
            rows = [jnp.sum(jnp.where(lane == t * bps + j, picked[hh], 0.0), axis=1,
                            keepdims=True) > 0.5 for j in range(bps)]
            parts = [s[:, j * blk:(j + 1) * blk] for j in range(bps)]
            m_new = m
            for j in range(bps):
                m_new = jnp.maximum(m_new, jnp.where(
                    rows[j], jnp.max(parts[j], axis=1, keepdims=True), NEG))
            alpha = jnp.exp(m - m_new)
            p = [jnp.exp(parts[j] - jnp.where(rows[j], m_new, -NEG)) for j in range(bps)]
            p = p[0] if bps == 1 else jnp.concatenate(p, axis=1)
            l = alpha * l + jnp.sum(p, axis=1, keepdims=True)
            acc = alpha * acc + _dot(p.astype(BF16), v)
            out += [m_new, l, acc]
        return tuple(out)

    n_steps = (own + bps - 1) // bps
    final = lax.fori_loop(0, n_steps, body, tuple(states[0]) + tuple(states[1]))
    o_ref[0] = jnp.where(lane < HEAD_DIM, _flash_out(final[0:3]), _flash_out(final[3:6]))


def _moba_t_kernel(q_ref, k_ref, v_ref, o_ref, km_scr, vt_scr, pk_scr, *, tq, pos0, n_blocks,
                   bps):
    i = pl.program_id(2)
    blk = MOBA_BLOCK
    sub = blk // LANES
    q0 = pos0 + i * tq
    own = q0 // blk
    scale = HEAD_DIM ** -0.5
    row = lax.broadcasted_iota(jnp.int32, (LANES, 1), 0)
    row_f = row.astype(F32)
    qpos = q0 + lax.broadcasted_iota(jnp.int32, (1, tq), 1)

    @pl.when(i == 0)
    def _():
        km_scr[...] = jnp.zeros_like(km_scr)
        for n in range(n_blocks):
            km_scr[n:n + 1, :] = jnp.sum(k_ref[0, n * blk:(n + 1) * blk, :], axis=0,
                                         keepdims=True) * (1.0 / blk)

        def transpose_block(n, carry):
            n0 = pl.multiple_of(n * blk, blk)
            vt = jnp.concatenate(
                [v_ref[0, pl.ds(n0 + j * LANES, LANES), :].T for j in range(sub)], axis=1)
            vt_scr[0, n] = jnp.where(row < HEAD_DIM, vt, 1.0).astype(BF16)
            vt_scr[1, n] = jnp.where(row >= HEAD_DIM, vt, 1.0).astype(BF16)
            return carry

        lax.fori_loop(0, n_blocks, transpose_block, 0)

    ones_rows = 16
    v_rows = [(0, HEAD_DIM + ones_rows), (HEAD_DIM - ones_rows, 2 * HEAD_DIM)]

    def weighted_values(hh, t, p):
        lo, hi = v_rows[hh]
        return _dot(vt_scr[hh, t, lo:hi, :], p)

    km_hi, km_lo = _split(km_scr[...])
    own0 = pl.multiple_of(own * blk, blk)
    k_own = k_ref[0, pl.ds(own0, blk), :].astype(BF16)
    causal = (own0 + lax.broadcasted_iota(jnp.int32, (blk, 1), 0)) <= qpos
    q_s, states = [], []
    for hh in range(2):
        q = q_ref[0, :, hh * LANES:(hh + 1) * LANES]
        q_hi, q_lo = _split(q)
        gate = _dot_nt(km_hi, q_hi) + _dot_nt(km_hi, q_lo) + _dot_nt(km_lo, q_hi)
        work = jnp.where(row < own, gate, NEG)
        pick = jnp.zeros((LANES, tq), F32)
        for _ in range(MOBA_TOPK):
            top = jnp.max(work, axis=0, keepdims=True)
            first = jnp.min(jnp.where(work == top, row_f, float(LANES)), axis=0, keepdims=True)
            hit = (row_f == first) & (top > 0.5 * NEG)
            pick = jnp.where(hit, 1.0, pick)
            work = jnp.where(hit, NEG, work)
        pk_scr[hh] = pick
        q_s.append((q * scale).astype(BF16))
        s = jnp.where(causal, _dot_nt(k_own, q_s[hh]), NEG)
        m = jnp.max(s, axis=0, keepdims=True)
        p = jnp.where(causal, jnp.exp(s - m), 0.0).astype(BF16)
        states += [m, weighted_values(hh, own, p)]

    def body(t, carry):
        k0 = pl.multiple_of(t * (bps * blk), bps * blk)
        k = k_ref[0, pl.ds(k0, bps * blk), :].astype(BF16)
        out = []
        for hh in range(2):
            m, acc = carry[2 * hh:2 * hh + 2]
            s = _dot_nt(k, q_s[hh])
            parts = [s[j * blk:(j + 1) * blk] for j in range(bps)]
            chosen = [pk_scr[hh, pl.ds(t * bps + j, 1), :] > 0.5 for j in range(bps)]
            m_new = m
            for j in range(bps):
                m_new = jnp.maximum(m_new, jnp.where(
                    chosen[j], jnp.max(parts[j], axis=0, keepdims=True), NEG))
            acc = jnp.exp(m - m_new) * acc
            for j in range(bps):
                p = jnp.exp(parts[j] - jnp.where(chosen[j], m_new, -NEG)).astype(BF16)
                acc = acc + weighted_values(hh, t * bps + j, p)
            out += [m_new, acc]
        return tuple(out)

    _, acc0, _, acc1 = lax.fori_loop(0, (own + bps - 1) // bps, body, tuple(states))
    o_t = jnp.concatenate(
        [acc0[0:HEAD_DIM] / acc0[HEAD_DIM:HEAD_DIM + 1, :],
         acc1[ones_rows:ones_rows + HEAD_DIM] / acc1[0:1, :]], axis=0)
    o_ref[0] = o_t.T


def _moba_attn_t(q, kv, tq, pos0, bps):
    b, t_q, _ = q.shape
    t_c = kv.shape[1]
    n_pairs = MOBA_HEADS // 2
    n_blocks = t_c // MOBA_BLOCK
    assert MOBA_BLOCK % tq == 0 and pos0 % tq == 0 and tq % LANES == 0
    assert n_blocks % bps == 0
    return pl.pallas_call(
        functools.partial(_moba_t_kernel, tq=tq, pos0=pos0, n_blocks=n_blocks, bps=bps),
        grid=(b, n_pairs, t_q // tq),
        in_specs=[pl.BlockSpec((1, tq, 2 * LANES), lambda bi, pr, i: (bi, i, pr)),
                  pl.BlockSpec((1, t_c, LANES), lambda bi, pr, i: (bi, 0, pr)),
                  pl.BlockSpec((1, t_c, LANES), lambda bi, pr, i: (bi, 0, n_pairs + pr))],
        out_specs=pl.BlockSpec((1, tq, LANES), lambda bi, pr, i: (bi, i, pr)),
        out_shape=jax.ShapeDtypeStruct((b, t_q, MOBA_HEADS * HEAD_DIM), F32),
        scratch_shapes=[pltpu.VMEM((LANES, LANES), F32),
                        pltpu.VMEM((2, n_blocks, LANES, MOBA_BLOCK), BF16),
                        pltpu.VMEM((2, LANES, tq), F32)],
        compiler_params=_cparams(("parallel", "parallel", "arbitrary")),
        name="moba_attn_t",
    )(q, kv, kv)


def _moba_attn(q, kv, tq, pos0, bps):
    b, t_q, _ = q.shape
    t_c = kv.shape[1]
    n_pairs = MOBA_HEADS // 2
    assert MOBA_BLOCK % tq == 0 and pos0 % tq == 0 and (pos0 // MOBA_BLOCK) % bps == 0
    return pl.pallas_call(
        functools.partial(_moba_kernel, tq=tq, pos0=pos0, n_blocks=t_c // MOBA_BLOCK, bps=bps),
        grid=(b, n_pairs, t_q // tq),
        in_specs=[pl.BlockSpec((1, tq, 2 * LANES), lambda bi, pr, i: (bi, i, pr)),
                  pl.BlockSpec((1, t_c, LANES), lambda bi, pr, i: (bi, 0, pr)),
                  pl.BlockSpec((1, t_c, LANES), lambda bi, pr, i: (bi, 0, n_pairs + pr))],
        out_specs=pl.BlockSpec((1, tq, LANES), lambda bi, pr, i: (bi, i, pr)),
        out_shape=jax.ShapeDtypeStruct((b, t_q, MOBA_HEADS * HEAD_DIM), F32),
        scratch_shapes=[pltpu.VMEM((LANES, LANES), F32)],
        compiler_params=_cparams(("parallel", "parallel", "arbitrary")),
        name="moba_attn",
    )(q, kv, kv)


def _pad_heads(w, halves):
    d = w.shape[0]
    n_h = len(halves)
    onehot = jax.nn.one_hot(jnp.asarray(halves), 2, dtype=w.dtype)
    return jnp.einsum("dhe,hs->dhse", w.reshape(d, n_h, HEAD_DIM), onehot).reshape(d, n_h * LANES)


def _rope_tables(pos):
    half = HEAD_DIM // 2
    inv = 1.0 / (ROPE_THETA ** (jnp.arange(half, dtype=F32) / half))
    ang = pos.astype(F32)[:, None] * inv[None, :]
    cos = jnp.tile(jnp.cos(ang), (1, LANES // half))
    sin = jnp.tile(jnp.sin(ang), (1, LANES // half))
    upper = (jnp.arange(LANES) % HEAD_DIM) >= half
    return cos, jnp.where(upper, sin, 0.0), jnp.where(upper, 0.0, -sin)


def _overlap_matrix(n_ch):
    c = jnp.arange(n_ch)[:, None]
    n = jnp.arange(LANES)[None, :]
    return ((c >= 4 * n - 1) & (c <= 4 * n + 3)).astype(BF16)


def _expand_matrix(t_c, tk):
    blk = (jnp.arange(t_c) // NSA_SEL_BLOCK).reshape(t_c // tk, 1, tk)
    return (jnp.arange(LANES)[None, :, None] == blk).astype(BF16)


def _compress_weights(w1, w2):
    half = NSA_STRIDE * HEAD_DIM
    out = []
    for kv in range(2):
        wa = w1[kv, :half].reshape(NSA_STRIDE, HEAD_DIM, NSA_HID)
        wb = w1[kv, half:].reshape(NSA_STRIDE, HEAD_DIM, NSA_HID)
        ab = jnp.concatenate([wa, wb], axis=2)
        z = jnp.zeros_like(ab)
        g0 = jnp.concatenate([ab, z], axis=1)
        g1 = jnp.concatenate([z, ab], axis=1)
        out.append(jnp.concatenate([g0, g1], axis=2).reshape(NSA_STRIDE * LANES, 4 * NSA_HID))
    z2 = jnp.zeros_like(w2)
    w2p = jnp.concatenate([jnp.concatenate([w2, z2], axis=2),
                           jnp.concatenate([z2, w2], axis=2)], axis=1)
    return out[0].astype(BF16), out[1].astype(BF16), w2p.astype(BF16)


def _prep_weights(w_in_even, w_out_even, nsa_cmp_pe, nsa_cmp_w1, nsa_cmp_w2, w_ffn_gate,
                  w_ffn_up, w_ffn_down, w_in_odd, w_out_odd, w_router, w_exp_gate, w_exp_up,
                  w_exp_down, w_ple_proj, w_ple_gate):
    bf = lambda a: a.astype(BF16)
    we = w_in_even[0]
    qw = NSA_HEADS * HEAD_DIM
    kvw = 2 * LANES
    c0 = qw + 3 * kvw
    n_gate = 3 * NSA_HEADS
    s0 = c0 + n_gate
    sbw = SB_HEADS * HEAD_DIM
    even = [
        bf(_pad_heads(we[:, :qw], [h // NSA_GROUP for h in range(NSA_HEADS)])),
        bf(we[:, qw:qw + kvw]),
        bf(we[:, qw + kvw:qw + 2 * kvw]),
        bf(we[:, qw + 2 * kvw:c0]),
        bf(jnp.pad(we[:, c0:s0], ((0, 0), (0, LANES - n_gate)))),
        bf(_pad_heads(we[:, s0:s0 + sbw], [h % 2 for h in range(SB_HEADS)])),
        bf(we[:, s0 + sbw:]),
    ]
    wo = w_in_odd[0]
    mw = MOBA_HEADS * HEAD_DIM
    odd = [bf(_pad_heads(wo[:, :mw], [h % 2 for h in range(MOBA_HEADS)])), bf(wo[:, mw:])]
    w1k, w1v, w2p = _compress_weights(nsa_cmp_w1[0], nsa_cmp_w2[0])
    pe = bf(jnp.broadcast_to(nsa_cmp_pe[0].reshape(2, 1, -1), (2, 8, 2 * NSA_STRIDE * HEAD_DIM)))
    wr = jnp.pad(w_router[0], ((0, 0), (0, LANES - N_EXPERTS)))
    wr_hi, wr_lo = _split(wr)
    return dict(
        even=even, odd=odd, w1k=w1k, w1v=w1v, w2p=w2p, pe=pe, w1=bf(nsa_cmp_w1[0]),
        wo_a=bf(w_out_even[0][:qw]), wo_b=bf(w_out_even[0][qw:]),
        ffn=(bf(w_ffn_gate[0]), bf(w_ffn_up[0]), bf(w_ffn_down[0])),
        wo_c=bf(w_out_odd[0]), wr_hi=wr_hi, wr_lo=wr_lo,
        exp=(bf(w_exp_gate[0]), bf(w_exp_up[0]), bf(w_exp_down[0])),
        ple_proj=bf(w_ple_proj), ple_gate=bf(w_ple_gate))


EVEN_SEGS = ("rope_dual", "plain", ("rope", "none"), ("rope", "none"), ("sigmoid",), "plain",
             "plain")
ODD_SEGS = (("rope",) * 16, ("rope",) * 8 + ("none",) * 8)


def _trunk(x, p, pos0, past, W, norms, sizes):
    b, t, d = x.shape
    n = b * t
    tm, tq, tk_sel, tk_sb, tm_moe, tq_moba, moba_bps = sizes
    norm_mix, norm_ffn, norm_ple, norm_final = norms
    pos = pos0 + jnp.arange(t, dtype=jnp.int32)
    tabs = _rope_tables(pos)
    if t < tm:
        tabs = [jnp.tile(a, (tm // t, 1)) for a in tabs]
    h = x.reshape(n, d)
    row = lambda a: a.reshape(1, d)
    b3 = lambda a: a.reshape(b, t, a.shape[-1])

    qp, qr, cmp_r, sel_r, win_r, gates, sbq, sbkv = _proj(
        h, row(norm_mix[0]), tabs, W["even"], EVEN_SEGS, tm)
    if past is None:
        n_pages = t // PAGE
        ident = jnp.arange(b * n_pages, dtype=jnp.int32).reshape(b, n_pages)
        kc, vc = _compress(cmp_r.reshape(b * n_pages, PAGE, 2 * LANES), ident,
                           W["w1k"], W["w1v"], W["w2p"], W["pe"], W["w1"], 8)
        sel_ctx, win_ctx, sb_ctx = b3(sel_r), b3(win_r), b3(sbkv)
        wpos0 = 0
        win_state = win_ctx[:, -NSA_WINDOW:]
    else:
        pt = past["page_table"]
        tail = lambda a: jnp.pad(b3(a), ((0, 0), (0, MOBA_BLOCK - t), (0, 0)))
        pool = lambda c: c[0].reshape(c.shape[1], PAGE, -1)
        kc, vc = _compress(pool(past["cache_nsa_cmp"]), pt,
                           W["w1k"], W["w1v"], W["w2p"], W["pe"], W["w1"], 8)
        sel_ctx = _gather_ctx(past["cache_nsa_sel"], pt, tail(sel_r), 8)
        sb_ctx = None
        state = past["state_nsa_win"][0].reshape(b, NSA_WINDOW, 2 * LANES)
        win_all = jnp.concatenate([state, b3(win_r)], axis=1)
        win_ctx = jnp.pad(win_all, ((0, 0), (0, LANES - t), (0, 0)))
        wpos0 = pos0 - NSA_WINDOW
        win_state = win_all[:, -NSA_WINDOW:]
    t_c = sel_ctx.shape[1]
    o_a = _nsa_attn(b3(qp), b3(qr), b3(gates), kc, vc, sel_ctx, win_ctx,
                    _overlap_matrix(kc.shape[1]), _expand_matrix(t_c, tk_sel),
                    tq, tk_sel, pos0, wpos0)
    tri = jnp.tril(jnp.ones((tk_sb, tk_sb), BF16), -1)
    if past is None:
        o_b, _ = _sb_attn(b3(sbq), sb_ctx, tri, tq, tk_sb, pos0)
    else:
        n_recent = min(SB_RECENT_PAGES, pt.shape[1])
        recent = _gather_ctx(past["cache_sb"], pt[:, -n_recent:], tail(sbkv), n_recent)
        o_recent, alive = _sb_attn(b3(sbq), recent, tri, tq, tk_sb, pos0,
                                   kpos0=pos0 - n_recent * PAGE)

        def whole_cache():
            ctx = _gather_ctx(past["cache_sb"], pt, tail(sbkv), 8)
            return _sb_attn(b3(sbq), ctx, tri, tq, tk_sb, pos0)[0]

        o_b = lax.cond(alive > 0.0, whole_cache, lambda: o_recent)
    h = _outproj(h, o_a.reshape(n, -1), o_b.reshape(n, -1), W["wo_a"], W["wo_b"], tm)
    h = _ffn(h, row(norm_ffn[0]), *W["ffn"], tm, W["ffn"][0].shape[1] // 2)
    h = _ple(h, p[0].reshape(n, -1), row(norm_ple[0]), row(norm_final),
             W["ple_gate"][0], W["ple_proj"][0], tm, False)

    mq, mkv = _proj(h, row(norm_mix[1]), tabs, W["odd"], ODD_SEGS, tm)
    if past is None:
        moba_ctx = b3(mkv)
    else:
        moba_ctx = _gather_ctx(past["cache_moba"], pt, tail(mkv), 4)
    if tq_moba % LANES == 0:
        o_c = _moba_attn_t(b3(mq), moba_ctx, tq_moba, pos0, moba_bps)
    else:
        o_c = _moba_attn(b3(mq), moba_ctx, tq_moba, pos0, moba_bps)
    half = o_c.shape[-1] // 2
    o_c = o_c.reshape(n, -1)
    h = _outproj(h, o_c[:, :half], o_c[:, half:], W["wo_c"][:half], W["wo_c"][half:], tm)
    h = _moe(h, row(norm_ffn[1]), W["wr_hi"], W["wr_lo"], *W["exp"], tm_moe, 512)
    y = _ple(h, p[1].reshape(n, -1), row(norm_ple[1]), row(norm_final),
             W["ple_gate"][1], W["ple_proj"][1], tm, True)

    kv5 = lambda a, heads: a.reshape(1, b, -1, 2, heads, HEAD_DIM)
    return (y.reshape(b, t, d), kv5(cmp_r, 2), kv5(sel_r, 2), kv5(win_state, 2),
            kv5(sbkv, SB_HEADS), kv5(mkv, MOBA_HEADS))


def kernel(x_prompt, x_sample, cache_nsa_cmp, cache_nsa_sel, state_nsa_win, cache_sb, cache_moba,
           page_table, p_prompt, p_sample, norm_mix, norm_ffn, norm_ple, norm_final, w_in_even,
           w_out_even, nsa_cmp_pe, nsa_cmp_w1, nsa_cmp_w2, w_ffn_gate, w_ffn_up, w_ffn_down,
           w_in_odd, w_out_odd, w_router, w_exp_gate, w_exp_up, w_exp_down, w_ple_proj,
           w_ple_gate):
    W = _prep_weights(w_in_even, w_out_even, nsa_cmp_pe, nsa_cmp_w1, nsa_cmp_w2, w_ffn_gate,
                      w_ffn_up, w_ffn_down, w_in_odd, w_out_odd, w_router, w_exp_gate, w_exp_up,
                      w_exp_down, w_ple_proj, w_ple_gate)
    norms = (norm_mix, norm_ffn, norm_ple, norm_final)
    past = dict(cache_nsa_cmp=cache_nsa_cmp, cache_nsa_sel=cache_nsa_sel,
                state_nsa_win=state_nsa_win, cache_sb=cache_sb, cache_moba=cache_moba,
                page_table=page_table)
    past_len = page_table.shape[1] * cache_sb.shape[2]
    t_dec = x_sample.shape[1]
    n_dec = x_sample.shape[0] * t_dec
    y_p, cmp_p, sel_p, win_p, sb_p, moba_p = _trunk(
        x_prompt, p_prompt, 0, None, W, norms, (256, 128, 512, 128, 1024, MOBA_BLOCK, 4))
    y_s, cmp_s, sel_s, win_s, sb_s, moba_s = _trunk(
        x_sample, p_sample, past_len, past, W, norms, (n_dec, t_dec, 256, 128, n_dec, t_dec, 8))
    return (y_p, y_s, cmp_p, cmp_s, sel_p, sel_s, win_p, win_s, sb_p, sb_s, moba_p, moba_s)
```

```python
import functools

import jax
import jax.numpy as jnp
from jax import lax
from jax.experimental import pallas as pl
from jax.experimental.pallas import tpu as pltpu

F32 = jnp.float32
BF16 = jnp.bfloat16

LANES = 128
HEAD_DIM = 64
PAGE = 128
RMS_EPS = 1e-6
ROPE_THETA = 10000.0
NEG = -1e30

NSA_HEADS = 8
NSA_GROUP = 4
NSA_STRIDE = 16
NSA_HID = 128
NSA_SEL_BLOCK = 64
NSA_PICKS = 13
NSA_WINDOW = 512
SB_HEADS = 8
MOBA_HEADS = 16
MOBA_BLOCK = 256
MOBA_TOPK = 3
N_EXPERTS = 8
SB_DEAD = -110.0
SB_RECENT_PAGES = 8

VMEM_LIMIT = 56 * 1024 * 1024


def _cparams(sem):
    return pltpu.CompilerParams(dimension_semantics=sem, vmem_limit_bytes=VMEM_LIMIT)


def _dot(a, b):
    return jnp.dot(a, b, preferred_element_type=F32)


def _dot_nt(a, b):
    return lax.dot_general(a, b, (((1,), (1,)), ((), ())), preferred_element_type=F32)


def _split(x):
    hi = x.astype(BF16)
    lo = (x - hi.astype(F32)).astype(BF16)
    return hi, lo


def _rmsnorm(x, g):
    return x * lax.rsqrt(jnp.mean(x * x, axis=-1, keepdims=True) + RMS_EPS) * g


def _sigmoid(x):
    return 1.0 / (1.0 + jnp.exp(-x))


def _rope_tile(y, cos, sa, sb):
    return y * cos + pltpu.roll(y, 32, 1) * sa + pltpu.roll(y, 96, 1) * sb


def _proj_kernel(x_ref, g_ref, cos_ref, sa_ref, sb_ref, *refs, segs):
    n_seg = len(segs)
    w_refs = refs[:n_seg]
    o_refs = list(refs[n_seg:])
    nb = _rmsnorm(x_ref[...], g_ref[...]).astype(BF16)
    cos, sa, sb = cos_ref[...], sa_ref[...], sb_ref[...]
    for w_ref, kinds in zip(w_refs, segs):
        y = _dot(nb, w_ref[...])
        if kinds == "plain":
            o_refs.pop(0)[...] = y
            continue
        if kinds == "rope_dual":
            o_refs.pop(0)[...] = y
            kinds = ("rope",) * (y.shape[1] // LANES)
        o_ref = o_refs.pop(0)
        for t, kind in enumerate(kinds):
            yt = y[:, t * LANES:(t + 1) * LANES]
            if kind == "rope":
                yt = _rope_tile(yt, cos, sa, sb)
            elif kind == "sigmoid":
                yt = _sigmoid(yt)
            o_ref[:, t * LANES:(t + 1) * LANES] = yt


def _proj(x, g, tabs, weights, segs, tm):
    n, d = x.shape
    nblk = tabs[0].shape[0] // tm
    out_shape, out_specs = [], []
    for w, kinds in zip(weights, segs):
        for _ in range(2 if kinds == "rope_dual" else 1):
            out_shape.append(jax.ShapeDtypeStruct((n, w.shape[1]), F32))
            out_specs.append(pl.BlockSpec((tm, w.shape[1]), lambda i: (i, 0)))
    tab_spec = pl.BlockSpec((tm, LANES), lambda i: (i % nblk, 0))
    return pl.pallas_call(
        functools.partial(_proj_kernel, segs=tuple(segs)),
        grid=(n // tm,),
        in_specs=[pl.BlockSpec((tm, d), lambda i: (i, 0)),
                  pl.BlockSpec((1, d), lambda i: (0, 0)),
                  tab_spec, tab_spec, tab_spec]
                 + [pl.BlockSpec(w.shape, lambda i: (0, 0)) for w in weights],
        out_specs=out_specs,
        out_shape=out_shape,
        compiler_params=_cparams(("parallel",)),
        name="norm_proj",
    )(x, g, *tabs, *weights)


def _outproj_kernel(h_ref, a_ref, b_ref, wa_ref, wb_ref, o_ref):
    o_ref[...] = (h_ref[...] + _dot(a_ref[...].astype(BF16), wa_ref[...])
                  + _dot(b_ref[...].astype(BF16), wb_ref[...]))


def _outproj(h, a, b, wa, wb, tm):
    n, d = h.shape
    row = lambda w: pl.BlockSpec((tm, w), lambda i: (i, 0))
    full = lambda w: pl.BlockSpec(w.shape, lambda i: (0, 0))
    return pl.pallas_call(
        _outproj_kernel,
        grid=(n // tm,),
        in_specs=[row(d), row(a.shape[1]), row(b.shape[1]), full(wa), full(wb)],
        out_specs=row(d),
        out_shape=jax.ShapeDtypeStruct((n, d), F32),
        compiler_params=_cparams(("parallel",)),
        name="out_proj",
    )(h, a, b, wa, wb)


def _ffn_kernel(h_ref, g_ref, wg_ref, wu_ref, wd_ref, o_ref, n_scr, acc_scr):
    f = pl.program_id(1)

    @pl.when(f == 0)
    def _():
        n_scr[...] = _rmsnorm(h_ref[...], g_ref[...]).astype(BF16)
        acc_scr[...] = jnp.zeros_like(acc_scr)

    nb = n_scr[...]
    gate = _dot(nb, wg_ref[...])
    up = _dot(nb, wu_ref[...])
    hid = (gate * _sigmoid(gate) * up).astype(BF16)
    acc_scr[...] += _dot(hid, wd_ref[...])

    @pl.when(f == pl.num_programs(1) - 1)
    def _():
        o_ref[...] = h_ref[...] + acc_scr[...]


def _ffn(h, g, wg, wu, wd, tm, tf):
    n, d = h.shape
    dff = wg.shape[1]
    return pl.pallas_call(
        _ffn_kernel,
        grid=(n // tm, dff // tf),
        in_specs=[pl.BlockSpec((tm, d), lambda i, f: (i, 0)),
                  pl.BlockSpec((1, d), lambda i, f: (0, 0)),
                  pl.BlockSpec((d, tf), lambda i, f: (0, f)),
                  pl.BlockSpec((d, tf), lambda i, f: (0, f)),
                  pl.BlockSpec((tf, d), lambda i, f: (f, 0))],
        out_specs=pl.BlockSpec((tm, d), lambda i, f: (i, 0)),
        out_shape=jax.ShapeDtypeStruct((n, d), F32),
        scratch_shapes=[pltpu.VMEM((tm, d), BF16), pltpu.VMEM((tm, d), F32)],
        compiler_params=_cparams(("parallel", "arbitrary")),
        name="swiglu_ffn",
    )(h, g, wg, wu, wd)


def _ple_kernel(h_ref, p_ref, g_ref, gf_ref, wg_ref, wp_ref, o_ref, *, final):
    h = h_ref[...]
    gate = _sigmoid(_dot(_rmsnorm(h, g_ref[...]).astype(BF16), wg_ref[...]))
    out = h + gate * _dot(p_ref[...].astype(BF16), wp_ref[...])
    if final:
        out = _rmsnorm(out, gf_ref[...])
    o_ref[...] = out


def _ple(h, p, g, gf, wg, wp, tm, final):
    n, d = h.shape
    row = lambda w: pl.BlockSpec((tm, w), lambda i: (i, 0))
    full = lambda a: pl.BlockSpec(a.shape, lambda i: (0, 0))
    return pl.pallas_call(
        functools.partial(_ple_kernel, final=final),
        grid=(n // tm,),
        in_specs=[row(d), row(p.shape[1]), full(g), full(gf), full(wg), full(wp)],
        out_specs=row(d),
        out_shape=jax.ShapeDtypeStruct((n, d), F32),
        compiler_params=_cparams(("parallel",)),
        name="ple",
    )(h, p, g, gf, wg, wp)


def _moe_kernel(h_ref, g_ref, wr_hi_ref, wr_lo_ref, wg_ref, wu_ref, wd_ref, o_ref,
                n_scr, gw_scr, acc_scr):
    e = pl.program_id(1)
    f = pl.program_id(2)
    lane = lax.broadcasted_iota(jnp.int32, (1, LANES), 1)

    @pl.when((e == 0) & (f == 0))
    def _():
        n = _rmsnorm(h_ref[...], g_ref[...])
        n_hi, n_lo = _split(n)
        n_scr[...] = n_hi
        logits = (_dot(n_hi, wr_hi_ref[...]) + _dot(n_lo, wr_hi_ref[...])
                  + _dot(n_hi, wr_lo_ref[...]))
        logits = jnp.where(lane < N_EXPERTS, logits, NEG)
        lane_f = lane.astype(F32)
        v1 = jnp.max(logits, axis=1, keepdims=True)
        i1 = jnp.min(jnp.where(logits == v1, lane_f, float(LANES)), axis=1, keepdims=True)
        rest = jnp.where(lane_f == i1, NEG, logits)
        v2 = jnp.max(rest, axis=1, keepdims=True)
        i2 = jnp.min(jnp.where(rest == v2, lane_f, float(LANES)), axis=1, keepdims=True)
        e2 = jnp.exp(v2 - v1)
        g1 = 1.0 / (1.0 + e2)
        gw_scr[...] = jnp.where(lane_f == i1, g1, jnp.where(lane_f == i2, e2 * g1, 0.0))
        acc_scr[...] = jnp.zeros_like(acc_scr)

    nb = n_scr[...]
    gate = _dot(nb, wg_ref[0])
    up = _dot(nb, wu_ref[0])
    hid = (gate * _sigmoid(gate) * up).astype(BF16)
    col = jnp.sum(jnp.where(lane == e, gw_scr[...], 0.0), axis=1, keepdims=True)
    acc_scr[...] += col * _dot(hid, wd_ref[0])

    @pl.when((e == pl.num_programs(1) - 1) & (f == pl.num_programs(2) - 1))
    def _():
        o_ref[...] = h_ref[...] + acc_scr[...]


def _moe(h, g, wr_hi, wr_lo, wg, wu, wd, tm, tf):
    n, d = h.shape
    n_e, _, dff = wg.shape
    return pl.pallas_call(
        _moe_kernel,
        grid=(n // tm, n_e, dff // tf),
        in_specs=[pl.BlockSpec((tm, d), lambda i, e, f: (i, 0)),
                  pl.BlockSpec((1, d), lambda i, e, f: (0, 0)),
                  pl.BlockSpec(wr_hi.shape, lambda i, e, f: (0, 0)),
                  pl.BlockSpec(wr_lo.shape, lambda i, e, f: (0, 0)),
                  pl.BlockSpec((1, d, tf), lambda i, e, f: (e, 0, f)),
                  pl.BlockSpec((1, d, tf), lambda i, e, f: (e, 0, f)),
                  pl.BlockSpec((1, tf, d), lambda i, e, f: (e, f, 0))],
        out_specs=pl.BlockSpec((tm, d), lambda i, e, f: (i, 0)),
        out_shape=jax.ShapeDtypeStruct((n, d), F32),
        scratch_shapes=[pltpu.VMEM((tm, d), BF16), pltpu.VMEM((tm, LANES), F32),
                        pltpu.VMEM((tm, d), F32)],
        compiler_params=_cparams(("parallel", "arbitrary", "arbitrary")),
        name="moe_ffn",
    )(h, g, wr_hi, wr_lo, wg, wu, wd)


def _gather_kernel(pt_ref, *refs, n_pg):
    del pt_ref
    page_refs, tail_ref, o_ref = refs[:n_pg], refs[n_pg], refs[n_pg + 1]
    s = pl.program_id(1)
    last = pl.num_programs(1) - 1

    n_heads = page_refs[0].shape[3]
    half = n_heads * HEAD_DIM

    @pl.when(s < last)
    def _():
        for i, p_ref in enumerate(page_refs):
            for kv in range(2):
                for pair in range(n_heads // 2):
                    tile = p_ref[0, 0, kv, 2 * pair:2 * pair + 2].reshape(LANES, PAGE)
                    c0 = kv * half + pair * LANES
                    o_ref[0, i * PAGE:(i + 1) * PAGE, c0:c0 + LANES] = tile.T.astype(o_ref.dtype)

    @pl.when(s == last)
    def _():
        o_ref[0, 0:tail_ref.shape[1], :] = tail_ref[0].astype(o_ref.dtype)


def _gather_ctx(cache, page_table, tail, n_pg):
    b, n_pages = page_table.shape
    n_heads = cache.shape[4]
    w = 2 * n_heads * HEAD_DIM
    n_tail = tail.shape[1]
    steps = n_pages // n_pg
    cache = cache.transpose(0, 1, 3, 4, 5, 2)

    def page_spec(i):
        return pl.BlockSpec(
            (1, 1, 2, n_heads, HEAD_DIM, PAGE),
            lambda bi, s, pt: (0, pt[bi, jnp.minimum(s, steps - 1) * n_pg + i], 0, 0, 0, 0))

    return pl.pallas_call(
        functools.partial(_gather_kernel, n_pg=n_pg),
        grid_spec=pltpu.PrefetchScalarGridSpec(
            num_scalar_prefetch=1,
            grid=(b, steps + 1),
            in_specs=[page_spec(i) for i in range(n_pg)]
                     + [pl.BlockSpec((1, n_tail, w), lambda bi, s, pt: (bi, 0, 0))],
            out_specs=pl.BlockSpec((1, n_pg * PAGE, w), lambda bi, s, pt: (bi, s, 0)),
        ),
        out_shape=jax.ShapeDtypeStruct((b, n_pages * PAGE + n_tail, w), BF16),
        compiler_params=_cparams(("parallel", "arbitrary")),
        name="paged_gather",
    )(page_table, *([cache] * n_pg), tail)


def _compress_kernel(pt_ref, *refs, n_pg):
    del pt_ref
    k_refs, v_refs = refs[:n_pg], refs[n_pg:2 * n_pg]
    (w1k_ref, w1v_ref, w2_ref, pe_ref, w1_ref, kc_ref, vc_ref,
     ak_scr, av_scr) = refs[2 * n_pg:]
    s = pl.program_id(1)
    n_ch = ak_scr.shape[0]
    per_page = PAGE // NSA_STRIDE

    for p_refs, a_scr in ((k_refs, ak_scr), (v_refs, av_scr)):
        for i, p_ref in enumerate(p_refs):
            base = pl.multiple_of((s * n_pg + i) * per_page, per_page)
            for j in range(NSA_STRIDE):
                a_scr[pl.ds(base, per_page), j * LANES:(j + 1) * LANES] = (
                    p_ref[0, pl.ds(j, per_page, stride=NSA_STRIDE), :])

    @pl.when(s == pl.num_programs(1) - 1)
    def _():
        for kv, (a_scr, w1p_ref, o_ref) in enumerate(((ak_scr, w1k_ref, kc_ref),
                                                      (av_scr, w1v_ref, vc_ref))):
            r = _dot(a_scr[...].astype(BF16), w1p_ref[...])
            bias = _dot(pe_ref[kv], w1_ref[kv])[0:1, :]
            hid = []
            for g in range(2):
                a = r[:, (2 * g) * NSA_HID:(2 * g + 1) * NSA_HID]
                b_next = pltpu.roll(r[:, (2 * g + 1) * NSA_HID:(2 * g + 2) * NSA_HID],
                                    n_ch - 1, 0)
                pre = a + b_next + bias
                hid.append(pre * _sigmoid(pre))
            hid = jnp.concatenate(hid, axis=1).astype(BF16)
            o_ref[0] = _dot(hid, w2_ref[kv])


def _compress(pool, page_table, w1k, w1v, w2, pe, w1, n_pg):
    b, n_pages = page_table.shape
    n_ch = n_pages * (PAGE // NSA_STRIDE)
    steps = n_pages // n_pg
    full = lambda a: pl.BlockSpec(a.shape, lambda bi, s, pt: (0,) * a.ndim)

    def page_spec(i, kv):
        return pl.BlockSpec((1, PAGE, LANES), lambda bi, s, pt: (pt[bi, s * n_pg + i], 0, kv))

    out_spec = pl.BlockSpec((1, n_ch, LANES), lambda bi, s, pt: (bi, 0, 0))
    return pl.pallas_call(
        functools.partial(_compress_kernel, n_pg=n_pg),
        grid_spec=pltpu.PrefetchScalarGridSpec(
            num_scalar_prefetch=1,
            grid=(b, steps),
            in_specs=[page_spec(i, kv) for kv in range(2) for i in range(n_pg)]
                     + [full(w1k), full(w1v), full(w2), full(pe), full(w1)],
            out_specs=[out_spec, out_spec],
            scratch_shapes=[pltpu.VMEM((n_ch, NSA_STRIDE * LANES), F32),
                            pltpu.VMEM((n_ch, NSA_STRIDE * LANES), F32)],
        ),
        out_shape=[jax.ShapeDtypeStruct((b, n_ch, LANES), F32)] * 2,
        compiler_params=_cparams(("parallel", "arbitrary")),
        name="nsa_compress",
    )(page_table, *([pool] * (2 * n_pg)), w1k, w1v, w2, pe, w1)


def _flash_step(state, s, valid, v):
    m, l, acc = state
    s = jnp.where(valid, s, NEG)
    m_new = jnp.maximum(m, jnp.max(s, axis=1, keepdims=True))
    alpha = jnp.exp(m - m_new)
    p = jnp.where(valid, jnp.exp(s - m_new), 0.0)
    l = alpha * l + jnp.sum(p, axis=1, keepdims=True)
    acc = alpha * acc + _dot(p.astype(BF16), v)
    return m_new, l, acc


def _flash_init(rows):
    return (jnp.full((rows, 1), NEG, F32), jnp.zeros((rows, 1), F32),
            jnp.zeros((rows, LANES), F32))


def _flash_out(state):
    _, l, acc = state
    return jnp.where(l > 0.0, acc / jnp.where(l > 0.0, l, 1.0), 0.0)


def _stack(x, times):
    return jnp.concatenate([x] * times, axis=0)


N_WIN_TILES = NSA_WINDOW // LANES + 1


def _nsa_kernel(qp_ref, qr_ref, gt_ref, kc_ref, vc_ref, sel_ref, ov_ref, ex_ref, *refs,
                tq, tk, pos0, wpos0):
    win_refs, o_ref = refs[:N_WIN_TILES], refs[N_WIN_TILES]
    i = pl.program_id(1)
    q0 = pos0 + i * tq
    scale = HEAD_DIM ** -0.5
    r4 = NSA_GROUP
    qpos = q0 + lax.broadcasted_iota(jnp.int32, (tq, 1), 0)
    qpos4 = _stack(qpos, r4)
    qblk = qpos // NSA_SEL_BLOCK
    lane = lax.broadcasted_iota(jnp.int32, (1, LANES), 1)
    lane_f = lane.astype(F32)
    n_ch = kc_ref.shape[1]
    c_end = lax.broadcasted_iota(jnp.int32, (1, n_ch), 1) * NSA_STRIDE + (2 * NSA_STRIDE - 1)
    kc = kc_ref[0].astype(BF16)
    vc = vc_ref[0].astype(BF16)
    gates = gt_ref[0]
    n_sel_tiles = (q0 + tq - 1) // tk + 1
    win_start = (q0 // LANES) * LANES - NSA_WINDOW

    q_rot, o_cmp, picked = [], [], []
    for g in range(2):
        heads = range(g * r4, (g + 1) * r4)
        q_c = jnp.concatenate([qp_ref[0, :, h * LANES:(h + 1) * LANES] for h in heads], axis=0)
        q_r = jnp.concatenate([qr_ref[0, :, h * LANES:(h + 1) * LANES] for h in heads], axis=0)
        q_c = (q_c * scale).astype(BF16)
        q_rot.append((q_r * scale).astype(BF16))

        s = _dot_nt(q_c, kc)
        valid = c_end <= qpos4
        s = jnp.where(valid, s, NEG)
        e = jnp.where(valid, jnp.exp(s - jnp.max(s, axis=1, keepdims=True)), 0.0)
        l = jnp.sum(e, axis=1, keepdims=True)
        p = jnp.where(l > 0.0, e / jnp.where(l > 0.0, l, 1.0), 0.0)
        o_cmp.append(_dot(p.astype(BF16), vc))

        p_sum = p[0:tq]
        for r in range(1, r4):
            p_sum = p_sum + p[r * tq:(r + 1) * tq]
        p_hi, p_lo = _split(p_sum)
        imp = _dot(p_hi, ov_ref[...]) + _dot(p_lo, ov_ref[...])

        cand = (lane <= qblk) & (lane != 0) & (lane != qblk) & (lane != qblk - 1)
        work = jnp.where(cand, imp, -1.0)
        pick = jnp.zeros((tq, LANES), F32)
        for _ in range(NSA_PICKS):
            top = jnp.max(work, axis=1, keepdims=True)
            first = jnp.min(jnp.where(work == top, lane_f, float(LANES)), axis=1,
                            keepdims=True)
            hit = (lane_f == first) & (top >= 0.0)
            pick = jnp.where(hit, 1.0, pick)
            work = jnp.where(hit, -1.0, work)
        picked.append(pick.astype(BF16))

    def sel_body(t, states):
        k0 = pl.multiple_of(t * tk, tk)
        k = sel_ref[0, pl.ds(k0, tk), 0:LANES].astype(BF16)
        v = sel_ref[0, pl.ds(k0, tk), LANES:2 * LANES].astype(BF16)
        kpos = k0 + lax.broadcasted_iota(jnp.int32, (1, tk), 1)
        kblk = kpos // NSA_SEL_BLOCK
        forced = (kblk == 0) | (kblk == qblk) | (kblk == qblk - 1)
        out = ()
        for g in range(2):
            chosen = _dot(picked[g], ex_ref[t]) > 0.5
            keep = jnp.where((chosen | forced) & (kpos <= qpos), 1.0, 0.0)
            out += _flash_step(states[3 * g:3 * g + 3], _dot_nt(q_rot[g], k),
                               _stack(keep, r4) > 0.5, v)
        return out

    sel_states = lax.fori_loop(0, n_sel_tiles, sel_body, _flash_init(r4 * tq) * 2)
    o_sel = [_flash_out(sel_states[0:3]), _flash_out(sel_states[3:6])]

    win_states = [_flash_init(r4 * tq), _flash_init(r4 * tq)]
    for jt, w_ref in enumerate(win_refs):
        k = w_ref[0, :, 0:LANES].astype(BF16)
        v = w_ref[0, :, LANES:2 * LANES].astype(BF16)
        wpos = win_start + jt * LANES + lane
        keep = jnp.where((wpos >= wpos0) & (wpos <= qpos) & (qpos - wpos < NSA_WINDOW),
                         1.0, 0.0)
        keep4 = _stack(keep, r4) > 0.5
        for g in range(2):
            win_states[g] = _flash_step(win_states[g], _dot_nt(q_rot[g], k), keep4, v)
    o_win = [_flash_out(win_states[0]), _flash_out(win_states[1])]

    outs = []
    for g in range(2):
        for r in range(r4):
            h = g * r4 + r
            rows = slice(r * tq, (r + 1) * tq)
            o = (gates[:, 3 * h:3 * h + 1] * o_cmp[g][rows]
                 + gates[:, 3 * h + 1:3 * h + 2] * o_sel[g][rows]
                 + gates[:, 3 * h + 2:3 * h + 3] * o_win[g][rows])
            if h % 2 != g:
                o = pltpu.roll(o, HEAD_DIM, 1)
            outs.append(o)

    for pair in range(NSA_HEADS // 2):
        o_ref[0, :, pair * LANES:(pair + 1) * LANES] = jnp.where(
            lane < HEAD_DIM, outs[2 * pair], outs[2 * pair + 1])


def _nsa_t_kernel(qp_ref, qr_ref, gt_ref, kc_ref, vc_ref, sel_ref, ovt_ref, ext_ref, *refs,
                  tq, tk, pos0, wpos0):
    win_refs, o_ref, svt_scr, vct_scr = (refs[:N_WIN_TILES], refs[N_WIN_TILES],
                                         refs[N_WIN_TILES + 1], refs[N_WIN_TILES + 2])
    i = pl.program_id(1)
    q0 = pos0 + i * tq
    scale = HEAD_DIM ** -0.5
    r4 = NSA_GROUP
    n_ch = kc_ref.shape[1]
    row = lax.broadcasted_iota(jnp.int32, (LANES, 1), 0)
    row_f = row.astype(F32)
    qpos = q0 + lax.broadcasted_iota(jnp.int32, (1, tq), 1)
    qblk = qpos // NSA_SEL_BLOCK
    n_sel_tiles = (q0 + tq - 1) // tk + 1
    win_start = (q0 // LANES) * LANES - NSA_WINDOW
    own_half = [row < HEAD_DIM, row >= HEAD_DIM]

    def lanes4(x):
        return jnp.concatenate([x] * r4, axis=1)

    def v_t(tiles):
        return jnp.concatenate([t.T for t in tiles], axis=1)

    @pl.when(i == 0)
    def _():
        def transpose_tile(t, carry):
            k0 = pl.multiple_of(t * tk, tk)
            vt = v_t([sel_ref[0, pl.ds(k0 + j * LANES, LANES), LANES:2 * LANES]
                      for j in range(tk // LANES)])
            for g in range(2):
                svt_scr[g, t] = jnp.where(own_half[g], vt, 1.0).astype(BF16)
            return carry

        lax.fori_loop(0, sel_ref.shape[1] // tk, transpose_tile, 0)
        vct = v_t([vc_ref[0, j * LANES:(j + 1) * LANES, :] for j in range(n_ch // LANES)])
        for g in range(2):
            vct_scr[g] = jnp.where(own_half[g], vct, 1.0).astype(BF16)

    kc = kc_ref[0].astype(BF16)
    c_end = (lax.broadcasted_iota(jnp.int32, (n_ch, 1), 0) * NSA_STRIDE
             + (2 * NSA_STRIDE - 1))
    cmp_bias = lanes4(jnp.where(c_end <= qpos, 0.0, NEG))

    q_rot, o_cmp, picked = [], [], []
    for g in range(2):
        heads = range(g * r4, (g + 1) * r4)
        q_c = jnp.concatenate([qp_ref[0, :, h * LANES:(h + 1) * LANES] for h in heads], axis=0)
        q_r = jnp.concatenate([qr_ref[0, :, h * LANES:(h + 1) * LANES] for h in heads], axis=0)
        q_c = (q_c * scale).astype(BF16)
        q_rot.append((q_r * scale).astype(BF16))

        s = _dot_nt(kc, q_c) + cmp_bias
        m = jnp.max(s, axis=0, keepdims=True)
        e = jnp.exp(s - m)
        p = jnp.where(m > 0.5 * NEG, e / jnp.sum(e, axis=0, keepdims=True), 0.0)
        o_cmp.append(_dot(vct_scr[g], p.astype(BF16)))

        p_sum = p[:, 0:tq]
        for r in range(1, r4):
            p_sum = p_sum + p[:, r * tq:(r + 1) * tq]
        p_hi, p_lo = _split(p_sum)
        imp = _dot(ovt_ref[...], p_hi) + _dot(ovt_ref[...], p_lo)

        cand = (row <= qblk) & (row != 0) & (row != qblk) & (row != qblk - 1)
        work = jnp.where(cand, imp, -1.0)
        pick = jnp.zeros((LANES, tq), F32)
        for _ in range(NSA_PICKS):
            top = jnp.max(work, axis=0, keepdims=True)
            first = jnp.min(jnp.where(work == top, row_f, float(LANES)), axis=0, keepdims=True)
            hit = (row_f == first) & (top >= 0.0)
            pick = jnp.where(hit, 1.0, pick)
            work = jnp.where(hit, -1.0, work)
        picked.append(pick.astype(BF16))

    def flash_t(state, s, vt):
        m, acc = state
        m_new = jnp.maximum(m, jnp.max(s, axis=0, keepdims=True))
        p = jnp.exp(s - m_new).astype(BF16)
        return m_new, jnp.exp(m - m_new) * acc + _dot(vt, p)

    def init_t():
        return (jnp.full((1, r4 * tq), NEG, F32), jnp.zeros((LANES, r4 * tq), F32))

    def sel_body(t, states):
        k0 = pl.multiple_of(t * tk, tk)
        k = sel_ref[0, pl.ds(k0, tk), 0:LANES].astype(BF16)
        kpos = k0 + lax.broadcasted_iota(jnp.int32, (tk, 1), 0)
        kblk = kpos // NSA_SEL_BLOCK
        forced = (kblk == 0) | (kblk == qblk) | (kblk == qblk - 1)
        causal = kpos <= qpos
        out = ()
        for g in range(2):
            chosen = _dot(ext_ref[t], picked[g]) > 0.5
            bias = lanes4(jnp.where((chosen | forced) & causal, 0.0, NEG))
            out += flash_t(states[2 * g:2 * g + 2], _dot_nt(k, q_rot[g]) + bias, svt_scr[g, t])
        return out

    sel_states = lax.fori_loop(0, n_sel_tiles, sel_body, init_t() * 2)

    win_states = [init_t(), init_t()]
    for jt, w_ref in enumerate(win_refs):
        k = w_ref[0, :, 0:LANES].astype(BF16)
        vt = w_ref[0, :, LANES:2 * LANES].T
        wpos = win_start + jt * LANES + row
        keep = (wpos >= wpos0) & (wpos <= qpos) & (qpos - wpos < NSA_WINDOW)
        bias = lanes4(jnp.where(keep, 0.0, NEG))
        for g in range(2):
            win_states[g] = flash_t(win_states[g], _dot_nt(k, q_rot[g]) + bias,
                                    jnp.where(own_half[g], vt, 1.0).astype(BF16))

    gates_t = gt_ref[0].T
    for g in range(2):
        ones_row = HEAD_DIM * (1 - g)
        o_sel = sel_states[2 * g + 1] / sel_states[2 * g + 1][ones_row:ones_row + 1, :]
        o_win = win_states[g][1] / win_states[g][1][ones_row:ones_row + 1, :]
        mixed = []
        for r in range(r4):
            h = g * r4 + r
            cols = slice(r * tq, (r + 1) * tq)
            o = (gates_t[3 * h:3 * h + 1, :] * o_cmp[g][:, cols]
                 + gates_t[3 * h + 1:3 * h + 2, :] * o_sel[:, cols]
                 + gates_t[3 * h + 2:3 * h + 3, :] * o_win[:, cols])
            mixed.append(o[g * HEAD_DIM:(g + 1) * HEAD_DIM, :])
        for pr in range(r4 // 2):
            pair = g * (r4 // 2) + pr
            tile = jnp.concatenate([mixed[2 * pr], mixed[2 * pr + 1]], axis=0)
            o_ref[0, :, pair * LANES:(pair + 1) * LANES] = tile.T


def _nsa_attn(qp, qr, gates, kc, vc, sel_ctx, win_ctx, ov, ex, tq, tk, pos0, wpos0):
    b, t_q, _ = qp.shape
    t_c = sel_ctx.shape[1]
    n_win_tiles_total = win_ctx.shape[1] // LANES
    tile0 = pos0 // LANES - NSA_WINDOW // LANES - wpos0 // LANES
    transposed = tq % LANES == 0
    if transposed:
        body, ov, ex = _nsa_t_kernel, ov.T, ex.transpose(0, 2, 1)
        scratch = [pltpu.VMEM((2, t_c // tk, LANES, tk), BF16),
                   pltpu.VMEM((2, LANES, kc.shape[1]), BF16)]
    else:
        body, scratch = _nsa_kernel, []

    def win_spec(jt):
        def idx(bi, i):
            t = tile0 + (i * tq) // LANES + jt
            return (bi, jnp.clip(t, 0, n_win_tiles_total - 1), 0)
        return pl.BlockSpec((1, LANES, 2 * LANES), idx)

    qspec = pl.BlockSpec((1, tq, NSA_HEADS * LANES), lambda bi, i: (bi, i, 0))
    return pl.pallas_call(
        functools.partial(body, tq=tq, tk=tk, pos0=pos0, wpos0=wpos0),
        grid=(b, t_q // tq),
        in_specs=[qspec, qspec,
                  pl.BlockSpec((1, tq, LANES), lambda bi, i: (bi, i, 0)),
                  pl.BlockSpec((1,) + kc.shape[1:], lambda bi, i: (bi, 0, 0)),
                  pl.BlockSpec((1,) + vc.shape[1:], lambda bi, i: (bi, 0, 0)),
                  pl.BlockSpec((1, t_c, 2 * LANES), lambda bi, i: (bi, 0, 0)),
                  pl.BlockSpec(ov.shape, lambda bi, i: (0, 0)),
                  pl.BlockSpec(ex.shape, lambda bi, i: (0, 0, 0))]
                 + [win_spec(jt) for jt in range(N_WIN_TILES)],
        out_specs=pl.BlockSpec((1, tq, NSA_HEADS * HEAD_DIM), lambda bi, i: (bi, i, 0)),
        out_shape=jax.ShapeDtypeStruct((b, t_q, NSA_HEADS * HEAD_DIM), F32),
        scratch_shapes=scratch,
        compiler_params=_cparams(("parallel", "arbitrary")),
        name="nsa_attn_t" if transposed else "nsa_attn",
    )(qp, qr, gates, kc, vc, sel_ctx, ov, ex, *([win_ctx] * N_WIN_TILES))


def _sb_kernel(q_ref, k_ref, v_ref, tri_ref, o_ref, alive_ref, *, tq, tk, pos0, kpos0):
    i = pl.program_id(2)
    q0 = pos0 + i * tq
    scale = HEAD_DIM ** -0.5
    qpos = q0 + lax.broadcasted_iota(jnp.int32, (tq, 1), 0)
    lane = lax.broadcasted_iota(jnp.int32, (1, LANES), 1)
    n_tiles = (q0 + tq - 2 - kpos0) // tk + 1
    tri = tri_ref[...]
    qs = [(q_ref[0, :, hh * LANES:(hh + 1) * LANES] * scale).astype(BF16) for hh in range(2)]

    def tile(q, k, v, valid, acc, run):
        z = _dot_nt(q, k)
        soft = jnp.log(1.0 + jnp.exp(-jnp.abs(z)))
        log_break = jnp.minimum(z, 0.0) - soft
        log_stay = jnp.where(valid, log_break - z, 0.0)
        s_hi, s_lo = _split(log_stay)
        after = _dot(s_hi, tri) + _dot(s_lo, tri)
        a = jnp.where(valid, jnp.exp(log_break + after + run), 0.0)
        return acc + _dot(a.astype(BF16), v), run + jnp.sum(log_stay, axis=1, keepdims=True)

    def cond(carry):
        return (carry[0] < n_tiles) & (carry[1] > 0)

    def body(carry):
        step, _, acc0, run0, acc1, run1 = carry
        k0 = pl.multiple_of((n_tiles - 1 - step) * tk, tk)
        k = k_ref[0, pl.ds(k0, tk), :].astype(BF16)
        v = v_ref[0, pl.ds(k0, tk), :].astype(BF16)
        valid = (kpos0 + k0 + lax.broadcasted_iota(jnp.int32, (1, tk), 1)) < qpos
        acc0, run0 = tile(qs[0], k, v, valid, acc0, run0)
        acc1, run1 = tile(qs[1], k, v, valid, acc1, run1)
        live = jnp.maximum(jnp.max(run0), jnp.max(run1)) > SB_DEAD
        return step + 1, live.astype(jnp.int32), acc0, run0, acc1, run1

    zero_acc = jnp.zeros((tq, LANES), F32)
    zero_run = jnp.zeros((tq, 1), F32)
    _, live, acc0, _, acc1, _ = lax.while_loop(
        cond, body, (jnp.int32(0), jnp.int32(1), zero_acc, zero_run, zero_acc, zero_run))
    o_ref[0] = jnp.where(lane < HEAD_DIM, acc0, acc1)
    alive_ref[...] = jnp.zeros(alive_ref.shape, F32) + live.astype(F32)


def _sb_attn(q, kv, tri, tq, tk, pos0, kpos0=0):
    b, t_q, _ = q.shape
    t_c = kv.shape[1]
    n_pairs = SB_HEADS // 2
    n_q = t_q // tq
    out, alive = pl.pallas_call(
        functools.partial(_sb_kernel, tq=tq, tk=tk, pos0=pos0, kpos0=kpos0),
        grid=(b, n_pairs, n_q),
        in_specs=[pl.BlockSpec((1, tq, 2 * LANES), lambda bi, pr, i: (bi, i, pr)),
                  pl.BlockSpec((1, t_c, LANES), lambda bi, pr, i: (bi, 0, pr)),
                  pl.BlockSpec((1, t_c, LANES), lambda bi, pr, i: (bi, 0, n_pairs + pr)),
                  pl.BlockSpec(tri.shape, lambda bi, pr, i: (0, 0))],
        out_specs=[pl.BlockSpec((1, tq, LANES), lambda bi, pr, i: (bi, i, pr)),
                   pl.BlockSpec((1, 1, 1, 8, LANES), lambda bi, pr, i: (bi, pr, i, 0, 0))],
        out_shape=[jax.ShapeDtypeStruct((b, t_q, SB_HEADS * HEAD_DIM), F32),
                   jax.ShapeDtypeStruct((b, n_pairs, n_q, 8, LANES), F32)],
        compiler_params=_cparams(("parallel", "parallel", "arbitrary")),
        name="sb_attn",
    )(q, kv, kv, tri)
    return out, jnp.max(alive)


def _moba_kernel(q_ref, k_ref, v_ref, o_ref, km_scr, *, tq, pos0, n_blocks, bps):
    i = pl.program_id(2)
    blk = MOBA_BLOCK
    q0 = pos0 + i * tq
    own = q0 // blk
    scale = HEAD_DIM ** -0.5
    qpos = q0 + lax.broadcasted_iota(jnp.int32, (tq, 1), 0)
    lane = lax.broadcasted_iota(jnp.int32, (1, LANES), 1)
    lane_f = lane.astype(F32)

    @pl.when(i == 0)
    def _():
        km_scr[...] = jnp.zeros_like(km_scr)
        for n in range(n_blocks):
            km_scr[n:n + 1, :] = jnp.sum(k_ref[0, n * blk:(n + 1) * blk, :].astype(F32),
                                         axis=0, keepdims=True) * (1.0 / blk)

    km_hi, km_lo = _split(km_scr[...])
    q_s, picked, states = [], [], []
    own0 = pl.multiple_of(own * blk, blk)
    k_own = k_ref[0, pl.ds(own0, blk), :].astype(BF16)
    v_own = v_ref[0, pl.ds(own0, blk), :].astype(BF16)
    causal = (own0 + lax.broadcasted_iota(jnp.int32, (1, blk), 1)) <= qpos
    for hh in range(2):
        q = q_ref[0, :, hh * LANES:(hh + 1) * LANES]
        q_hi, q_lo = _split(q)
        gate = _dot_nt(q_hi, km_hi) + _dot_nt(q_lo, km_hi) + _dot_nt(q_hi, km_lo)
        work = jnp.where(lane < own, gate, NEG)
        pick = jnp.zeros((tq, LANES), F32)
        for _ in range(MOBA_TOPK):
            top = jnp.max(work, axis=1, keepdims=True)
            first = jnp.min(jnp.where(work == top, lane_f, float(LANES)), axis=1,
                            keepdims=True)
            hit = (lane_f == first) & (top > 0.5 * NEG)
            pick = jnp.where(hit, 1.0, pick)
            work = jnp.where(hit, NEG, work)
        picked.append(pick)
        q_s.append((q * scale).astype(BF16))
        states.append(_flash_step(_flash_init(tq), _dot_nt(q_s[hh], k_own), causal, v_own))

    def body(t, carry):
        k0 = pl.multiple_of(t * (bps * blk), bps * blk)
        k = k_ref[0, pl.ds(k0, bps * blk), :].astype(BF16)
        v = v_ref[0, pl.ds(k0, bps * blk), :].astype(BF16)
        out = []
        for hh in range(2):
            m, l, acc = carry[3 * hh:3 * hh + 3]
            s = _dot_nt(q_s[hh], k)
            rows = [jnp.sum(jnp.where(lane == t * bps + j, picked[hh], 0.0), axis=1,
                            keepdims=True) > 0.5 for j in range(bps)]
            parts = [s[:, j * blk:(j + 1) * blk] for j in range(bps)]
            m_new = m
            for j in range(bps):
                m_new = jnp.maximum(m_new, jnp.where(
                    rows[j], jnp.max(parts[j], axis=1, keepdims=True), NEG))
            alpha = jnp.exp(m - m_new)
            p = [jnp.exp(parts[j] - jnp.where(rows[j], m_new, -NEG)) for j in range(bps)]
            p = p[0] if bps == 1 else jnp.concatenate(p, axis=1)
            l = alpha * l + jnp.sum(p, axis=1, keepdims=True)
            acc = alpha * acc + _dot(p.astype(BF16), v)
            out += [m_new, l, acc]
        return tuple(out)

    n_steps = (own + bps - 1) // bps
    final = lax.fori_loop(0, n_steps, body, tuple(states[0]) + tuple(states[1]))
    o_ref[0] = jnp.where(lane < HEAD_DIM, _flash_out(final[0:3]), _flash_out(final[3:6]))


def _moba_t_kernel(q_ref, k_ref, v_ref, o_ref, km_scr, vt_scr, pk_scr, *, tq, pos0, n_blocks,
                   bps):
    i = pl.program_id(2)
    blk = MOBA_BLOCK
    sub = blk // LANES
    q0 = pos0 + i * tq
    own = q0 // blk
    scale = HEAD_DIM ** -0.5
    row = lax.broadcasted_iota(jnp.int32, (LANES, 1), 0)
    row_f = row.astype(F32)
    qpos = q0 + lax.broadcasted_iota(jnp.int32, (1, tq), 1)

    @pl.when(i == 0)
    def _():
        km_scr[...] = jnp.zeros_like(km_scr)
        for n in range(n_blocks):
            km_scr[n:n + 1, :] = jnp.sum(k_ref[0, n * blk:(n + 1) * blk, :], axis=0,
                                         keepdims=True) * (1.0 / blk)

        def transpose_block(n, carry):
            n0 = pl.multiple_of(n * blk, blk)
            vt = jnp.concatenate(
                [v_ref[0, pl.ds(n0 + j * LANES, LANES), :].T for j in range(sub)], axis=1)
            vt_scr[0, n] = jnp.where(row < HEAD_DIM, vt, 1.0).astype(BF16)
            vt_scr[1, n] = jnp.where(row >= HEAD_DIM, vt, 1.0).astype(BF16)
            return carry

        lax.fori_loop(0, n_blocks, transpose_block, 0)

    ones_rows = 16
    v_rows = [(0, HEAD_DIM + ones_rows), (HEAD_DIM - ones_rows, 2 * HEAD_DIM)]

    def weighted_values(hh, t, p):
        lo, hi = v_rows[hh]
        return _dot(vt_scr[hh, t, lo:hi, :], p)

    km_hi, km_lo = _split(km_scr[...])
    own0 = pl.multiple_of(own * blk, blk)
    k_own = k_ref[0, pl.ds(own0, blk), :].astype(BF16)
    causal = (own0 + lax.broadcasted_iota(jnp.int32, (blk, 1), 0)) <= qpos
    q_s, states = [], []
    for hh in range(2):
        q = q_ref[0, :, hh * LANES:(hh + 1) * LANES]
        q_hi, q_lo = _split(q)
        gate = _dot_nt(km_hi, q_hi) + _dot_nt(km_hi, q_lo) + _dot_nt(km_lo, q_hi)
        work = jnp.where(row < own, gate, NEG)
        pick = jnp.zeros((LANES, tq), F32)
        for _ in range(MOBA_TOPK):
            top = jnp.max(work, axis=0, keepdims=True)
            first = jnp.min(jnp.where(work == top, row_f, float(LANES)), axis=0, keepdims=True)
            hit = (row_f == first) & (top > 0.5 * NEG)
            pick = jnp.where(hit, 1.0, pick)
            work = jnp.where(hit, NEG, work)
        pk_scr[hh] = pick
        q_s.append((q * scale).astype(BF16))
        s = jnp.where(causal, _dot_nt(k_own, q_s[hh]), NEG)
        m = jnp.max(s, axis=0, keepdims=True)
        p = jnp.where(causal, jnp.exp(s - m), 0.0).astype(BF16)
        states += [m, weighted_values(hh, own, p)]

    def body(t, carry):
        k0 = pl.multiple_of(t * (bps * blk), bps * blk)
        k = k_ref[0, pl.ds(k0, bps * blk), :].astype(BF16)
        out = []
        for hh in range(2):
            m, acc = carry[2 * hh:2 * hh + 2]
            s = _dot_nt(k, q_s[hh])
            parts = [s[j * blk:(j + 1) * blk] for j in range(bps)]
            chosen = [pk_scr[hh, pl.ds(t * bps + j, 1), :] > 0.5 for j in range(bps)]
            m_new = m
            for j in range(bps):
                m_new = jnp.maximum(m_new, jnp.where(
                    chosen[j], jnp.max(parts[j], axis=0, keepdims=True), NEG))
            acc = jnp.exp(m - m_new) * acc
            for j in range(bps):
                p = jnp.exp(parts[j] - jnp.where(chosen[j], m_new, -NEG)).astype(BF16)
                acc = acc + weighted_values(hh, t * bps + j, p)
            out += [m_new, acc]
        return tuple(out)

    _, acc0, _, acc1 = lax.fori_loop(0, (own + bps - 1) // bps, body, tuple(states))
    o_t = jnp.concatenate(
        [acc0[0:HEAD_DIM] / acc0[HEAD_DIM:HEAD_DIM + 1, :],
         acc1[ones_rows:ones_rows + HEAD_DIM] / acc1[0:1, :]], axis=0)
    o_ref[0] = o_t.T


def _moba_attn_t(q, kv, tq, pos0, bps):
    b, t_q, _ = q.shape
    t_c = kv.shape[1]
    n_pairs = MOBA_HEADS // 2
    n_blocks = t_c // MOBA_BLOCK
    assert MOBA_BLOCK % tq == 0 and pos0 % tq == 0 and tq % LANES == 0
    assert n_blocks % bps == 0
    return pl.pallas_call(
        functools.partial(_moba_t_kernel, tq=tq, pos0=pos0, n_blocks=n_blocks, bps=bps),
        grid=(b, n_pairs, t_q // tq),
        in_specs=[pl.BlockSpec((1, tq, 2 * LANES), lambda bi, pr, i: (bi, i, pr)),
                  pl.BlockSpec((1, t_c, LANES), lambda bi, pr, i: (bi, 0, pr)),
                  pl.BlockSpec((1, t_c, LANES), lambda bi, pr, i: (bi, 0, n_pairs + pr))],
        out_specs=pl.BlockSpec((1, tq, LANES), lambda bi, pr, i: (bi, i, pr)),
        out_shape=jax.ShapeDtypeStruct((b, t_q, MOBA_HEADS * HEAD_DIM), F32),
        scratch_shapes=[pltpu.VMEM((LANES, LANES), F32),
                        pltpu.VMEM((2, n_blocks, LANES, MOBA_BLOCK), BF16),
                        pltpu.VMEM((2, LANES, tq), F32)],
        compiler_params=_cparams(("parallel", "parallel", "arbitrary")),
        name="moba_attn_t",
    )(q, kv, kv)


def _moba_attn(q, kv, tq, pos0, bps):
    b, t_q, _ = q.shape
    t_c = kv.shape[1]
    n_pairs = MOBA_HEADS // 2
    assert MOBA_BLOCK % tq == 0 and pos0 % tq == 0 and (pos0 // MOBA_BLOCK) % bps == 0
    return pl.pallas_call(
        functools.partial(_moba_kernel, tq=tq, pos0=pos0, n_blocks=t_c // MOBA_BLOCK, bps=bps),
        grid=(b, n_pairs, t_q // tq),
        in_specs=[pl.BlockSpec((1, tq, 2 * LANES), lambda bi, pr, i: (bi, i, pr)),
                  pl.BlockSpec((1, t_c, LANES), lambda bi, pr, i: (bi, 0, pr)),
                  pl.BlockSpec((1, t_c, LANES), lambda bi, pr, i: (bi, 0, n_pairs + pr))],
        out_specs=pl.BlockSpec((1, tq, LANES), lambda bi, pr, i: (bi, i, pr)),
        out_shape=jax.ShapeDtypeStruct((b, t_q, MOBA_HEADS * HEAD_DIM), F32),
        scratch_shapes=[pltpu.VMEM((LANES, LANES), F32)],
        compiler_params=_cparams(("parallel", "parallel", "arbitrary")),
        name="moba_attn",
    )(q, kv, kv)


def _pad_heads(w, halves):
    d = w.shape[0]
    n_h = len(halves)
    onehot = jax.nn.one_hot(jnp.asarray(halves), 2, dtype=w.dtype)
    return jnp.einsum("dhe,hs->dhse", w.reshape(d, n_h, HEAD_DIM), onehot).reshape(d, n_h * LANES)


def _rope_tables(pos):
    half = HEAD_DIM // 2
    inv = 1.0 / (ROPE_THETA ** (jnp.arange(half, dtype=F32) / half))
    ang = pos.astype(F32)[:, None] * inv[None, :]
    cos = jnp.tile(jnp.cos(ang), (1, LANES // half))
    sin = jnp.tile(jnp.sin(ang), (1, LANES // half))
    upper = (jnp.arange(LANES) % HEAD_DIM) >= half
    return cos, jnp.where(upper, sin, 0.0), jnp.where(upper, 0.0, -sin)


def _overlap_matrix(n_ch):
    c = jnp.arange(n_ch)[:, None]
    n = jnp.arange(LANES)[None, :]
    return ((c >= 4 * n - 1) & (c <= 4 * n + 3)).astype(BF16)


def _expand_matrix(t_c, tk):
    blk = (jnp.arange(t_c) // NSA_SEL_BLOCK).reshape(t_c // tk, 1, tk)
    return (jnp.arange(LANES)[None, :, None] == blk).astype(BF16)


def _compress_weights(w1, w2):
    half = NSA_STRIDE * HEAD_DIM
    out = []
    for kv in range(2):
        wa = w1[kv, :half].reshape(NSA_STRIDE, HEAD_DIM, NSA_HID)
        wb = w1[kv, half:].reshape(NSA_STRIDE, HEAD_DIM, NSA_HID)
        ab = jnp.concatenate([wa, wb], axis=2)
        z = jnp.zeros_like(ab)
        g0 = jnp.concatenate([ab, z], axis=1)
        g1 = jnp.concatenate([z, ab], axis=1)
        out.append(jnp.concatenate([g0, g1], axis=2).reshape(NSA_STRIDE * LANES, 4 * NSA_HID))
    z2 = jnp.zeros_like(w2)
    w2p = jnp.concatenate([jnp.concatenate([w2, z2], axis=2),
                           jnp.concatenate([z2, w2], axis=2)], axis=1)
    return out[0].astype(BF16), out[1].astype(BF16), w2p.astype(BF16)


def _prep_weights(w_in_even, w_out_even, nsa_cmp_pe, nsa_cmp_w1, nsa_cmp_w2, w_ffn_gate,
                  w_ffn_up, w_ffn_down, w_in_odd, w_out_odd, w_router, w_exp_gate, w_exp_up,
                  w_exp_down, w_ple_proj, w_ple_gate):
    bf = lambda a: a.astype(BF16)
    we = w_in_even[0]
    qw = NSA_HEADS * HEAD_DIM
    kvw = 2 * LANES
    c0 = qw + 3 * kvw
    n_gate = 3 * NSA_HEADS
    s0 = c0 + n_gate
    sbw = SB_HEADS * HEAD_DIM
    even = [
        bf(_pad_heads(we[:, :qw], [h // NSA_GROUP for h in range(NSA_HEADS)])),
        bf(we[:, qw:qw + kvw]),
        bf(we[:, qw + kvw:qw + 2 * kvw]),
        bf(we[:, qw + 2 * kvw:c0]),
        bf(jnp.pad(we[:, c0:s0], ((0, 0), (0, LANES - n_gate)))),
        bf(_pad_heads(we[:, s0:s0 + sbw], [h % 2 for h in range(SB_HEADS)])),
        bf(we[:, s0 + sbw:]),
    ]
    wo = w_in_odd[0]
    mw = MOBA_HEADS * HEAD_DIM
    odd = [bf(_pad_heads(wo[:, :mw], [h % 2 for h in range(MOBA_HEADS)])), bf(wo[:, mw:])]
    w1k, w1v, w2p = _compress_weights(nsa_cmp_w1[0], nsa_cmp_w2[0])
    pe = bf(jnp.broadcast_to(nsa_cmp_pe[0].reshape(2, 1, -1), (2, 8, 2 * NSA_STRIDE * HEAD_DIM)))
    wr = jnp.pad(w_router[0], ((0, 0), (0, LANES - N_EXPERTS)))
    wr_hi, wr_lo = _split(wr)
    return dict(
        even=even, odd=odd, w1k=w1k, w1v=w1v, w2p=w2p, pe=pe, w1=bf(nsa_cmp_w1[0]),
        wo_a=bf(w_out_even[0][:qw]), wo_b=bf(w_out_even[0][qw:]),
        ffn=(bf(w_ffn_gate[0]), bf(w_ffn_up[0]), bf(w_ffn_down[0])),
        wo_c=bf(w_out_odd[0]), wr_hi=wr_hi, wr_lo=wr_lo,
        exp=(bf(w_exp_gate[0]), bf(w_exp_up[0]), bf(w_exp_down[0])),
        ple_proj=bf(w_ple_proj), ple_gate=bf(w_ple_gate))


EVEN_SEGS = ("rope_dual", "plain", ("rope", "none"), ("rope", "none"), ("sigmoid",), "plain",
             "plain")
ODD_SEGS = (("rope",) * 16, ("rope",) * 8 + ("none",) * 8)


def _trunk(x, p, pos0, past, W, norms, sizes):
    b, t, d = x.shape
    n = b * t
    tm, tq, tk_sel, tk_sb, tm_moe, tq_moba, moba_bps = sizes
    norm_mix, norm_ffn, norm_ple, norm_final = norms
    pos = pos0 + jnp.arange(t, dtype=jnp.int32)
    tabs = _rope_tables(pos)
    if t < tm:
        tabs = [jnp.tile(a, (tm // t, 1)) for a in tabs]
    h = x.reshape(n, d)
    row = lambda a: a.reshape(1, d)
    b3 = lambda a: a.reshape(b, t, a.shape[-1])

    qp, qr, cmp_r, sel_r, win_r, gates, sbq, sbkv = _proj(
        h, row(norm_mix[0]), tabs, W["even"], EVEN_SEGS, tm)
    if past is None:
        n_pages = t // PAGE
        ident = jnp.arange(b * n_pages, dtype=jnp.int32).reshape(b, n_pages)
        kc, vc = _compress(cmp_r.reshape(b * n_pages, PAGE, 2 * LANES), ident,
                           W["w1k"], W["w1v"], W["w2p"], W["pe"], W["w1"], 8)
        sel_ctx, win_ctx, sb_ctx = b3(sel_r), b3(win_r), b3(sbkv)
        wpos0 = 0
        win_state = win_ctx[:, -NSA_WINDOW:]
    else:
        pt = past["page_table"]
        tail = lambda a: jnp.pad(b3(a), ((0, 0), (0, MOBA_BLOCK - t), (0, 0)))
        pool = lambda c: c[0].reshape(c.shape[1], PAGE, -1)
        kc, vc = _compress(pool(past["cache_nsa_cmp"]), pt,
                           W["w1k"], W["w1v"], W["w2p"], W["pe"], W["w1"], 8)
        sel_ctx = _gather_ctx(past["cache_nsa_sel"], pt, tail(sel_r), 8)
        sb_ctx = None
        state = past["state_nsa_win"][0].reshape(b, NSA_WINDOW, 2 * LANES)
        win_all = jnp.concatenate([state, b3(win_r)], axis=1)
        win_ctx = jnp.pad(win_all, ((0, 0), (0, LANES - t), (0, 0)))
        wpos0 = pos0 - NSA_WINDOW
        win_state = win_all[:, -NSA_WINDOW:]
    t_c = sel_ctx.shape[1]
    o_a = _nsa_attn(b3(qp), b3(qr), b3(gates), kc, vc, sel_ctx, win_ctx,
                    _overlap_matrix(kc.shape[1]), _expand_matrix(t_c, tk_sel),
                    tq, tk_sel, pos0, wpos0)
    tri = jnp.tril(jnp.ones((tk_sb, tk_sb), BF16), -1)
    if past is None:
        o_b, _ = _sb_attn(b3(sbq), sb_ctx, tri, tq, tk_sb, pos0)
    else:
        n_recent = min(SB_RECENT_PAGES, pt.shape[1])
        recent = _gather_ctx(past["cache_sb"], pt[:, -n_recent:], tail(sbkv), n_recent)
        o_recent, alive = _sb_attn(b3(sbq), recent, tri, tq, tk_sb, pos0,
                                   kpos0=pos0 - n_recent * PAGE)

        def whole_cache():
            ctx = _gather_ctx(past["cache_sb"], pt, tail(sbkv), 8)
            return _sb_attn(b3(sbq), ctx, tri, tq, tk_sb, pos0)[0]

        o_b = lax.cond(alive > 0.0, whole_cache, lambda: o_recent)
    h = _outproj(h, o_a.reshape(n, -1), o_b.reshape(n, -1), W["wo_a"], W["wo_b"], tm)
    h = _ffn(h, row(norm_ffn[0]), *W["ffn"], tm, W["ffn"][0].shape[1] // 2)
    h = _ple(h, p[0].reshape(n, -1), row(norm_ple[0]), row(norm_final),
             W["ple_gate"][0], W["ple_proj"][0], tm, False)

    mq, mkv = _proj(h, row(norm_mix[1]), tabs, W["odd"], ODD_SEGS, tm)
    if past is None:
        moba_ctx = b3(mkv)
    else:
        moba_ctx = _gather_ctx(past["cache_moba"], pt, tail(mkv), 4)
    if tq_moba % LANES == 0:
        o_c = _moba_attn_t(b3(mq), moba_ctx, tq_moba, pos0, moba_bps)
    else:
        o_c = _moba_attn(b3(mq), moba_ctx, tq_moba, pos0, moba_bps)
    half = o_c.shape[-1] // 2
    o_c = o_c.reshape(n, -1)
    h = _outproj(h, o_c[:, :half], o_c[:, half:], W["wo_c"][:half], W["wo_c"][half:], tm)
    h = _moe(h, row(norm_ffn[1]), W["wr_hi"], W["wr_lo"], *W["exp"], tm_moe, 896)
    y = _ple(h, p[1].reshape(n, -1), row(norm_ple[1]), row(norm_final),
             W["ple_gate"][1], W["ple_proj"][1], tm, True)

    kv5 = lambda a, heads: a.reshape(1, b, -1, 2, heads, HEAD_DIM)
    return (y.reshape(b, t, d), kv5(cmp_r, 2), kv5(sel_r, 2), kv5(win_state, 2),
            kv5(sbkv, SB_HEADS), kv5(mkv, MOBA_HEADS))


def kernel(x_prompt, x_sample, cache_nsa_cmp, cache_nsa_sel, state_nsa_win, cache_sb, cache_moba,
           page_table, p_prompt, p_sample, norm_mix, norm_ffn, norm_ple, norm_final, w_in_even,
           w_out_even, nsa_cmp_pe, nsa_cmp_w1, nsa_cmp_w2, w_ffn_gate, w_ffn_up, w_ffn_down,
           w_in_odd, w_out_odd, w_router, w_exp_gate, w_exp_up, w_exp_down, w_ple_proj,
           w_ple_gate):
    W = _prep_weights(w_in_even, w_out_even, nsa_cmp_pe, nsa_cmp_w1, nsa_cmp_w2, w_ffn_gate,
                      w_ffn_up, w_ffn_down, w_in_odd, w_out_odd, w_router, w_exp_gate, w_exp_up,
                      w_exp_down, w_ple_proj, w_ple_gate)
    norms = (norm_mix, norm_ffn, norm_ple, norm_final)
    past = dict(cache_nsa_cmp=cache_nsa_cmp, cache_nsa_sel=cache_nsa_sel,
                state_nsa_win=state_nsa_win, cache_sb=cache_sb, cache_moba=cache_moba,
                page_table=page_table)
    past_len = page_table.shape[1] * cache_sb.shape[2]
    t_dec = x_sample.shape[1]
    n_dec = x_sample.shape[0] * t_dec
    y_p, cmp_p, sel_p, win_p, sb_p, moba_p = _trunk(
        x_prompt, p_prompt, 0, None, W, norms, (256, 128, 512, 128, 1024, MOBA_BLOCK, 4))
    y_s, cmp_s, sel_s, win_s, sb_s, moba_s = _trunk(
        x_sample, p_sample, past_len, past, W, norms, (n_dec, t_dec, 256, 128, n_dec, t_dec, 8))
    return (y_p, y_s, cmp_p, cmp_s, sel_p, sel_s, win_p, win_s, sb_p, sb_s, moba_p, moba_s)
```

```python
import functools

import jax
import jax.numpy as jnp
from jax import lax
from jax.experimental import pallas as pl
from jax.experimental.pallas import tpu as pltpu

F32 = jnp.float32
BF16 = jnp.bfloat16

LANES = 128
HEAD_DIM = 64
PAGE = 128
RMS_EPS = 1e-6
ROPE_THETA = 10000.0
NEG = -1e30

NSA_HEADS = 8
NSA_GROUP = 4
NSA_STRIDE = 16
NSA_HID = 128
NSA_SEL_BLOCK = 64
NSA_PICKS = 13
NSA_WINDOW = 512
SB_HEADS = 8
MOBA_HEADS = 16
MOBA_BLOCK = 256
MOBA_TOPK = 3
N_EXPERTS = 8
SB_DEAD = -110.0
SB_RECENT_PAGES = 8

VMEM_LIMIT = 56 * 1024 * 1024


def _cparams(sem):
    return pltpu.CompilerParams(dimension_semantics=sem, vmem_limit_bytes=VMEM_LIMIT)


def _dot(a, b):
    return jnp.dot(a, b, preferred_element_type=F32)


def _dot_nt(a, b):
    return lax.dot_general(a, b, (((1,), (1,)), ((), ())), preferred_element_type=F32)


def _split(x):
    hi = x.astype(BF16)
    lo = (x - hi.astype(F32)).astype(BF16)
    return hi, lo


def _rmsnorm(x, g):
    return x * lax.rsqrt(jnp.mean(x * x, axis=-1, keepdims=True) + RMS_EPS) * g


def _sigmoid(x):
    return 1.0 / (1.0 + jnp.exp(-x))


def _rope_tile(y, cos, sa, sb):
    return y * cos + pltpu.roll(y, 32, 1) * sa + pltpu.roll(y, 96, 1) * sb


def _proj_kernel(x_ref, g_ref, cos_ref, sa_ref, sb_ref, *refs, segs):
    n_seg = len(segs)
    w_refs = refs[:n_seg]
    o_refs = list(refs[n_seg:])
    nb = _rmsnorm(x_ref[...], g_ref[...]).astype(BF16)
    cos, sa, sb = cos_ref[...], sa_ref[...], sb_ref[...]
    for w_ref, kinds in zip(w_refs, segs):
        y = _dot(nb, w_ref[...])
        if kinds == "plain":
            o_refs.pop(0)[...] = y
            continue
        if kinds == "rope_dual":
            o_refs.pop(0)[...] = y
            kinds = ("rope",) * (y.shape[1] // LANES)
        o_ref = o_refs.pop(0)
        for t, kind in enumerate(kinds):
            yt = y[:, t * LANES:(t + 1) * LANES]
            if kind == "rope":
                yt = _rope_tile(yt, cos, sa, sb)
            elif kind == "sigmoid":
                yt = _sigmoid(yt)
            o_ref[:, t * LANES:(t + 1) * LANES] = yt


def _proj(x, g, tabs, weights, segs, tm):
    n, d = x.shape
    nblk = tabs[0].shape[0] // tm
    out_shape, out_specs = [], []
    for w, kinds in zip(weights, segs):
        for _ in range(2 if kinds == "rope_dual" else 1):
            out_shape.append(jax.ShapeDtypeStruct((n, w.shape[1]), F32))
            out_specs.append(pl.BlockSpec((tm, w.shape[1]), lambda i: (i, 0)))
    tab_spec = pl.BlockSpec((tm, LANES), lambda i: (i % nblk, 0))
    return pl.pallas_call(
        functools.partial(_proj_kernel, segs=tuple(segs)),
        grid=(n // tm,),
        in_specs=[pl.BlockSpec((tm, d), lambda i: (i, 0)),
                  pl.BlockSpec((1, d), lambda i: (0, 0)),
                  tab_spec, tab_spec, tab_spec]
                 + [pl.BlockSpec(w.shape, lambda i: (0, 0)) for w in weights],
        out_specs=out_specs,
        out_shape=out_shape,
        compiler_params=_cparams(("parallel",)),
        name="norm_proj",
    )(x, g, *tabs, *weights)


def _outproj_kernel(h_ref, a_ref, b_ref, wa_ref, wb_ref, o_ref):
    o_ref[...] = (h_ref[...] + _dot(a_ref[...].astype(BF16), wa_ref[...])
                  + _dot(b_ref[...].astype(BF16), wb_ref[...]))


def _outproj(h, a, b, wa, wb, tm):
    n, d = h.shape
    row = lambda w: pl.BlockSpec((tm, w), lambda i: (i, 0))
    full = lambda w: pl.BlockSpec(w.shape, lambda i: (0, 0))
    return pl.pallas_call(
        _outproj_kernel,
        grid=(n // tm,),
        in_specs=[row(d), row(a.shape[1]), row(b.shape[1]), full(wa), full(wb)],
        out_specs=row(d),
        out_shape=jax.ShapeDtypeStruct((n, d), F32),
        compiler_params=_cparams(("parallel",)),
        name="out_proj",
    )(h, a, b, wa, wb)


def _ffn_kernel(h_ref, g_ref, wg_ref, wu_ref, wd_ref, o_ref, n_scr, acc_scr):
    f = pl.program_id(1)

    @pl.when(f == 0)
    def _():
        n_scr[...] = _rmsnorm(h_ref[...], g_ref[...]).astype(BF16)
        acc_scr[...] = jnp.zeros_like(acc_scr)

    nb = n_scr[...]
    gate = _dot(nb, wg_ref[...])
    up = _dot(nb, wu_ref[...])
    hid = (gate * _sigmoid(gate) * up).astype(BF16)
    acc_scr[...] += _dot(hid, wd_ref[...])

    @pl.when(f == pl.num_programs(1) - 1)
    def _():
        o_ref[...] = h_ref[...] + acc_scr[...]


def _ffn(h, g, wg, wu, wd, tm, tf):
    n, d = h.shape
    dff = wg.shape[1]
    return pl.pallas_call(
        _ffn_kernel,
        grid=(n // tm, dff // tf),
        in_specs=[pl.BlockSpec((tm, d), lambda i, f: (i, 0)),
                  pl.BlockSpec((1, d), lambda i, f: (0, 0)),
                  pl.BlockSpec((d, tf), lambda i, f: (0, f)),
                  pl.BlockSpec((d, tf), lambda i, f: (0, f)),
                  pl.BlockSpec((tf, d), lambda i, f: (f, 0))],
        out_specs=pl.BlockSpec((tm, d), lambda i, f: (i, 0)),
        out_shape=jax.ShapeDtypeStruct((n, d), F32),
        scratch_shapes=[pltpu.VMEM((tm, d), BF16), pltpu.VMEM((tm, d), F32)],
        compiler_params=_cparams(("parallel", "arbitrary")),
        name="swiglu_ffn",
    )(h, g, wg, wu, wd)


def _ple_kernel(h_ref, p_ref, g_ref, gf_ref, wg_ref, wp_ref, o_ref, *, final):
    h = h_ref[...]
    gate = _sigmoid(_dot(_rmsnorm(h, g_ref[...]).astype(BF16), wg_ref[...]))
    out = h + gate * _dot(p_ref[...].astype(BF16), wp_ref[...])
    if final:
        out = _rmsnorm(out, gf_ref[...])
    o_ref[...] = out


def _ple(h, p, g, gf, wg, wp, tm, final):
    n, d = h.shape
    row = lambda w: pl.BlockSpec((tm, w), lambda i: (i, 0))
    full = lambda a: pl.BlockSpec(a.shape, lambda i: (0, 0))
    return pl.pallas_call(
        functools.partial(_ple_kernel, final=final),
        grid=(n // tm,),
        in_specs=[row(d), row(p.shape[1]), full(g), full(gf), full(wg), full(wp)],
        out_specs=row(d),
        out_shape=jax.ShapeDtypeStruct((n, d), F32),
        compiler_params=_cparams(("parallel",)),
        name="ple",
    )(h, p, g, gf, wg, wp)


def _moe_kernel(h_ref, g_ref, wr_hi_ref, wr_lo_ref, wg_ref, wu_ref, wd_ref, o_ref,
                n_scr, gw_scr, acc_scr):
    e = pl.program_id(1)
    f = pl.program_id(2)
    lane = lax.broadcasted_iota(jnp.int32, (1, LANES), 1)

    @pl.when((e == 0) & (f == 0))
    def _():
        n = _rmsnorm(h_ref[...], g_ref[...])
        n_hi, n_lo = _split(n)
        n_scr[...] = n_hi
        logits = (_dot(n_hi, wr_hi_ref[...]) + _dot(n_lo, wr_hi_ref[...])
                  + _dot(n_hi, wr_lo_ref[...]))
        logits = jnp.where(lane < N_EXPERTS, logits, NEG)
        lane_f = lane.astype(F32)
        v1 = jnp.max(logits, axis=1, keepdims=True)
        i1 = jnp.min(jnp.where(logits == v1, lane_f, float(LANES)), axis=1, keepdims=True)
        rest = jnp.where(lane_f == i1, NEG, logits)
        v2 = jnp.max(rest, axis=1, keepdims=True)
        i2 = jnp.min(jnp.where(rest == v2, lane_f, float(LANES)), axis=1, keepdims=True)
        e2 = jnp.exp(v2 - v1)
        g1 = 1.0 / (1.0 + e2)
        gw_scr[...] = jnp.where(lane_f == i1, g1, jnp.where(lane_f == i2, e2 * g1, 0.0))
        acc_scr[...] = jnp.zeros_like(acc_scr)

    nb = n_scr[...]
    gate = _dot(nb, wg_ref[0])
    up = _dot(nb, wu_ref[0])
    hid = (gate * _sigmoid(gate) * up).astype(BF16)
    col = jnp.sum(jnp.where(lane == e, gw_scr[...], 0.0), axis=1, keepdims=True)
    acc_scr[...] += col * _dot(hid, wd_ref[0])

    @pl.when((e == pl.num_programs(1) - 1) & (f == pl.num_programs(2) - 1))
    def _():
        o_ref[...] = h_ref[...] + acc_scr[...]


def _moe(h, g, wr_hi, wr_lo, wg, wu, wd, tm, tf):
    n, d = h.shape
    n_e, _, dff = wg.shape
    return pl.pallas_call(
        _moe_kernel,
        grid=(n // tm, n_e, dff // tf),
        in_specs=[pl.BlockSpec((tm, d), lambda i, e, f: (i, 0)),
                  pl.BlockSpec((1, d), lambda i, e, f: (0, 0)),
                  pl.BlockSpec(wr_hi.shape, lambda i, e, f: (0, 0)),
                  pl.BlockSpec(wr_lo.shape, lambda i, e, f: (0, 0)),
                  pl.BlockSpec((1, d, tf), lambda i, e, f: (e, 0, f)),
                  pl.BlockSpec((1, d, tf), lambda i, e, f: (e, 0, f)),
                  pl.BlockSpec((1, tf, d), lambda i, e, f: (e, f, 0))],
        out_specs=pl.BlockSpec((tm, d), lambda i, e, f: (i, 0)),
        out_shape=jax.ShapeDtypeStruct((n, d), F32),
        scratch_shapes=[pltpu.VMEM((tm, d), BF16), pltpu.VMEM((tm, LANES), F32),
                        pltpu.VMEM((tm, d), F32)],
        compiler_params=_cparams(("parallel", "arbitrary", "arbitrary")),
        name="moe_ffn",
    )(h, g, wr_hi, wr_lo, wg, wu, wd)


def _gather_kernel(pt_ref, *refs, n_pg):
    del pt_ref
    page_refs, tail_ref, o_ref = refs[:n_pg], refs[n_pg], refs[n_pg + 1]
    s = pl.program_id(1)
    last = pl.num_programs(1) - 1

    n_heads = page_refs[0].shape[3]
    half = n_heads * HEAD_DIM

    @pl.when(s < last)
    def _():
        for i, p_ref in enumerate(page_refs):
            for kv in range(2):
                for pair in range(n_heads // 2):
                    tile = p_ref[0, 0, kv, 2 * pair:2 * pair + 2].reshape(LANES, PAGE)
                    c0 = kv * half + pair * LANES
                    o_ref[0, i * PAGE:(i + 1) * PAGE, c0:c0 + LANES] = tile.T.astype(o_ref.dtype)

    @pl.when(s == last)
    def _():
        o_ref[0, 0:tail_ref.shape[1], :] = tail_ref[0].astype(o_ref.dtype)


def _gather_ctx(cache, page_table, tail, n_pg):
    b, n_pages = page_table.shape
    n_heads = cache.shape[4]
    w = 2 * n_heads * HEAD_DIM
    n_tail = tail.shape[1]
    steps = n_pages // n_pg
    cache = cache.transpose(0, 1, 3, 4, 5, 2)

    def page_spec(i):
        return pl.BlockSpec(
            (1, 1, 2, n_heads, HEAD_DIM, PAGE),
            lambda bi, s, pt: (0, pt[bi, jnp.minimum(s, steps - 1) * n_pg + i], 0, 0, 0, 0))

    return pl.pallas_call(
        functools.partial(_gather_kernel, n_pg=n_pg),
        grid_spec=pltpu.PrefetchScalarGridSpec(
            num_scalar_prefetch=1,
            grid=(b, steps + 1),
            in_specs=[page_spec(i) for i in range(n_pg)]
                     + [pl.BlockSpec((1, n_tail, w), lambda bi, s, pt: (bi, 0, 0))],
            out_specs=pl.BlockSpec((1, n_pg * PAGE, w), lambda bi, s, pt: (bi, s, 0)),
        ),
        out_shape=jax.ShapeDtypeStruct((b, n_pages * PAGE + n_tail, w), BF16),
        compiler_params=_cparams(("parallel", "arbitrary")),
        name="paged_gather",
    )(page_table, *([cache] * n_pg), tail)


def _compress_kernel(pt_ref, *refs, n_pg):
    del pt_ref
    k_refs, v_refs = refs[:n_pg], refs[n_pg:2 * n_pg]
    (w1k_ref, w1v_ref, w2_ref, pe_ref, w1_ref, kc_ref, vc_ref,
     ak_scr, av_scr) = refs[2 * n_pg:]
    s = pl.program_id(1)
    n_ch = ak_scr.shape[0]
    per_page = PAGE // NSA_STRIDE

    for p_refs, a_scr in ((k_refs, ak_scr), (v_refs, av_scr)):
        for i, p_ref in enumerate(p_refs):
            base = pl.multiple_of((s * n_pg + i) * per_page, per_page)
            for j in range(NSA_STRIDE):
                a_scr[pl.ds(base, per_page), j * LANES:(j + 1) * LANES] = (
                    p_ref[0, pl.ds(j, per_page, stride=NSA_STRIDE), :])

    @pl.when(s == pl.num_programs(1) - 1)
    def _():
        for kv, (a_scr, w1p_ref, o_ref) in enumerate(((ak_scr, w1k_ref, kc_ref),
                                                      (av_scr, w1v_ref, vc_ref))):
            r = _dot(a_scr[...].astype(BF16), w1p_ref[...])
            bias = _dot(pe_ref[kv], w1_ref[kv])[0:1, :]
            hid = []
            for g in range(2):
                a = r[:, (2 * g) * NSA_HID:(2 * g + 1) * NSA_HID]
                b_next = pltpu.roll(r[:, (2 * g + 1) * NSA_HID:(2 * g + 2) * NSA_HID],
                                    n_ch - 1, 0)
                pre = a + b_next + bias
                hid.append(pre * _sigmoid(pre))
            hid = jnp.concatenate(hid, axis=1).astype(BF16)
            o_ref[0] = _dot(hid, w2_ref[kv])


def _compress(pool, page_table, w1k, w1v, w2, pe, w1, n_pg):
    b, n_pages = page_table.shape
    n_ch = n_pages * (PAGE // NSA_STRIDE)
    steps = n_pages // n_pg
    full = lambda a: pl.BlockSpec(a.shape, lambda bi, s, pt: (0,) * a.ndim)

    def page_spec(i, kv):
        return pl.BlockSpec((1, PAGE, LANES), lambda bi, s, pt: (pt[bi, s * n_pg + i], 0, kv))

    out_spec = pl.BlockSpec((1, n_ch, LANES), lambda bi, s, pt: (bi, 0, 0))
    return pl.pallas_call(
        functools.partial(_compress_kernel, n_pg=n_pg),
        grid_spec=pltpu.PrefetchScalarGridSpec(
            num_scalar_prefetch=1,
            grid=(b, steps),
            in_specs=[page_spec(i, kv) for kv in range(2) for i in range(n_pg)]
                     + [full(w1k), full(w1v), full(w2), full(pe), full(w1)],
            out_specs=[out_spec, out_spec],
            scratch_shapes=[pltpu.VMEM((n_ch, NSA_STRIDE * LANES), F32),
                            pltpu.VMEM((n_ch, NSA_STRIDE * LANES), F32)],
        ),
        out_shape=[jax.ShapeDtypeStruct((b, n_ch, LANES), F32)] * 2,
        compiler_params=_cparams(("parallel", "arbitrary")),
        name="nsa_compress",
    )(page_table, *([pool] * (2 * n_pg)), w1k, w1v, w2, pe, w1)


def _flash_step(state, s, valid, v):
    m, l, acc = state
    s = jnp.where(valid, s, NEG)
    m_new = jnp.maximum(m, jnp.max(s, axis=1, keepdims=True))
    alpha = jnp.exp(m - m_new)
    p = jnp.where(valid, jnp.exp(s - m_new), 0.0)
    l = alpha * l + jnp.sum(p, axis=1, keepdims=True)
    acc = alpha * acc + _dot(p.astype(BF16), v)
    return m_new, l, acc


def _flash_init(rows):
    return (jnp.full((rows, 1), NEG, F32), jnp.zeros((rows, 1), F32),
            jnp.zeros((rows, LANES), F32))


def _flash_out(state):
    _, l, acc = state
    return jnp.where(l > 0.0, acc / jnp.where(l > 0.0, l, 1.0), 0.0)


def _stack(x, times):
    return jnp.concatenate([x] * times, axis=0)


N_WIN_TILES = NSA_WINDOW // LANES + 1


def _nsa_kernel(qp_ref, qr_ref, gt_ref, kc_ref, vc_ref, sel_ref, ov_ref, ex_ref, *refs,
                tq, tk, pos0, wpos0):
    win_refs, o_ref = refs[:N_WIN_TILES], refs[N_WIN_TILES]
    i = pl.program_id(1)
    q0 = pos0 + i * tq
    scale = HEAD_DIM ** -0.5
    r4 = NSA_GROUP
    qpos = q0 + lax.broadcasted_iota(jnp.int32, (tq, 1), 0)
    qpos4 = _stack(qpos, r4)
    qblk = qpos // NSA_SEL_BLOCK
    lane = lax.broadcasted_iota(jnp.int32, (1, LANES), 1)
    lane_f = lane.astype(F32)
    n_ch = kc_ref.shape[1]
    c_end = lax.broadcasted_iota(jnp.int32, (1, n_ch), 1) * NSA_STRIDE + (2 * NSA_STRIDE - 1)
    kc = kc_ref[0].astype(BF16)
    vc = vc_ref[0].astype(BF16)
    gates = gt_ref[0]
    n_sel_tiles = (q0 + tq - 1) // tk + 1
    win_start = (q0 // LANES) * LANES - NSA_WINDOW

    q_rot, o_cmp, picked = [], [], []
    for g in range(2):
        heads = range(g * r4, (g + 1) * r4)
        q_c = jnp.concatenate([qp_ref[0, :, h * LANES:(h + 1) * LANES] for h in heads], axis=0)
        q_r = jnp.concatenate([qr_ref[0, :, h * LANES:(h + 1) * LANES] for h in heads], axis=0)
        q_c = (q_c * scale).astype(BF16)
        q_rot.append((q_r * scale).astype(BF16))

        s = _dot_nt(q_c, kc)
        valid = c_end <= qpos4
        s = jnp.where(valid, s, NEG)
        e = jnp.where(valid, jnp.exp(s - jnp.max(s, axis=1, keepdims=True)), 0.0)
        l = jnp.sum(e, axis=1, keepdims=True)
        p = jnp.where(l > 0.0, e / jnp.where(l > 0.0, l, 1.0), 0.0)
        o_cmp.append(_dot(p.astype(BF16), vc))

        p_sum = p[0:tq]
        for r in range(1, r4):
            p_sum = p_sum + p[r * tq:(r + 1) * tq]
        p_hi, p_lo = _split(p_sum)
        imp = _dot(p_hi, ov_ref[...]) + _dot(p_lo, ov_ref[...])

        cand = (lane <= qblk) & (lane != 0) & (lane != qblk) & (lane != qblk - 1)
        work = jnp.where(cand, imp, -1.0)
        pick = jnp.zeros((tq, LANES), F32)
        for _ in range(NSA_PICKS):
            top = jnp.max(work, axis=1, keepdims=True)
            first = jnp.min(jnp.where(work == top, lane_f, float(LANES)), axis=1,
                            keepdims=True)
            hit = (lane_f == first) & (top >= 0.0)
            pick = jnp.where(hit, 1.0, pick)
            work = jnp.where(hit, -1.0, work)
        picked.append(pick.astype(BF16))

    def sel_body(t, states):
        k0 = pl.multiple_of(t * tk, tk)
        k = sel_ref[0, pl.ds(k0, tk), 0:LANES].astype(BF16)
        v = sel_ref[0, pl.ds(k0, tk), LANES:2 * LANES].astype(BF16)
        kpos = k0 + lax.broadcasted_iota(jnp.int32, (1, tk), 1)
        kblk = kpos // NSA_SEL_BLOCK
        forced = (kblk == 0) | (kblk == qblk) | (kblk == qblk - 1)
        out = ()
        for g in range(2):
            chosen = _dot(picked[g], ex_ref[t]) > 0.5
            keep = jnp.where((chosen | forced) & (kpos <= qpos), 1.0, 0.0)
            out += _flash_step(states[3 * g:3 * g + 3], _dot_nt(q_rot[g], k),
                               _stack(keep, r4) > 0.5, v)
        return out

    sel_states = lax.fori_loop(0, n_sel_tiles, sel_body, _flash_init(r4 * tq) * 2)
    o_sel = [_flash_out(sel_states[0:3]), _flash_out(sel_states[3:6])]

    win_states = [_flash_init(r4 * tq), _flash_init(r4 * tq)]
    for jt, w_ref in enumerate(win_refs):
        k = w_ref[0, :, 0:LANES].astype(BF16)
        v = w_ref[0, :, LANES:2 * LANES].astype(BF16)
        wpos = win_start + jt * LANES + lane
        keep = jnp.where((wpos >= wpos0) & (wpos <= qpos) & (qpos - wpos < NSA_WINDOW),
                         1.0, 0.0)
        keep4 = _stack(keep, r4) > 0.5
        for g in range(2):
            win_states[g] = _flash_step(win_states[g], _dot_nt(q_rot[g], k), keep4, v)
    o_win = [_flash_out(win_states[0]), _flash_out(win_states[1])]

    outs = []
    for g in range(2):
        for r in range(r4):
            h = g * r4 + r
            rows = slice(r * tq, (r + 1) * tq)
            o = (gates[:, 3 * h:3 * h + 1] * o_cmp[g][rows]
                 + gates[:, 3 * h + 1:3 * h + 2] * o_sel[g][rows]
                 + gates[:, 3 * h + 2:3 * h + 3] * o_win[g][rows])
            if h % 2 != g:
                o = pltpu.roll(o, HEAD_DIM, 1)
            outs.append(o)

    for pair in range(NSA_HEADS // 2):
        o_ref[0, :, pair * LANES:(pair + 1) * LANES] = jnp.where(
            lane < HEAD_DIM, outs[2 * pair], outs[2 * pair + 1])


def _nsa_t_kernel(qp_ref, qr_ref, gt_ref, kc_ref, vc_ref, sel_ref, ovt_ref, ext_ref, *refs,
                  tq, tk, pos0, wpos0):
    win_refs, o_ref, svt_scr, vct_scr = (refs[:N_WIN_TILES], refs[N_WIN_TILES],
                                         refs[N_WIN_TILES + 1], refs[N_WIN_TILES + 2])
    i = pl.program_id(1)
    q0 = pos0 + i * tq
    scale = HEAD_DIM ** -0.5
    r4 = NSA_GROUP
    n_ch = kc_ref.shape[1]
    row = lax.broadcasted_iota(jnp.int32, (LANES, 1), 0)
    row_f = row.astype(F32)
    qpos = q0 + lax.broadcasted_iota(jnp.int32, (1, tq), 1)
    qblk = qpos // NSA_SEL_BLOCK
    n_sel_tiles = (q0 + tq - 1) // tk + 1
    win_start = (q0 // LANES) * LANES - NSA_WINDOW
    own_half = [row < HEAD_DIM, row >= HEAD_DIM]

    def lanes4(x):
        return jnp.concatenate([x] * r4, axis=1)

    def v_t(tiles):
        return jnp.concatenate([t.T for t in tiles], axis=1)

    @pl.when(i == 0)
    def _():
        def transpose_tile(t, carry):
            k0 = pl.multiple_of(t * tk, tk)
            vt = v_t([sel_ref[0, pl.ds(k0 + j * LANES, LANES), LANES:2 * LANES]
                      for j in range(tk // LANES)])
            for g in range(2):
                svt_scr[g, t] = jnp.where(own_half[g], vt, 1.0).astype(BF16)
            return carry

        lax.fori_loop(0, sel_ref.shape[1] // tk, transpose_tile, 0)
        vct = v_t([vc_ref[0, j * LANES:(j + 1) * LANES, :] for j in range(n_ch // LANES)])
        for g in range(2):
            vct_scr[g] = jnp.where(own_half[g], vct, 1.0).astype(BF16)

    kc = kc_ref[0].astype(BF16)
    c_end = (lax.broadcasted_iota(jnp.int32, (n_ch, 1), 0) * NSA_STRIDE
             + (2 * NSA_STRIDE - 1))
    cmp_bias = lanes4(jnp.where(c_end <= qpos, 0.0, NEG))

    q_rot, o_cmp, picked = [], [], []
    for g in range(2):
        heads = range(g * r4, (g + 1) * r4)
        q_c = jnp.concatenate([qp_ref[0, :, h * LANES:(h + 1) * LANES] for h in heads], axis=0)
        q_r = jnp.concatenate([qr_ref[0, :, h * LANES:(h + 1) * LANES] for h in heads], axis=0)
        q_c = (q_c * scale).astype(BF16)
        q_rot.append((q_r * scale).astype(BF16))

        s = _dot_nt(kc, q_c) + cmp_bias
        m = jnp.max(s, axis=0, keepdims=True)
        e = jnp.exp(s - m)
        p = jnp.where(m > 0.5 * NEG, e / jnp.sum(e, axis=0, keepdims=True), 0.0)
        o_cmp.append(_dot(vct_scr[g], p.astype(BF16)))

        p_sum = p[:, 0:tq]
        for r in range(1, r4):
            p_sum = p_sum + p[:, r * tq:(r + 1) * tq]
        p_hi, p_lo = _split(p_sum)
        imp = _dot(ovt_ref[...], p_hi) + _dot(ovt_ref[...], p_lo)

        cand = (row <= qblk) & (row != 0) & (row != qblk) & (row != qblk - 1)
        work = jnp.where(cand, imp, -1.0)
        pick = jnp.zeros((LANES, tq), F32)
        for _ in range(NSA_PICKS):
            top = jnp.max(work, axis=0, keepdims=True)
            first = jnp.min(jnp.where(work == top, row_f, float(LANES)), axis=0, keepdims=True)
            hit = (row_f == first) & (top >= 0.0)
            pick = jnp.where(hit, 1.0, pick)
            work = jnp.where(hit, -1.0, work)
        picked.append(pick.astype(BF16))

    def flash_t(state, s, vt):
        m, acc = state
        m_new = jnp.maximum(m, jnp.max(s, axis=0, keepdims=True))
        p = jnp.exp(s - m_new).astype(BF16)
        return m_new, jnp.exp(m - m_new) * acc + _dot(vt, p)

    def init_t():
        return (jnp.full((1, r4 * tq), NEG, F32), jnp.zeros((LANES, r4 * tq), F32))

    def sel_body(t, states):
        k0 = pl.multiple_of(t * tk, tk)
        k = sel_ref[0, pl.ds(k0, tk), 0:LANES].astype(BF16)
        kpos = k0 + lax.broadcasted_iota(jnp.int32, (tk, 1), 0)
        kblk = kpos // NSA_SEL_BLOCK
        forced = (kblk == 0) | (kblk == qblk) | (kblk == qblk - 1)
        causal = kpos <= qpos
        out = ()
        for g in range(2):
            chosen = _dot(ext_ref[t], picked[g]) > 0.5
            bias = lanes4(jnp.where((chosen | forced) & causal, 0.0, NEG))
            out += flash_t(states[2 * g:2 * g + 2], _dot_nt(k, q_rot[g]) + bias, svt_scr[g, t])
        return out

    sel_states = lax.fori_loop(0, n_sel_tiles, sel_body, init_t() * 2)

    win_states = [init_t(), init_t()]
    for jt, w_ref in enumerate(win_refs):
        k = w_ref[0, :, 0:LANES].astype(BF16)
        vt = w_ref[0, :, LANES:2 * LANES].T
        wpos = win_start + jt * LANES + row
        keep = (wpos >= wpos0) & (wpos <= qpos) & (qpos - wpos < NSA_WINDOW)
        bias = lanes4(jnp.where(keep, 0.0, NEG))
        for g in range(2):
            win_states[g] = flash_t(win_states[g], _dot_nt(k, q_rot[g]) + bias,
                                    jnp.where(own_half[g], vt, 1.0).astype(BF16))

    gates_t = gt_ref[0].T
    for g in range(2):
        ones_row = HEAD_DIM * (1 - g)
        o_sel = sel_states[2 * g + 1] / sel_states[2 * g + 1][ones_row:ones_row + 1, :]
        o_win = win_states[g][1] / win_states[g][1][ones_row:ones_row + 1, :]
        mixed = []
        for r in range(r4):
            h = g * r4 + r
            cols = slice(r * tq, (r + 1) * tq)
            o = (gates_t[3 * h:3 * h + 1, :] * o_cmp[g][:, cols]
                 + gates_t[3 * h + 1:3 * h + 2, :] * o_sel[:, cols]
                 + gates_t[3 * h + 2:3 * h + 3, :] * o_win[:, cols])
            mixed.append(o[g * HEAD_DIM:(g + 1) * HEAD_DIM, :])
        for pr in range(r4 // 2):
            pair = g * (r4 // 2) + pr
            tile = jnp.concatenate([mixed[2 * pr], mixed[2 * pr + 1]], axis=0)
            o_ref[0, :, pair * LANES:(pair + 1) * LANES] = tile.T


def _nsa_attn(qp, qr, gates, kc, vc, sel_ctx, win_ctx, ov, ex, tq, tk, pos0, wpos0):
    b, t_q, _ = qp.shape
    t_c = sel_ctx.shape[1]
    n_win_tiles_total = win_ctx.shape[1] // LANES
    tile0 = pos0 // LANES - NSA_WINDOW // LANES - wpos0 // LANES
    transposed = tq % LANES == 0
    if transposed:
        body, ov, ex = _nsa_t_kernel, ov.T, ex.transpose(0, 2, 1)
        scratch = [pltpu.VMEM((2, t_c // tk, LANES, tk), BF16),
                   pltpu.VMEM((2, LANES, kc.shape[1]), BF16)]
    else:
        body, scratch = _nsa_kernel, []

    def win_spec(jt):
        def idx(bi, i):
            t = tile0 + (i * tq) // LANES + jt
            return (bi, jnp.clip(t, 0, n_win_tiles_total - 1), 0)
        return pl.BlockSpec((1, LANES, 2 * LANES), idx)

    qspec = pl.BlockSpec((1, tq, NSA_HEADS * LANES), lambda bi, i: (bi, i, 0))
    return pl.pallas_call(
        functools.partial(body, tq=tq, tk=tk, pos0=pos0, wpos0=wpos0),
        grid=(b, t_q // tq),
        in_specs=[qspec, qspec,
                  pl.BlockSpec((1, tq, LANES), lambda bi, i: (bi, i, 0)),
                  pl.BlockSpec((1,) + kc.shape[1:], lambda bi, i: (bi, 0, 0)),
                  pl.BlockSpec((1,) + vc.shape[1:], lambda bi, i: (bi, 0, 0)),
                  pl.BlockSpec((1, t_c, 2 * LANES), lambda bi, i: (bi, 0, 0)),
                  pl.BlockSpec(ov.shape, lambda bi, i: (0, 0)),
                  pl.BlockSpec(ex.shape, lambda bi, i: (0, 0, 0))]
                 + [win_spec(jt) for jt in range(N_WIN_TILES)],
        out_specs=pl.BlockSpec((1, tq, NSA_HEADS * HEAD_DIM), lambda bi, i: (bi, i, 0)),
        out_shape=jax.ShapeDtypeStruct((b, t_q, NSA_HEADS * HEAD_DIM), F32),
        scratch_shapes=scratch,
        compiler_params=_cparams(("parallel", "arbitrary")),
        name="nsa_attn_t" if transposed else "nsa_attn",
    )(qp, qr, gates, kc, vc, sel_ctx, ov, ex, *([win_ctx] * N_WIN_TILES))


def _sb_kernel(q_ref, k_ref, v_ref, tri_ref, o_ref, alive_ref, *, tq, tk, pos0, kpos0):
    i = pl.program_id(2)
    q0 = pos0 + i * tq
    scale = HEAD_DIM ** -0.5
    qpos = q0 + lax.broadcasted_iota(jnp.int32, (tq, 1), 0)
    lane = lax.broadcasted_iota(jnp.int32, (1, LANES), 1)
    n_tiles = (q0 + tq - 2 - kpos0) // tk + 1
    tri = tri_ref[...]
    qs = [(q_ref[0, :, hh * LANES:(hh + 1) * LANES] * scale).astype(BF16) for hh in range(2)]

    def tile(q, k, v, valid, acc, run):
        z = _dot_nt(q, k)
        soft = jnp.log(1.0 + jnp.exp(-jnp.abs(z)))
        log_break = jnp.minimum(z, 0.0) - soft
        log_stay = jnp.where(valid, log_break - z, 0.0)
        s_hi, s_lo = _split(log_stay)
        after = _dot(s_hi, tri) + _dot(s_lo, tri)
        a = jnp.where(valid, jnp.exp(log_break + after + run), 0.0)
        return acc + _dot(a.astype(BF16), v), run + jnp.sum(log_stay, axis=1, keepdims=True)

    def cond(carry):
        return (carry[0] < n_tiles) & (carry[1] > 0)

    def body(carry):
        step, _, acc0, run0, acc1, run1 = carry
        k0 = pl.multiple_of((n_tiles - 1 - step) * tk, tk)
        k = k_ref[0, pl.ds(k0, tk), :].astype(BF16)
        v = v_ref[0, pl.ds(k0, tk), :].astype(BF16)
        valid = (kpos0 + k0 + lax.broadcasted_iota(jnp.int32, (1, tk), 1)) < qpos
        acc0, run0 = tile(qs[0], k, v, valid, acc0, run0)
        acc1, run1 = tile(qs[1], k, v, valid, acc1, run1)
        live = jnp.maximum(jnp.max(run0), jnp.max(run1)) > SB_DEAD
        return step + 1, live.astype(jnp.int32), acc0, run0, acc1, run1

    zero_acc = jnp.zeros((tq, LANES), F32)
    zero_run = jnp.zeros((tq, 1), F32)
    _, live, acc0, _, acc1, _ = lax.while_loop(
        cond, body, (jnp.int32(0), jnp.int32(1), zero_acc, zero_run, zero_acc, zero_run))
    o_ref[0] = jnp.where(lane < HEAD_DIM, acc0, acc1)
    alive_ref[...] = jnp.zeros(alive_ref.shape, F32) + live.astype(F32)


def _sb_attn(q, kv, tri, tq, tk, pos0, kpos0=0):
    b, t_q, _ = q.shape
    t_c = kv.shape[1]
    n_pairs = SB_HEADS // 2
    n_q = t_q // tq
    out, alive = pl.pallas_call(
        functools.partial(_sb_kernel, tq=tq, tk=tk, pos0=pos0, kpos0=kpos0),
        grid=(b, n_pairs, n_q),
        in_specs=[pl.BlockSpec((1, tq, 2 * LANES), lambda bi, pr, i: (bi, i, pr)),
                  pl.BlockSpec((1, t_c, LANES), lambda bi, pr, i: (bi, 0, pr)),
                  pl.BlockSpec((1, t_c, LANES), lambda bi, pr, i: (bi, 0, n_pairs + pr)),
                  pl.BlockSpec(tri.shape, lambda bi, pr, i: (0, 0))],
        out_specs=[pl.BlockSpec((1, tq, LANES), lambda bi, pr, i: (bi, i, pr)),
                   pl.BlockSpec((1, 1, 1, 8, LANES), lambda bi, pr, i: (bi, pr, i, 0, 0))],
        out_shape=[jax.ShapeDtypeStruct((b, t_q, SB_HEADS * HEAD_DIM), F32),
                   jax.ShapeDtypeStruct((b, n_pairs, n_q, 8, LANES), F32)],
        compiler_params=_cparams(("parallel", "parallel", "arbitrary")),
        name="sb_attn",
    )(q, kv, kv, tri)
    return out, jnp.max(alive)


def _moba_kernel(q_ref, k_ref, v_ref, o_ref, km_scr, *, tq, pos0, n_blocks, bps):
    i = pl.program_id(2)
    blk = MOBA_BLOCK
    q0 = pos0 + i * tq
    own = q0 // blk
    scale = HEAD_DIM ** -0.5
    qpos = q0 + lax.broadcasted_iota(jnp.int32, (tq, 1), 0)
    lane = lax.broadcasted_iota(jnp.int32, (1, LANES), 1)
    lane_f = lane.astype(F32)

    @pl.when(i == 0)
    def _():
        km_scr[...] = jnp.zeros_like(km_scr)
        for n in range(n_blocks):
            km_scr[n:n + 1, :] = jnp.sum(k_ref[0, n * blk:(n + 1) * blk, :].astype(F32),
                                         axis=0, keepdims=True) * (1.0 / blk)

    km_hi, km_lo = _split(km_scr[...])
    q_s, picked, states = [], [], []
    own0 = pl.multiple_of(own * blk, blk)
    k_own = k_ref[0, pl.ds(own0, blk), :].astype(BF16)
    v_own = v_ref[0, pl.ds(own0, blk), :].astype(BF16)
    causal = (own0 + lax.broadcasted_iota(jnp.int32, (1, blk), 1)) <= qpos
    for hh in range(2):
        q = q_ref[0, :, hh * LANES:(hh + 1) * LANES]
        q_hi, q_lo = _split(q)
        gate = _dot_nt(q_hi, km_hi) + _dot_nt(q_lo, km_hi) + _dot_nt(q_hi, km_lo)
        work = jnp.where(lane < own, gate, NEG)
        pick = jnp.zeros((tq, LANES), F32)
        for _ in range(MOBA_TOPK):
            top = jnp.max(work, axis=1, keepdims=True)
            first = jnp.min(jnp.where(work == top, lane_f, float(LANES)), axis=1,
                            keepdims=True)
            hit = (lane_f == first) & (top > 0.5 * NEG)
            pick = jnp.where(hit, 1.0, pick)
            work = jnp.where(hit, NEG, work)
        picked.append(pick)
        q_s.append((q * scale).astype(BF16))
        states.append(_flash_step(_flash_init(tq), _dot_nt(q_s[hh], k_own), causal, v_own))

    def body(t, carry):
        k0 = pl.multiple_of(t * (bps * blk), bps * blk)
        k = k_ref[0, pl.ds(k0, bps * blk), :].astype(BF16)
        v = v_ref[0, pl.ds(k0, bps * blk), :].astype(BF16)
        out = []
        for hh in range(2):
            m, l, acc = carry[3 * hh:3 * hh + 3]
            s = _dot_nt(q_s[hh], k)
            rows = [jnp.sum(jnp.where(lane == t * bps + j, picked[hh], 0.0), axis=1,
                            keepdims=True) > 0.5 for j in range(bps)]
            parts = [s[:, j * blk:(j + 1) * blk] for j in range(bps)]
            m_new = m
            for j in range(bps):
                m_new = jnp.maximum(m_new, jnp.where(
                    rows[j], jnp.max(parts[j], axis=1, keepdims=True), NEG))
            alpha = jnp.exp(m - m_new)
            p = [jnp.exp(parts[j] - jnp.where(rows[j], m_new, -NEG)) for j in range(bps)]
            p = p[0] if bps == 1 else jnp.concatenate(p, axis=1)
            l = alpha * l + jnp.sum(p, axis=1, keepdims=True)
            acc = alpha * acc + _dot(p.astype(BF16), v)
            out += [m_new, l, acc]
        return tuple(out)

    n_steps = (own + bps - 1) // bps
    final = lax.fori_loop(0, n_steps, body, tuple(states[0]) + tuple(states[1]))
    o_ref[0] = jnp.where(lane < HEAD_DIM, _flash_out(final[0:3]), _flash_out(final[3:6]))


def _moba_t_kernel(q_ref, k_ref, v_ref, o_ref, km_scr, vt_scr, pk_scr, *, tq, pos0, n_blocks,
                   bps):
    i = pl.program_id(2)
    blk = MOBA_BLOCK
    sub = blk // LANES
    q0 = pos0 + i * tq
    own = q0 // blk
    scale = HEAD_DIM ** -0.5
    row = lax.broadcasted_iota(jnp.int32, (LANES, 1), 0)
    row_f = row.astype(F32)
    qpos = q0 + lax.broadcasted_iota(jnp.int32, (1, tq), 1)

    @pl.when(i == 0)
    def _():
        km_scr[...] = jnp.zeros_like(km_scr)
        for n in range(n_blocks):
            km_scr[n:n + 1, :] = jnp.sum(k_ref[0, n * blk:(n + 1) * blk, :], axis=0,
                                         keepdims=True) * (1.0 / blk)

        def transpose_block(n, carry):
            n0 = pl.multiple_of(n * blk, blk)
            vt = jnp.concatenate(
                [v_ref[0, pl.ds(n0 + j * LANES, LANES), :].T for j in range(sub)], axis=1)
            vt_scr[0, n] = jnp.where(row < HEAD_DIM, vt, 1.0).astype(BF16)
            vt_scr[1, n] = jnp.where(row >= HEAD_DIM, vt, 1.0).astype(BF16)
            return carry

        lax.fori_loop(0, n_blocks, transpose_block, 0)

    ones_rows = 16
    v_rows = [(0, HEAD_DIM + ones_rows), (HEAD_DIM - ones_rows, 2 * HEAD_DIM)]

    def weighted_values(hh, t, p):
        lo, hi = v_rows[hh]
        return _dot(vt_scr[hh, t, lo:hi, :], p)

    km_hi, km_lo = _split(km_scr[...])
    own0 = pl.multiple_of(own * blk, blk)
    k_own = k_ref[0, pl.ds(own0, blk), :].astype(BF16)
    causal = (own0 + lax.broadcasted_iota(jnp.int32, (blk, 1), 0)) <= qpos
    q_s, states = [], []
    for hh in range(2):
        q = q_ref[0, :, hh * LANES:(hh + 1) * LANES]
        q_hi, q_lo = _split(q)
        gate = _dot_nt(km_hi, q_hi) + _dot_nt(km_hi, q_lo) + _dot_nt(km_lo, q_hi)
        work = jnp.where(row < own, gate, NEG)
        pick = jnp.zeros((LANES, tq), F32)
        for _ in range(MOBA_TOPK):
            top = jnp.max(work, axis=0, keepdims=True)
            first = jnp.min(jnp.where(work == top, row_f, float(LANES)), axis=0, keepdims=True)
            hit = (row_f == first) & (top > 0.5 * NEG)
            pick = jnp.where(hit, 1.0, pick)
            work = jnp.where(hit, NEG, work)
        pk_scr[hh] = pick
        q_s.append((q * scale).astype(BF16))
        s = jnp.where(causal, _dot_nt(k_own, q_s[hh]), NEG)
        m = jnp.max(s, axis=0, keepdims=True)
        p = jnp.where(causal, jnp.exp(s - m), 0.0).astype(BF16)
        states += [m, weighted_values(hh, own, p)]

    def body(t, carry):
        k0 = pl.multiple_of(t * (bps * blk), bps * blk)
        k = k_ref[0, pl.ds(k0, bps * blk), :].astype(BF16)
        out = []
        for hh in range(2):
            m, acc = carry[2 * hh:2 * hh + 2]
            s = _dot_nt(k, q_s[hh])
            parts = [s[j * blk:(j + 1) * blk] for j in range(bps)]
            chosen = [pk_scr[hh, pl.ds(t * bps + j, 1), :] > 0.5 for j in range(bps)]
            m_new = m
            for j in range(bps):
                m_new = jnp.maximum(m_new, jnp.where(
                    chosen[j], jnp.max(parts[j], axis=0, keepdims=True), NEG))
            acc = jnp.exp(m - m_new) * acc
            for j in range(bps):
                p = jnp.exp(parts[j] - jnp.where(chosen[j], m_new, -NEG)).astype(BF16)
                acc = acc + weighted_values(hh, t * bps + j, p)
            out += [m_new, acc]
        return tuple(out)

    _, acc0, _, acc1 = lax.fori_loop(0, (own + bps - 1) // bps, body, tuple(states))
    o_t = jnp.concatenate(
        [acc0[0:HEAD_DIM] / acc0[HEAD_DIM:HEAD_DIM + 1, :],
         acc1[ones_rows:ones_rows + HEAD_DIM] / acc1[0:1, :]], axis=0)
    o_ref[0] = o_t.T


def _moba_attn_t(q, kv, tq, pos0, bps):
    b, t_q, _ = q.shape
    t_c = kv.shape[1]
    n_pairs = MOBA_HEADS // 2
    n_blocks = t_c // MOBA_BLOCK
    assert MOBA_BLOCK % tq == 0 and pos0 % tq == 0 and tq % LANES == 0
    assert n_blocks % bps == 0
    return pl.pallas_call(
        functools.partial(_moba_t_kernel, tq=tq, pos0=pos0, n_blocks=n_blocks, bps=bps),
        grid=(b, n_pairs, t_q // tq),
        in_specs=[pl.BlockSpec((1, tq, 2 * LANES), lambda bi, pr, i: (bi, i, pr)),
                  pl.BlockSpec((1, t_c, LANES), lambda bi, pr, i: (bi, 0, pr)),
                  pl.BlockSpec((1, t_c, LANES), lambda bi, pr, i: (bi, 0, n_pairs + pr))],
        out_specs=pl.BlockSpec((1, tq, LANES), lambda bi, pr, i: (bi, i, pr)),
        out_shape=jax.ShapeDtypeStruct((b, t_q, MOBA_HEADS * HEAD_DIM), F32),
        scratch_shapes=[pltpu.VMEM((LANES, LANES), F32),
                        pltpu.VMEM((2, n_blocks, LANES, MOBA_BLOCK), BF16),
                        pltpu.VMEM((2, LANES, tq), F32)],
        compiler_params=_cparams(("parallel", "parallel", "arbitrary")),
        name="moba_attn_t",
    )(q, kv, kv)


def _moba_attn(q, kv, tq, pos0, bps):
    b, t_q, _ = q.shape
    t_c = kv.shape[1]
    n_pairs = MOBA_HEADS // 2
    assert MOBA_BLOCK % tq == 0 and pos0 % tq == 0 and (pos0 // MOBA_BLOCK) % bps == 0
    return pl.pallas_call(
        functools.partial(_moba_kernel, tq=tq, pos0=pos0, n_blocks=t_c // MOBA_BLOCK, bps=bps),
        grid=(b, n_pairs, t_q // tq),
        in_specs=[pl.BlockSpec((1, tq, 2 * LANES), lambda bi, pr, i: (bi, i, pr)),
                  pl.BlockSpec((1, t_c, LANES), lambda bi, pr, i: (bi, 0, pr)),
                  pl.BlockSpec((1, t_c, LANES), lambda bi, pr, i: (bi, 0, n_pairs + pr))],
        out_specs=pl.BlockSpec((1, tq, LANES), lambda bi, pr, i: (bi, i, pr)),
        out_shape=jax.ShapeDtypeStruct((b, t_q, MOBA_HEADS * HEAD_DIM), F32),
        scratch_shapes=[pltpu.VMEM((LANES, LANES), F32)],
        compiler_params=_cparams(("parallel", "parallel", "arbitrary")),
        name="moba_attn",
    )(q, kv, kv)


def _pad_heads(w, halves):
    d = w.shape[0]
    n_h = len(halves)
    onehot = jax.nn.one_hot(jnp.asarray(halves), 2, dtype=w.dtype)
    return jnp.einsum("dhe,hs->dhse", w.reshape(d, n_h, HEAD_DIM), onehot).reshape(d, n_h * LANES)


def _rope_tables(pos):
    half = HEAD_DIM // 2
    inv = 1.0 / (ROPE_THETA ** (jnp.arange(half, dtype=F32) / half))
    ang = pos.astype(F32)[:, None] * inv[None, :]
    cos = jnp.tile(jnp.cos(ang), (1, LANES // half))
    sin = jnp.tile(jnp.sin(ang), (1, LANES // half))
    upper = (jnp.arange(LANES) % HEAD_DIM) >= half
    return cos, jnp.where(upper, sin, 0.0), jnp.where(upper, 0.0, -sin)


def _overlap_matrix(n_ch):
    c = jnp.arange(n_ch)[:, None]
    n = jnp.arange(LANES)[None, :]
    return ((c >= 4 * n - 1) & (c <= 4 * n + 3)).astype(BF16)


def _expand_matrix(t_c, tk):
    blk = (jnp.arange(t_c) // NSA_SEL_BLOCK).reshape(t_c // tk, 1, tk)
    return (jnp.arange(LANES)[None, :, None] == blk).astype(BF16)


def _compress_weights(w1, w2):
    half = NSA_STRIDE * HEAD_DIM
    out = []
    for kv in range(2):
        wa = w1[kv, :half].reshape(NSA_STRIDE, HEAD_DIM, NSA_HID)
        wb = w1[kv, half:].reshape(NSA_STRIDE, HEAD_DIM, NSA_HID)
        ab = jnp.concatenate([wa, wb], axis=2)
        z = jnp.zeros_like(ab)
        g0 = jnp.concatenate([ab, z], axis=1)
        g1 = jnp.concatenate([z, ab], axis=1)
        out.append(jnp.concatenate([g0, g1], axis=2).reshape(NSA_STRIDE * LANES, 4 * NSA_HID))
    z2 = jnp.zeros_like(w2)
    w2p = jnp.concatenate([jnp.concatenate([w2, z2], axis=2),
                           jnp.concatenate([z2, w2], axis=2)], axis=1)
    return out[0].astype(BF16), out[1].astype(BF16), w2p.astype(BF16)


def _prep_weights(w_in_even, w_out_even, nsa_cmp_pe, nsa_cmp_w1, nsa_cmp_w2, w_ffn_gate,
                  w_ffn_up, w_ffn_down, w_in_odd, w_out_odd, w_router, w_exp_gate, w_exp_up,
                  w_exp_down, w_ple_proj, w_ple_gate):
    bf = lambda a: a.astype(BF16)
    we = w_in_even[0]
    qw = NSA_HEADS * HEAD_DIM
    kvw = 2 * LANES
    c0 = qw + 3 * kvw
    n_gate = 3 * NSA_HEADS
    s0 = c0 + n_gate
    sbw = SB_HEADS * HEAD_DIM
    even = [
        bf(_pad_heads(we[:, :qw], [h // NSA_GROUP for h in range(NSA_HEADS)])),
        bf(we[:, qw:qw + kvw]),
        bf(we[:, qw + kvw:qw + 2 * kvw]),
        bf(we[:, qw + 2 * kvw:c0]),
        bf(jnp.pad(we[:, c0:s0], ((0, 0), (0, LANES - n_gate)))),
        bf(_pad_heads(we[:, s0:s0 + sbw], [h % 2 for h in range(SB_HEADS)])),
        bf(we[:, s0 + sbw:]),
    ]
    wo = w_in_odd[0]
    mw = MOBA_HEADS * HEAD_DIM
    odd = [bf(_pad_heads(wo[:, :mw], [h % 2 for h in range(MOBA_HEADS)])), bf(wo[:, mw:])]
    w1k, w1v, w2p = _compress_weights(nsa_cmp_w1[0], nsa_cmp_w2[0])
    pe = bf(jnp.broadcast_to(nsa_cmp_pe[0].reshape(2, 1, -1), (2, 8, 2 * NSA_STRIDE * HEAD_DIM)))
    wr = jnp.pad(w_router[0], ((0, 0), (0, LANES - N_EXPERTS)))
    wr_hi, wr_lo = _split(wr)
    return dict(
        even=even, odd=odd, w1k=w1k, w1v=w1v, w2p=w2p, pe=pe, w1=bf(nsa_cmp_w1[0]),
        wo_a=bf(w_out_even[0][:qw]), wo_b=bf(w_out_even[0][qw:]),
        ffn=(bf(w_ffn_gate[0]), bf(w_ffn_up[0]), bf(w_ffn_down[0])),
        wo_c=bf(w_out_odd[0]), wr_hi=wr_hi, wr_lo=wr_lo,
        exp=(bf(w_exp_gate[0]), bf(w_exp_up[0]), bf(w_exp_down[0])),
        ple_proj=bf(w_ple_proj), ple_gate=bf(w_ple_gate))


EVEN_SEGS = ("rope_dual", "plain", ("rope", "none"), ("rope", "none"), ("sigmoid",), "plain",
             "plain")
ODD_SEGS = (("rope",) * 16, ("rope",) * 8 + ("none",) * 8)


def _trunk(x, p, pos0, past, W, norms, sizes):
    b, t, d = x.shape
    n = b * t
    tm, tq, tk_sel, tk_sb, tm_moe, tq_moba, moba_bps = sizes
    norm_mix, norm_ffn, norm_ple, norm_final = norms
    pos = pos0 + jnp.arange(t, dtype=jnp.int32)
    tabs = _rope_tables(pos)
    if t < tm:
        tabs = [jnp.tile(a, (tm // t, 1)) for a in tabs]
    h = x.reshape(n, d)
    row = lambda a: a.reshape(1, d)
    b3 = lambda a: a.reshape(b, t, a.shape[-1])

    qp, qr, cmp_r, sel_r, win_r, gates, sbq, sbkv = _proj(
        h, row(norm_mix[0]), tabs, W["even"], EVEN_SEGS, tm)
    if past is None:
        n_pages = t // PAGE
        ident = jnp.arange(b * n_pages, dtype=jnp.int32).reshape(b, n_pages)
        kc, vc = _compress(cmp_r.reshape(b * n_pages, PAGE, 2 * LANES), ident,
                           W["w1k"], W["w1v"], W["w2p"], W["pe"], W["w1"], 8)
        sel_ctx, win_ctx, sb_ctx = b3(sel_r), b3(win_r), b3(sbkv)
        wpos0 = 0
        win_state = win_ctx[:, -NSA_WINDOW:]
    else:
        pt = past["page_table"]
        tail = lambda a: jnp.pad(b3(a), ((0, 0), (0, MOBA_BLOCK - t), (0, 0)))
        pool = lambda c: c[0].reshape(c.shape[1], PAGE, -1)
        kc, vc = _compress(pool(past["cache_nsa_cmp"]), pt,
                           W["w1k"], W["w1v"], W["w2p"], W["pe"], W["w1"], 8)
        sel_ctx = _gather_ctx(past["cache_nsa_sel"], pt, tail(sel_r), 8)
        sb_ctx = None
        state = past["state_nsa_win"][0].reshape(b, NSA_WINDOW, 2 * LANES)
        win_all = jnp.concatenate([state, b3(win_r)], axis=1)
        win_ctx = jnp.pad(win_all, ((0, 0), (0, LANES - t), (0, 0)))
        wpos0 = pos0 - NSA_WINDOW
        win_state = win_all[:, -NSA_WINDOW:]
    t_c = sel_ctx.shape[1]
    o_a = _nsa_attn(b3(qp), b3(qr), b3(gates), kc, vc, sel_ctx, win_ctx,
                    _overlap_matrix(kc.shape[1]), _expand_matrix(t_c, tk_sel),
                    tq, tk_sel, pos0, wpos0)
    tri = jnp.tril(jnp.ones((tk_sb, tk_sb), BF16), -1)
    if past is None:
        o_b, _ = _sb_attn(b3(sbq), sb_ctx, tri, tq, tk_sb, pos0)
    else:
        n_recent = min(SB_RECENT_PAGES, pt.shape[1])
        recent = _gather_ctx(past["cache_sb"], pt[:, -n_recent:], tail(sbkv), n_recent)
        o_recent, alive = _sb_attn(b3(sbq), recent, tri, tq, tk_sb, pos0,
                                   kpos0=pos0 - n_recent * PAGE)

        def whole_cache():
            ctx = _gather_ctx(past["cache_sb"], pt, tail(sbkv), 8)
            return _sb_attn(b3(sbq), ctx, tri, tq, tk_sb, pos0)[0]

        o_b = lax.cond(alive > 0.0, whole_cache, lambda: o_recent)
    h = _outproj(h, o_a.reshape(n, -1), o_b.reshape(n, -1), W["wo_a"], W["wo_b"], tm)
    h = _ffn(h, row(norm_ffn[0]), *W["ffn"], tm, W["ffn"][0].shape[1] // 2)
    h = _ple(h, p[0].reshape(n, -1), row(norm_ple[0]), row(norm_final),
             W["ple_gate"][0], W["ple_proj"][0], tm, False)

    mq, mkv = _proj(h, row(norm_mix[1]), tabs, W["odd"], ODD_SEGS, tm)
    if past is None:
        moba_ctx = b3(mkv)
    else:
        moba_ctx = _gather_ctx(past["cache_moba"], pt, tail(mkv), 4)
    if tq_moba % LANES == 0:
        o_c = _moba_attn_t(b3(mq), moba_ctx, tq_moba, pos0, moba_bps)
    else:
        o_c = _moba_attn(b3(mq), moba_ctx, tq_moba, pos0, moba_bps)
    half = o_c.shape[-1] // 2
    o_c = o_c.reshape(n, -1)
    h = _outproj(h, o_c[:, :half], o_c[:, half:], W["wo_c"][:half], W["wo_c"][half:], tm)
    h = _moe(h, row(norm_ffn[1]), W["wr_hi"], W["wr_lo"], *W["exp"], tm_moe, 512)
    y = _ple(h, p[1].reshape(n, -1), row(norm_ple[1]), row(norm_final),
             W["ple_gate"][1], W["ple_proj"][1], tm, True)

    kv5 = lambda a, heads: a.reshape(1, b, -1, 2, heads, HEAD_DIM)
    return (y.reshape(b, t, d), kv5(cmp_r, 2), kv5(sel_r, 2), kv5(win_state, 2),
            kv5(sbkv, SB_HEADS), kv5(mkv, MOBA_HEADS))


def kernel(x_prompt, x_sample, cache_nsa_cmp, cache_nsa_sel, state_nsa_win, cache_sb, cache_moba,
           page_table, p_prompt, p_sample, norm_mix, norm_ffn, norm_ple, norm_final, w_in_even,
           w_out_even, nsa_cmp_pe, nsa_cmp_w1, nsa_cmp_w2, w_ffn_gate, w_ffn_up, w_ffn_down,
           w_in_odd, w_out_odd, w_router, w_exp_gate, w_exp_up, w_exp_down, w_ple_proj,
           w_ple_gate):
    W = _prep_weights(w_in_even, w_out_even, nsa_cmp_pe, nsa_cmp_w1, nsa_cmp_w2, w_ffn_gate,
                      w_ffn_up, w_ffn_down, w_in_odd, w_out_odd, w_router, w_exp_gate, w_exp_up,
                      w_exp_down, w_ple_proj, w_ple_gate)
    norms = (norm_mix, norm_ffn, norm_ple, norm_final)
    past = dict(cache_nsa_cmp=cache_nsa_cmp, cache_nsa_sel=cache_nsa_sel,
                state_nsa_win=state_nsa_win, cache_sb=cache_sb, cache_moba=cache_moba,
                page_table=page_table)
    past_len = page_table.shape[1] * cache_sb.shape[2]
    t_dec = x_sample.shape[1]
    n_dec = x_sample.shape[0] * t_dec
    y_p, cmp_p, sel_p, win_p, sb_p, moba_p = _trunk(
        x_prompt, p_prompt, 0, None, W, norms, (256, 128, 512, 128, 1024, MOBA_BLOCK, 4))
    y_s, cmp_s, sel_s, win_s, sb_s, moba_s = _trunk(
        x_sample, p_sample, past_len, past, W, norms, (n_dec, t_dec, 256, 128, n_dec, t_dec, 8))
    return (y_p, y_s, cmp_p, cmp_s, sel_p, sel_s, win_p, win_s, sb_p, sb_s, moba_p, moba_s)
```
